```python
import math
import jax, jax.numpy as jnp
from jax import lax
import numpy as np

D_MODEL = 2048
BATCH = 16
SEQ = 2048
DEPTH = 4

HEAD_DIM = 128
N_HEADS = D_MODEL // HEAD_DIM
DIFF_HEADS = N_HEADS // 4
FOX_HEADS = (N_HEADS - DIFF_HEADS) // 2
SB_HEADS = N_HEADS - DIFF_HEADS - FOX_HEADS
DIFF_QK_DIM = HEAD_DIM // 2
FOX_W = FOX_HEADS * HEAD_DIM
SB_W = SB_HEADS * HEAD_DIM
DIFF_W = DIFF_HEADS * HEAD_DIM
DIFF_QK_W = DIFF_HEADS * 2 * DIFF_QK_DIM
MIX_W = FOX_W + SB_W + DIFF_W
SPLIT_SIZES = (FOX_W, FOX_W, FOX_W, FOX_W,
               SB_W, SB_W, SB_W, SB_W,
               DIFF_QK_W, DIFF_QK_W, DIFF_W, DIFF_W,
               FOX_HEADS)
VALUE_COLS = (False, False, True, False,
              False, False, True, False,
              False, False, True, False,
              False)
IN_W = sum(SPLIT_SIZES)
Q_BLOCK = 128
DEEPNORM_ALPHA = (2 * DEPTH) ** 0.25
DEEPNORM_BETA = (8 * DEPTH) ** -0.25
LN_EPS = 1e-5
SUBLN_EPS = 1e-5
NEG_INF = -1e30

kernel_name = "hybrid_fox_stickbreak_diffattn_deepnorm"


def _split_points():
    pts, acc = [], 0
    for n in SPLIT_SIZES[:-1]:
        acc += n
        pts.append(acc)
    return pts


def _to_blocks(a):
    b, s = a.shape[:2]
    a = a.reshape((b, s // Q_BLOCK, Q_BLOCK) + a.shape[2:])
    return jnp.moveaxis(a, 1, 0)


def _from_blocks(a):
    a = jnp.moveaxis(a, 0, 1)
    return a.reshape((a.shape[0], a.shape[1] * a.shape[2]) + a.shape[3:])


def _layernorm(x, g, b):
    xf = x.astype(jnp.float32)
    mu = jnp.mean(xf, axis=-1, keepdims=True)
    var = jnp.mean(jnp.square(xf - mu), axis=-1, keepdims=True)
    y = (xf - mu) * lax.rsqrt(var + LN_EPS) * g.astype(jnp.float32) + b.astype(jnp.float32)
    return y.astype(x.dtype)


def forgetting_attention(q, k, v, log_f):
    s = q.shape[1]
    pos = jnp.arange(s)
    c = jnp.cumsum(log_f, axis=1)
    ck = jnp.moveaxis(c, 1, 2)
    scale = HEAD_DIM ** -0.5

    def block(args):
        qb, cq, tq = args
        logits = jnp.einsum('bqhd,bkhd->bhqk', qb, k).astype(jnp.float32) * scale
        logits = logits + jnp.moveaxis(cq, 1, 2)[..., None] - ck[:, :, None, :]
        causal = tq[:, None] >= pos[None, :]
        p = jax.nn.softmax(jnp.where(causal, logits, NEG_INF), axis=-1)
        return jnp.einsum('bhqk,bkhd->bqhd', p.astype(v.dtype), v)

    out = lax.map(block, (_to_blocks(q), _to_blocks(c), pos.reshape(-1, Q_BLOCK)))
    return _from_blocks(out)


def stick_breaking_attention(q, k, v):
    s = q.shape[1]
    pos = jnp.arange(s)
    scale = HEAD_DIM ** -0.5

    def block(args):
        qb, tq = args
        z = jnp.einsum('bqhd,bkhd->bhqk', qb, k).astype(jnp.float32) * scale
        strict = tq[:, None] > pos[None, :]
        log_beta = jax.nn.log_sigmoid(z)
        log_keep = jnp.where(strict, jax.nn.log_sigmoid(-z), 0.0)
        later = lax.cumsum(log_keep, axis=3, reverse=True) - log_keep
        w = jnp.where(strict, jnp.exp(log_beta + later), 0.0)
        return jnp.einsum('bhqk,bkhd->bqhd', w.astype(v.dtype), v)

    out = lax.map(block, (_to_blocks(q), pos.reshape(-1, Q_BLOCK)))
    return _from_blocks(out)


def differential_attention(q, k, v, lam, slopes):
    s = q.shape[1]
    pos = jnp.arange(s)
    scale = DIFF_QK_DIM ** -0.5

    def block(args):
        qb, tq = args
        logits = jnp.einsum('bqhcd,bkhcd->bhcqk', qb, k).astype(jnp.float32) * scale
        dist = (tq[:, None] - pos[None, :]).astype(jnp.float32)
        logits = logits - slopes[None, :, None, None, None] * dist
        causal = tq[:, None] >= pos[None, :]
        p = jax.nn.softmax(jnp.where(causal, logits, NEG_INF), axis=-1)
        attn = p[:, :, 0] - lam * p[:, :, 1]
        return jnp.einsum('bhqk,bkhd->bqhd', attn.astype(v.dtype), v)

    out = lax.map(block, (_to_blocks(q), pos.reshape(-1, Q_BLOCK)))
    return _from_blocks(out)


def hybrid_layer(x, w_in, b_f, lam_p, subln_g, w_out, ln_g, ln_b, layer_idx):
    b, s, _ = x.shape
    h = x @ w_in
    (fq, fk, fv, fg, sq, sk, sv, sg, dq, dk, dv, dg, ff) = jnp.split(h, _split_points(), axis=-1)

    log_f = jax.nn.log_sigmoid(ff.astype(jnp.float32) + b_f.astype(jnp.float32))
    hs = (b, s, FOX_HEADS, HEAD_DIM)
    o_fox = forgetting_attention(fq.reshape(hs), fk.reshape(hs), fv.reshape(hs), log_f)
    o_fox = o_fox.reshape(b, s, FOX_W) * jax.nn.silu(fg)

    hs = (b, s, SB_HEADS, HEAD_DIM)
    o_sb = stick_breaking_attention(sq.reshape(hs), sk.reshape(hs), sv.reshape(hs))
    o_sb = o_sb.reshape(b, s, SB_W) * jax.nn.silu(sg)

    lam_init = 0.8 - 0.6 * math.exp(-0.3 * layer_idx)
    lp = lam_p.astype(jnp.float32)
    lam = jnp.exp(jnp.sum(lp[0] * lp[1])) - jnp.exp(jnp.sum(lp[2] * lp[3])) + lam_init
    slopes = jnp.exp2(-8.0 * jnp.arange(1, DIFF_HEADS + 1, dtype=jnp.float32) / DIFF_HEADS)
    qs = (b, s, DIFF_HEADS, 2, DIFF_QK_DIM)
    o_d = differential_attention(dq.reshape(qs), dk.reshape(qs),
                                 dv.reshape(b, s, DIFF_HEADS, HEAD_DIM), lam, slopes)
    of = o_d.astype(jnp.float32)
    of = of * lax.rsqrt(jnp.mean(jnp.square(of), axis=-1, keepdims=True) + SUBLN_EPS)
    of = of * subln_g.astype(jnp.float32) * (1.0 - lam_init)
    o_d = of.astype(x.dtype).reshape(b, s, DIFF_W) * jax.nn.silu(dg)

    y = jnp.concatenate([o_fox, o_sb, o_d], axis=-1) @ w_out
    return _layernorm(DEEPNORM_ALPHA * x + y, ln_g, ln_b)


def setup_inputs(seed: int = 0) -> dict:
    key = jax.random.key(seed)
    ks = jax.random.split(key, 8)
    x = jax.random.normal(ks[0], (BATCH, SEQ, D_MODEL), jnp.float32)
    col_scale = jnp.concatenate([
        jnp.full((n,), DEEPNORM_BETA if is_v else 1.0, jnp.float32)
        for n, is_v in zip(SPLIT_SIZES, VALUE_COLS)])
    w_in = jax.random.normal(ks[1], (DEPTH, D_MODEL, IN_W), jnp.float32) * (D_MODEL ** -0.5) * col_scale
    b_f = jax.random.uniform(ks[2], (DEPTH, FOX_HEADS), jnp.float32, 1.0, 5.0)
    diff_lambda = 0.1 * jax.random.normal(ks[3], (DEPTH, 4, DIFF_QK_DIM), jnp.float32)
    diff_subln_g = 1.0 + 0.02 * jax.random.normal(ks[4], (DEPTH, HEAD_DIM), jnp.float32)
    w_out = jax.random.normal(ks[5], (DEPTH, MIX_W, D_MODEL), jnp.float32) * (MIX_W ** -0.5) * DEEPNORM_BETA
    ln_g = 1.0 + 0.02 * jax.random.normal(ks[6], (DEPTH, D_MODEL), jnp.float32)
    ln_b = 0.02 * jax.random.normal(ks[7], (DEPTH, D_MODEL), jnp.float32)
    return {"x": x, "w_in": w_in, "b_f": b_f, "diff_lambda": diff_lambda,
            "diff_subln_g": diff_subln_g, "w_out": w_out, "ln_g": ln_g, "ln_b": ln_b}


def reference(x, w_in, b_f, diff_lambda, diff_subln_g, w_out, ln_g, ln_b):
    for l in range(DEPTH):
        x = hybrid_layer(x, w_in[l], b_f[l], diff_lambda[l], diff_subln_g[l],
                         w_out[l], ln_g[l], ln_b[l], l)
    return x
```

```python
import functools
import math

import jax
import jax.numpy as jnp
from jax import lax
from jax.experimental import pallas as pl
from jax.experimental.pallas import tpu as pltpu

HEAD_DIM = 128
FOX_HEADS = 6
SB_HEADS = 6
DIFF_HEADS = 4
DIFF_QK_DIM = HEAD_DIM // 2
FOX_W = FOX_HEADS * HEAD_DIM
SB_W = SB_HEADS * HEAD_DIM
DIFF_W = DIFF_HEADS * HEAD_DIM
DEPTH_FOR_DEEPNORM = 4
DEEPNORM_ALPHA = (2 * DEPTH_FOR_DEEPNORM) ** 0.25
LN_EPS = 1e-5
SUBLN_EPS = 1e-5
NEG_BIG = -1e30

LANES = 128
VMEM_LIMIT_BYTES = 56 * 1024 * 1024

QKV_BLOCKS = 3 * FOX_HEADS + 3 * SB_HEADS + 3 * DIFF_HEADS
FOX_Q0, FOX_K0, FOX_V0 = 0, FOX_HEADS, 2 * FOX_HEADS
SB_Q0 = 3 * FOX_HEADS
SB_K0, SB_V0 = SB_Q0 + SB_HEADS, SB_Q0 + 2 * SB_HEADS
DIFF_Q0 = SB_Q0 + 3 * SB_HEADS
DIFF_K0, DIFF_V0 = DIFF_Q0 + DIFF_HEADS, DIFF_Q0 + 2 * DIFF_HEADS
GATE_FOX0, GATE_SB0, GATE_DIFF0 = 0, FOX_HEADS, FOX_HEADS + SB_HEADS
GATE_FF = FOX_HEADS + SB_HEADS + DIFF_HEADS
GATE_BLOCKS = GATE_FF + 1


def _params(*semantics):
    return pltpu.CompilerParams(dimension_semantics=semantics, vmem_limit_bytes=VMEM_LIMIT_BYTES)


def _proj_kernel(x_ref, w_ref, s_ref, o_ref):
    acc = jnp.dot(x_ref[...], w_ref[...], preferred_element_type=jnp.float32)
    acc = acc * s_ref[...]
    for c in range(o_ref.shape[0]):
        o_ref[c] = acc[:, c * LANES:(c + 1) * LANES].astype(o_ref.dtype)


def _project(x, w, col_scale, out_dtype, tm, tn):
    t, k = x.shape
    n = w.shape[1]
    return pl.pallas_call(
        _proj_kernel,
        grid=(t // tm, n // tn),
        in_specs=[
            pl.BlockSpec((tm, k), lambda i, j: (i, 0)),
            pl.BlockSpec((k, tn), lambda i, j: (0, j)),
            pl.BlockSpec((1, tn), lambda i, j: (0, j)),
        ],
        out_specs=pl.BlockSpec((tn // LANES, tm, LANES), lambda i, j: (j, i, 0)),
        out_shape=jax.ShapeDtypeStruct((n // LANES, t, LANES), out_dtype),
        compiler_params=_params("parallel", "parallel"),
    )(x, w, col_scale)


def _log_sigmoid(z):
    return jnp.minimum(z, 0.0) - jnp.log1p(jnp.exp(-jnp.abs(z)))


def _split3(x):
    x1 = x.astype(jnp.bfloat16)
    r = x - x1.astype(jnp.float32)
    x2 = r.astype(jnp.bfloat16)
    x3 = (r - x2.astype(jnp.float32)).astype(jnp.bfloat16)
    return x1, x2, x3


def _decay_kernel(ff_ref, bf_ref, tri_ref, c_ref, *, chunk):
    s = ff_ref.shape[1]
    lf = _log_sigmoid(ff_ref[0] + bf_ref[...])
    lft = lf.T[0:8, :]
    tri = tri_ref[...]
    carry = jnp.zeros((8, 1), jnp.float32)
    for c in range(s // chunk):
        x1, x2, x3 = _split3(lft[:, c * chunk:(c + 1) * chunk])
        cs = (jnp.dot(x1, tri, preferred_element_type=jnp.float32)
              + jnp.dot(x2, tri, preferred_element_type=jnp.float32)
              + jnp.dot(x3, tri, preferred_element_type=jnp.float32)) + carry
        c_ref[0, :, c * chunk:(c + 1) * chunk] = cs
        carry = cs[:, chunk - 1:chunk]


def _forget_prefix(gates, b_f_row, batch, seq, chunk):
    tri = (lax.broadcasted_iota(jnp.int32, (chunk, chunk), 0)
           <= lax.broadcasted_iota(jnp.int32, (chunk, chunk), 1)).astype(jnp.bfloat16)
    return pl.pallas_call(
        functools.partial(_decay_kernel, chunk=chunk),
        grid=(batch,),
        in_specs=[
            pl.BlockSpec((1, seq, LANES), lambda b: (GATE_FF, b, 0)),
            pl.BlockSpec((1, LANES), lambda b: (0, 0)),
            pl.BlockSpec((chunk, chunk), lambda b: (0, 0)),
        ],
        out_specs=pl.BlockSpec((1, 8, seq), lambda b: (b, 0, 0)),
        out_shape=jax.ShapeDtypeStruct((batch, 8, seq), jnp.float32),
        compiler_params=_params("parallel"),
    )(gates, b_f_row, tri)


def _qk(q, k):
    return lax.dot_general(q, k, (((1,), (1,)), ((), ())), preferred_element_type=jnp.float32)


def _silu(g):
    return g * (1.0 / (1.0 + jnp.exp(-g)))


def _softmax_step(s, v, m, l, acc):
    m_new = jnp.maximum(m, jnp.max(s, axis=1, keepdims=True))
    alpha = jnp.exp(m - m_new)
    p = jnp.exp(s - m_new)
    l = alpha * l + jnp.sum(p, axis=1, keepdims=True)
    acc = alpha * acc + jnp.dot(p.astype(v.dtype), v, preferred_element_type=jnp.float32)
    return m_new, l, acc


def _attn_specs(q0, k0, v0, g0, seq, batch_blocks):
    del batch_blocks
    blk = lambda base: pl.BlockSpec((1, seq, LANES), lambda b, h: (base + h, b, 0))
    return blk(q0), blk(k0), blk(v0), blk(g0)


def _fox_kernel(q_ref, k_ref, v_ref, g_ref, ck_ref, o_ref, *, tq, tk):
    seq = q_ref.shape[1]
    nq = seq // tq
    ratio = tq // tk
    causal = (lax.broadcasted_iota(jnp.int32, (tq, tk), 0)
              >= lax.broadcasted_iota(jnp.int32, (tq, tk), 1))

    def q_tile(qi, _):
        r0 = pl.multiple_of(qi * tq, tq)
        q = q_ref[0, pl.ds(r0, tq), :]

        def scores(j):
            c0 = pl.multiple_of(j * tk, tk)
            k = k_ref[0, pl.ds(c0, tk), :]
            v = v_ref[0, pl.ds(c0, tk), :]
            return _qk(q, k) - ck_ref[0, j], v

        def body(j, carry):
            s, v = scores(j)
            return _softmax_step(s, v, *carry)

        init = (jnp.full((tq, 1), NEG_BIG, jnp.float32), jnp.zeros((tq, 1), jnp.float32),
                jnp.zeros((tq, HEAD_DIM), jnp.float32))
        carry = lax.fori_loop(0, qi * ratio, body, init)
        for d in range(ratio):
            s, v = scores(qi * ratio + d)
            row_shift = d * tk
            mask = causal if row_shift == 0 else (
                lax.broadcasted_iota(jnp.int32, (tq, tk), 0) - row_shift
                >= lax.broadcasted_iota(jnp.int32, (tq, tk), 1))
            carry = _softmax_step(jnp.where(mask, s, NEG_BIG), v, *carry)
        _, l, acc = carry
        out = (acc / l) * _silu(g_ref[0, pl.ds(r0, tq), :])
        o_ref[pl.ds(r0, tq), :] = out.astype(o_ref.dtype)
        return 0

    lax.fori_loop(0, nq, q_tile, 0)


def _fox_attention(qkv, gates, ck, batch, seq, tq, tk):
    q_s, k_s, v_s, g_s = _attn_specs(FOX_Q0, FOX_K0, FOX_V0, GATE_FOX0, seq, batch)
    nk = seq // tk
    return pl.pallas_call(
        functools.partial(_fox_kernel, tq=tq, tk=tk),
        grid=(batch, FOX_HEADS),
        in_specs=[q_s, k_s, v_s, g_s,
                  pl.BlockSpec((1, nk, 1, tk), lambda b, h: (b * FOX_HEADS + h, 0, 0, 0))],
        out_specs=pl.BlockSpec((seq, LANES), lambda b, h: (b, h)),
        out_shape=jax.ShapeDtypeStruct((batch * seq, FOX_W), jnp.bfloat16),
        compiler_params=_params("parallel", "parallel"),
    )(qkv, qkv, qkv, gates, ck)


def _sb_block(q, k, v, tri2, carry, acc, strict):
    z = _qk(q, k)
    sp = jnp.log1p(jnp.exp(-jnp.abs(z)))
    log_beta = jnp.minimum(z, 0.0) - sp
    log_keep = log_beta - z
    if strict is not None:
        log_keep = jnp.where(strict, log_keep, 0.0)
    hi = log_keep.astype(jnp.bfloat16)
    lo = (log_keep - hi.astype(jnp.float32)).astype(jnp.bfloat16)
    later = jnp.dot(jnp.concatenate([hi, lo], axis=1), tri2,
                    preferred_element_type=jnp.float32) + carry
    w = jnp.exp(log_beta + later)
    if strict is not None:
        w = jnp.where(strict, w, 0.0)
    acc = acc + jnp.dot(w.astype(v.dtype), v, preferred_element_type=jnp.float32)
    carry = later[:, 0:1] + log_keep[:, 0:1]
    return carry, acc


def _sb_kernel(q_ref, k_ref, v_ref, g_ref, tri_ref, o_ref, *, tq):
    seq = q_ref.shape[1]
    nq = seq // tq
    tk = tq
    tri2 = tri_ref[...]
    strict = (lax.broadcasted_iota(jnp.int32, (tq, tk), 0)
              > lax.broadcasted_iota(jnp.int32, (tq, tk), 1))

    def q_tile(qi, _):
        r0 = pl.multiple_of(qi * tq, tq)
        q = q_ref[0, pl.ds(r0, tq), :]
        carry, acc = _sb_block(q, k_ref[0, pl.ds(r0, tk), :], v_ref[0, pl.ds(r0, tk), :], tri2,
                               jnp.zeros((tq, 1), jnp.float32),
                               jnp.zeros((tq, HEAD_DIM), jnp.float32), strict)

        def body(i, state):
            c0 = pl.multiple_of((qi - 1 - i) * tk, tk)
            return _sb_block(q, k_ref[0, pl.ds(c0, tk), :], v_ref[0, pl.ds(c0, tk), :], tri2,
                             state[0], state[1], None)

        _, acc = lax.fori_loop(0, qi, body, (carry, acc))
        out = acc * _silu(g_ref[0, pl.ds(r0, tq), :])
        o_ref[pl.ds(r0, tq), :] = out.astype(o_ref.dtype)
        return 0

    lax.fori_loop(0, nq, q_tile, 0)


def _sb_attention(qkv, gates, batch, seq, tq):
    q_s, k_s, v_s, g_s = _attn_specs(SB_Q0, SB_K0, SB_V0, GATE_SB0, seq, batch)
    upper = (lax.broadcasted_iota(jnp.int32, (tq, tq), 0)
             > lax.broadcasted_iota(jnp.int32, (tq, tq), 1)).astype(jnp.bfloat16)
    tri2 = jnp.concatenate([upper, upper], axis=0)
    return pl.pallas_call(
        functools.partial(_sb_kernel, tq=tq),
        grid=(batch, SB_HEADS),
        in_specs=[q_s, k_s, v_s, g_s, pl.BlockSpec((2 * tq, tq), lambda b, h: (0, 0))],
        out_specs=pl.BlockSpec((seq, LANES), lambda b, h: (b, h)),
        out_shape=jax.ShapeDtypeStruct((batch * seq, SB_W), jnp.bfloat16),
        compiler_params=_params("parallel", "parallel"),
    )(qkv, qkv, qkv, gates, tri2)


def _diff_kernel(lam_ref, gsub_ref, q_ref, k_ref, v_ref, g_ref, o_ref, *, tq, tk, lam_init):
    seq = q_ref.shape[1]
    nq = seq // tq
    ratio = tq // tk
    h = pl.program_id(1)
    lp = lam_ref[...]
    lam = (jnp.exp(jnp.sum(lp[0:1] * lp[1:2], axis=1, keepdims=True))
           - jnp.exp(jnp.sum(lp[2:3] * lp[3:4], axis=1, keepdims=True)) + lam_init)
    expo = jnp.full((1, tk), 127 - (8 // DIFF_HEADS) * (h + 1), jnp.int32)
    slope = lax.bitcast_convert_type(expo << 23, jnp.float32)
    col = lax.broadcasted_iota(jnp.int32, (1, tk), 1)
    lane = lax.broadcasted_iota(jnp.int32, (tq, HEAD_DIM), 1)
    rows2 = lax.broadcasted_iota(jnp.int32, (2 * tq, tk), 0)
    rows2 = jnp.where(rows2 >= tq, rows2 - tq, rows2)
    cols2 = lax.broadcasted_iota(jnp.int32, (2 * tq, tk), 1)

    def q_tile(qi, _):
        r0 = pl.multiple_of(qi * tq, tq)
        q = q_ref[0, pl.ds(r0, tq), :]
        zero = jnp.zeros_like(q)
        qq = jnp.concatenate([jnp.where(lane < DIFF_QK_DIM, q, zero),
                              jnp.where(lane >= DIFF_QK_DIM, q, zero)], axis=0)

        def scores(j):
            c0 = pl.multiple_of(j * tk, tk)
            k = k_ref[0, pl.ds(c0, tk), :]
            v = v_ref[0, pl.ds(c0, tk), :]
            bias = slope * (col + (c0 - r0)).astype(jnp.float32)
            return _qk(qq, k) + bias, v

        def body(j, carry):
            s, v = scores(j)
            return _softmax_step(s, v, *carry)

        init = (jnp.full((2 * tq, 1), NEG_BIG, jnp.float32), jnp.zeros((2 * tq, 1), jnp.float32),
                jnp.zeros((2 * tq, HEAD_DIM), jnp.float32))
        carry = lax.fori_loop(0, qi * ratio, body, init)
        for d in range(ratio):
            s, v = scores(qi * ratio + d)
            mask = rows2 - d * tk >= cols2
            carry = _softmax_step(jnp.where(mask, s, NEG_BIG), v, *carry)
        _, l, acc = carry
        o = acc / l
        o = o[0:tq] - lam * o[tq:2 * tq]
        o = o * lax.rsqrt(jnp.mean(o * o, axis=1, keepdims=True) + SUBLN_EPS)
        o = o * gsub_ref[...] * (1.0 - lam_init)
        out = o * _silu(g_ref[0, pl.ds(r0, tq), :])
        o_ref[pl.ds(r0, tq), :] = out.astype(o_ref.dtype)
        return 0

    lax.fori_loop(0, nq, q_tile, 0)


def _diff_attention(qkv, gates, lam_p, subln_row, batch, seq, tq, tk, lam_init):
    q_s, k_s, v_s, g_s = _attn_specs(DIFF_Q0, DIFF_K0, DIFF_V0, GATE_DIFF0, seq, batch)
    return pl.pallas_call(
        functools.partial(_diff_kernel, tq=tq, tk=tk, lam_init=lam_init),
        grid=(batch, DIFF_HEADS),
        in_specs=[pl.BlockSpec(lam_p.shape, lambda b, h: (0, 0)),
                  pl.BlockSpec((1, HEAD_DIM), lambda b, h: (0, 0)),
                  q_s, k_s, v_s, g_s],
        out_specs=pl.BlockSpec((seq, LANES), lambda b, h: (b, h)),
        out_shape=jax.ShapeDtypeStruct((batch * seq, DIFF_W), jnp.bfloat16),
        compiler_params=_params("parallel", "parallel"),
    )(lam_p, subln_row, qkv, qkv, qkv, gates)


def _merge_kernel(of_ref, os_ref, od_ref, wf_ref, ws_ref, wd_ref, x_ref, g_ref, b_ref, y_ref, yb_ref):
    y = (jnp.dot(of_ref[...], wf_ref[...], preferred_element_type=jnp.float32)
         + jnp.dot(os_ref[...], ws_ref[...], preferred_element_type=jnp.float32)
         + jnp.dot(od_ref[...], wd_ref[...], preferred_element_type=jnp.float32))
    z = DEEPNORM_ALPHA * x_ref[...] + y
    mu = jnp.mean(z, axis=1, keepdims=True)
    zc = z - mu
    var = jnp.mean(zc * zc, axis=1, keepdims=True)
    out = zc * lax.rsqrt(var + LN_EPS) * g_ref[...] + b_ref[...]
    y_ref[...] = out
    yb_ref[...] = out.astype(yb_ref.dtype)


def _merge(o_fox, o_sb, o_diff, w_f, w_s, w_d, x, ln_g, ln_b, tm):
    t, d = x.shape
    row = lambda w: pl.BlockSpec((tm, w), lambda i: (i, 0))
    full = lambda a: pl.BlockSpec(a.shape, lambda i: (0, 0))
    return pl.pallas_call(
        _merge_kernel,
        grid=(t // tm,),
        in_specs=[row(FOX_W), row(SB_W), row(DIFF_W), full(w_f), full(w_s), full(w_d),
                  row(d), full(ln_g), full(ln_b)],
        out_specs=[row(d), row(d)],
        out_shape=[jax.ShapeDtypeStruct((t, d), jnp.float32),
                   jax.ShapeDtypeStruct((t, d), jnp.bfloat16)],
        compiler_params=_params("parallel"),
    )(o_fox, o_sb, o_diff, w_f, w_s, w_d, x, ln_g, ln_b)


def _split_in_weights(w_in):
    sizes = (FOX_W,) * 4 + (SB_W,) * 4 + (DIFF_W,) * 4 + (FOX_HEADS,)
    offs = [0]
    for n in sizes:
        offs.append(offs[-1] + n)
    part = lambda i: w_in[:, :, offs[i]:offs[i + 1]]
    fq, fk, fv, fg, sq, sk, sv, sg, dq, dk, dv, dg, ff = (part(i) for i in range(13))
    w_qkv = jnp.concatenate([fq, fk, fv, sq, sk, sv, dq, dk, dv], axis=-1).astype(jnp.bfloat16)
    ff = jnp.pad(ff, ((0, 0), (0, 0), (0, LANES - FOX_HEADS)))
    w_gate = jnp.concatenate([fg, sg, dg, ff], axis=-1).astype(jnp.bfloat16)
    return w_qkv, w_gate


def _qkv_col_scale():
    one = lambda n: jnp.ones((n,), jnp.float32)
    full = lambda n, v: jnp.full((n,), v, jnp.float32)
    return jnp.concatenate([
        full(FOX_W, HEAD_DIM ** -0.5), one(2 * FOX_W),
        full(SB_W, HEAD_DIM ** -0.5), one(2 * SB_W),
        full(DIFF_W, DIFF_QK_DIM ** -0.5), one(2 * DIFF_W)])[None, :]


def kernel(x, w_in, b_f, diff_lambda, diff_subln_g, w_out, ln_g, ln_b):
    batch, seq, d_model = x.shape
    depth = w_in.shape[0]
    t = batch * seq
    tq = min(seq, 256)
    tm = min(t, 1024)

    w_qkv, w_gate = _split_in_weights(w_in)
    w_out_b = w_out.astype(jnp.bfloat16)
    qkv_scale = _qkv_col_scale()
    gate_scale = jnp.ones((1, GATE_BLOCKS * LANES), jnp.float32)
    b_f_rows = jnp.pad(b_f, ((0, 0), (0, LANES - FOX_HEADS)))

    xf = x.reshape(t, d_model)
    xb = xf.astype(jnp.bfloat16)
    for l in range(depth):
        lam_init = 0.8 - 0.6 * math.exp(-0.3 * l)
        qkv = _project(xb, w_qkv[l], qkv_scale, jnp.bfloat16, tm, 1024)
        gates = _project(xb, w_gate[l], gate_scale, jnp.float32, min(t, 512), GATE_BLOCKS * LANES)
        c = _forget_prefix(gates, b_f_rows[l:l + 1], batch, seq, tq)
        ck = c[:, :FOX_HEADS, :].reshape(batch * FOX_HEADS, seq // tq, 1, tq)
        o_fox = _fox_attention(qkv, gates, ck, batch, seq, tq, tq)
        o_sb = _sb_attention(qkv, gates, batch, seq, tq)
        o_diff = _diff_attention(qkv, gates, diff_lambda[l], diff_subln_g[l:l + 1], batch, seq, tq, tq,
                                 lam_init)
        wl = w_out_b[l]
        xf, xb = _merge(o_fox, o_sb, o_diff, wl[:FOX_W], wl[FOX_W:FOX_W + SB_W], wl[FOX_W + SB_W:],
                        xf, ln_g[l:l + 1], ln_b[l:l + 1], min(t, 512))
    return xf.reshape(batch, seq, d_model)
```

```python
import functools
import math

import jax
import jax.numpy as jnp
from jax import lax
from jax.experimental import pallas as pl
from jax.experimental.pallas import tpu as pltpu

HEAD_DIM = 128
FOX_HEADS = 6
SB_HEADS = 6
DIFF_HEADS = 4
DIFF_QK_DIM = HEAD_DIM // 2
FOX_W = FOX_HEADS * HEAD_DIM
SB_W = SB_HEADS * HEAD_DIM
DIFF_W = DIFF_HEADS * HEAD_DIM
DEPTH_FOR_DEEPNORM = 4
DEEPNORM_ALPHA = (2 * DEPTH_FOR_DEEPNORM) ** 0.25
LN_EPS = 1e-5
SUBLN_EPS = 1e-5
NEG_BIG = -1e30

LANES = 128
VMEM_LIMIT_BYTES = 56 * 1024 * 1024

QKV_BLOCKS = 3 * FOX_HEADS + 3 * SB_HEADS + 3 * DIFF_HEADS
FOX_Q0, FOX_K0, FOX_V0 = 0, FOX_HEADS, 2 * FOX_HEADS
SB_Q0 = 3 * FOX_HEADS
SB_K0, SB_V0 = SB_Q0 + SB_HEADS, SB_Q0 + 2 * SB_HEADS
DIFF_Q0 = SB_Q0 + 3 * SB_HEADS
DIFF_K0, DIFF_V0 = DIFF_Q0 + DIFF_HEADS, DIFF_Q0 + 2 * DIFF_HEADS
GATE_FOX0, GATE_SB0, GATE_DIFF0 = 0, FOX_HEADS, FOX_HEADS + SB_HEADS
GATE_FF = FOX_HEADS + SB_HEADS + DIFF_HEADS
GATE_BLOCKS = GATE_FF + 1
FOX_GROUP, SB_GROUP, DIFF_GROUP = 3, 3, 2


def _params(*semantics):
    return pltpu.CompilerParams(dimension_semantics=semantics, vmem_limit_bytes=VMEM_LIMIT_BYTES)


def _proj_kernel(x_ref, w_ref, s_ref, o_ref):
    acc = jnp.dot(x_ref[...], w_ref[...], preferred_element_type=jnp.float32)
    acc = acc * s_ref[...]
    for c in range(o_ref.shape[0]):
        o_ref[c] = acc[:, c * LANES:(c + 1) * LANES].astype(o_ref.dtype)


def _project(x, w, col_scale, out_dtype, tm, tn, name):
    t, k = x.shape
    n = w.shape[1]
    return pl.pallas_call(
        _proj_kernel,
        grid=(t // tm, n // tn),
        in_specs=[
            pl.BlockSpec((tm, k), lambda i, j: (i, 0)),
            pl.BlockSpec((k, tn), lambda i, j: (0, j)),
            pl.BlockSpec((1, tn), lambda i, j: (0, j)),
        ],
        out_specs=pl.BlockSpec((tn // LANES, tm, LANES), lambda i, j: (j, i, 0)),
        out_shape=jax.ShapeDtypeStruct((n // LANES, t, LANES), out_dtype),
        compiler_params=_params("parallel", "parallel"),
        name=name,
    )(x, w, col_scale)


def _log_sigmoid(z):
    return jnp.minimum(z, 0.0) - jnp.log1p(jnp.exp(-jnp.abs(z)))


def _split3(x):
    x1 = x.astype(jnp.bfloat16)
    r = x - x1.astype(jnp.float32)
    x2 = r.astype(jnp.bfloat16)
    x3 = (r - x2.astype(jnp.float32)).astype(jnp.bfloat16)
    return x1, x2, x3


def _decay_kernel(ff_ref, bf_ref, tri_ref, c_ref, *, chunk):
    s = ff_ref.shape[1]
    lf = _log_sigmoid(ff_ref[0] + bf_ref[...])
    lft = lf.T[0:8, :]
    tri = tri_ref[...]
    carry = jnp.zeros((8, 1), jnp.float32)
    for c in range(s // chunk):
        x1, x2, x3 = _split3(lft[:, c * chunk:(c + 1) * chunk])
        cs = (jnp.dot(x1, tri, preferred_element_type=jnp.float32)
              + jnp.dot(x2, tri, preferred_element_type=jnp.float32)
              + jnp.dot(x3, tri, preferred_element_type=jnp.float32)) + carry
        c_ref[0, :, c * chunk:(c + 1) * chunk] = cs
        carry = cs[:, chunk - 1:chunk]


def _forget_prefix(gates, b_f_row, batch, seq, chunk):
    tri = (lax.broadcasted_iota(jnp.int32, (chunk, chunk), 0)
           <= lax.broadcasted_iota(jnp.int32, (chunk, chunk), 1)).astype(jnp.bfloat16)
    return pl.pallas_call(
        functools.partial(_decay_kernel, chunk=chunk),
        grid=(batch,),
        in_specs=[
            pl.BlockSpec((1, seq, LANES), lambda b: (GATE_FF, b, 0)),
            pl.BlockSpec((1, LANES), lambda b: (0, 0)),
            pl.BlockSpec((chunk, chunk), lambda b: (0, 0)),
        ],
        out_specs=pl.BlockSpec((1, 8, seq), lambda b: (b, 0, 0)),
        out_shape=jax.ShapeDtypeStruct((batch, 8, seq), jnp.float32),
        compiler_params=_params("parallel"),
        name="forget_prefix",
    )(gates, b_f_row, tri)


def _qk(q, k):
    return lax.dot_general(q, k, (((1,), (1,)), ((), ())), preferred_element_type=jnp.float32)


def _silu(g):
    return g * (1.0 / (1.0 + jnp.exp(-g)))


def _softmax_steps(svs, carry, mask=None):
    stats = []
    for (s, _), (m, l, _) in zip(svs, carry):
        if mask is not None:
            s = jnp.where(mask, s, NEG_BIG)
        m_new = jnp.maximum(m, jnp.max(s, axis=1, keepdims=True))
        alpha = jnp.exp(m - m_new)
        p = jnp.exp(s - m_new)
        stats.append((m_new, alpha, alpha * l + jnp.sum(p, axis=1, keepdims=True), p))
    out = []
    for (_, v), (_, _, acc), (m_new, alpha, l, p) in zip(svs, carry, stats):
        acc = alpha * acc + jnp.dot(p.astype(v.dtype), v, preferred_element_type=jnp.float32)
        out.append((m_new, l, acc))
    return tuple(out)


def _attn_specs(q0, k0, v0, g0, seq, heads):
    blk = lambda base: pl.BlockSpec((heads, seq, LANES), lambda b, hg: (base // heads + hg, b, 0))
    return blk(q0), blk(k0), blk(v0), blk(g0)


def _fox_kernel(q_ref, k_ref, v_ref, g_ref, ck_ref, o_ref, *, tq, tk):
    heads, seq = q_ref.shape[0], q_ref.shape[1]
    nq = seq // tq
    ratio = tq // tk
    rows = lax.broadcasted_iota(jnp.int32, (tq, tk), 0)
    cols = lax.broadcasted_iota(jnp.int32, (tq, tk), 1)

    def q_tile(qi, _):
        r0 = pl.multiple_of(qi * tq, tq)
        qs = [q_ref[g, pl.ds(r0, tq), :] for g in range(heads)]

        def scores(g, j):
            c0 = pl.multiple_of(j * tk, tk)
            k = k_ref[g, pl.ds(c0, tk), :]
            v = v_ref[g, pl.ds(c0, tk), :]
            return _qk(qs[g], k) - ck_ref[g, j], v

        def body(j, carry):
            return _softmax_steps([scores(g, j) for g in range(heads)], carry)

        init = (jnp.full((tq, 1), NEG_BIG, jnp.float32), jnp.zeros((tq, 1), jnp.float32),
                jnp.zeros((tq, HEAD_DIM), jnp.float32))
        carry = lax.fori_loop(0, qi * ratio, body, (init,) * heads)
        for d in range(ratio):
            carry = _softmax_steps([scores(g, qi * ratio + d) for g in range(heads)], carry,
                                   rows - d * tk >= cols)
        for g in range(heads):
            _, l, acc = carry[g]
            out = (acc / l) * _silu(g_ref[g, pl.ds(r0, tq), :])
            o_ref[pl.ds(r0, tq), g * LANES:(g + 1) * LANES] = out.astype(o_ref.dtype)
        return 0

    lax.fori_loop(0, nq, q_tile, 0)


def _fox_attention(qkv, gates, ck, batch, seq, tq, tk, heads):
    q_s, k_s, v_s, g_s = _attn_specs(FOX_Q0, FOX_K0, FOX_V0, GATE_FOX0, seq, heads)
    nk = seq // tk
    groups = FOX_HEADS // heads
    return pl.pallas_call(
        functools.partial(_fox_kernel, tq=tq, tk=tk),
        grid=(batch, groups),
        in_specs=[q_s, k_s, v_s, g_s,
                  pl.BlockSpec((heads, nk, 1, tk), lambda b, hg: (b * groups + hg, 0, 0, 0))],
        out_specs=pl.BlockSpec((seq, heads * LANES), lambda b, hg: (b, hg)),
        out_shape=jax.ShapeDtypeStruct((batch * seq, FOX_W), jnp.bfloat16),
        compiler_params=_params("parallel", "parallel"),
        name="fox_attention",
    )(qkv, qkv, qkv, gates, ck)


def _sb_blocks(qs, ks, vs, tri2, states, strict):
    zs = [_qk(q, k) for q, k in zip(qs, ks)]
    parts = []
    for z in zs:
        sp = jnp.log1p(jnp.exp(-jnp.abs(z)))
        log_beta = jnp.minimum(z, 0.0) - sp
        log_keep = log_beta - z
        if strict is not None:
            log_keep = jnp.where(strict, log_keep, 0.0)
        hi = log_keep.astype(jnp.bfloat16)
        lo = (log_keep - hi.astype(jnp.float32)).astype(jnp.bfloat16)
        parts.append((log_beta, log_keep, jnp.concatenate([hi, lo], axis=1)))
    laters = [jnp.dot(hl, tri2, preferred_element_type=jnp.float32) for _, _, hl in parts]
    ws = []
    for (log_beta, _, _), later, (carry, _) in zip(parts, laters, states):
        w = jnp.exp(log_beta + (later + carry))
        if strict is not None:
            w = jnp.where(strict, w, 0.0)
        ws.append(w)
    out = []
    for w, v, (_, log_keep, _), later, (carry, acc) in zip(ws, vs, parts, laters, states):
        acc = acc + jnp.dot(w.astype(v.dtype), v, preferred_element_type=jnp.float32)
        out.append((carry + later[:, 0:1] + log_keep[:, 0:1], acc))
    return tuple(out)


def _sb_kernel(q_ref, k_ref, v_ref, g_ref, tri_ref, o_ref, *, tq):
    heads, seq = q_ref.shape[0], q_ref.shape[1]
    nq = seq // tq
    tk = tq
    tri2 = tri_ref[...]
    strict = (lax.broadcasted_iota(jnp.int32, (tq, tk), 0)
              > lax.broadcasted_iota(jnp.int32, (tq, tk), 1))

    def q_tile(qi, _):
        r0 = pl.multiple_of(qi * tq, tq)
        qs = [q_ref[g, pl.ds(r0, tq), :] for g in range(heads)]
        def block(c0, state, mask):
            return _sb_blocks(qs, [k_ref[g, pl.ds(c0, tk), :] for g in range(heads)],
                              [v_ref[g, pl.ds(c0, tk), :] for g in range(heads)], tri2, state, mask)

        init = (jnp.zeros((tq, 1), jnp.float32), jnp.zeros((tq, HEAD_DIM), jnp.float32))
        state = block(r0, (init,) * heads, strict)
        state = lax.fori_loop(
            0, qi, lambda i, st: block(pl.multiple_of((qi - 1 - i) * tk, tk), st, None), state)
        for g in range(heads):
            out = state[g][1] * _silu(g_ref[g, pl.ds(r0, tq), :])
            o_ref[pl.ds(r0, tq), g * LANES:(g + 1) * LANES] = out.astype(o_ref.dtype)
        return 0

    lax.fori_loop(0, nq, q_tile, 0)


def _sb_attention(qkv, gates, batch, seq, tq, heads):
    q_s, k_s, v_s, g_s = _attn_specs(SB_Q0, SB_K0, SB_V0, GATE_SB0, seq, heads)
    upper = (lax.broadcasted_iota(jnp.int32, (tq, tq), 0)
             > lax.broadcasted_iota(jnp.int32, (tq, tq), 1)).astype(jnp.bfloat16)
    tri2 = jnp.concatenate([upper, upper], axis=0)
    return pl.pallas_call(
        functools.partial(_sb_kernel, tq=tq),
        grid=(batch, SB_HEADS // heads),
        in_specs=[q_s, k_s, v_s, g_s, pl.BlockSpec((2 * tq, tq), lambda b, hg: (0, 0))],
        out_specs=pl.BlockSpec((seq, heads * LANES), lambda b, hg: (b, hg)),
        out_shape=jax.ShapeDtypeStruct((batch * seq, SB_W), jnp.bfloat16),
        compiler_params=_params("parallel", "parallel"),
        name="sb_attention",
    )(qkv, qkv, qkv, gates, tri2)


def _diff_kernel(lam_ref, gsub_ref, q_ref, k_ref, v_ref, g_ref, o_ref, *, tq, tk, lam_init):
    heads, seq = q_ref.shape[0], q_ref.shape[1]
    nq = seq // tq
    ratio = tq // tk
    lp = lam_ref[...]
    lam = (jnp.exp(jnp.sum(lp[0:1] * lp[1:2], axis=1, keepdims=True))
           - jnp.exp(jnp.sum(lp[2:3] * lp[3:4], axis=1, keepdims=True)) + lam_init)
    col = lax.broadcasted_iota(jnp.int32, (1, tk), 1)
    slopes = []
    for g in range(heads):
        h = pl.program_id(1) * heads + g
        expo = jnp.full((1, tk), 127 - (8 // DIFF_HEADS) * (h + 1), jnp.int32)
        slopes.append(lax.bitcast_convert_type(expo << 23, jnp.float32))
    lane = lax.broadcasted_iota(jnp.int32, (tq, HEAD_DIM), 1)
    rows2 = lax.broadcasted_iota(jnp.int32, (2 * tq, tk), 0)
    rows2 = jnp.where(rows2 >= tq, rows2 - tq, rows2)
    cols2 = lax.broadcasted_iota(jnp.int32, (2 * tq, tk), 1)

    def q_tile(qi, _):
        r0 = pl.multiple_of(qi * tq, tq)
        qqs = []
        for g in range(heads):
            q = q_ref[g, pl.ds(r0, tq), :]
            zero = jnp.zeros_like(q)
            qqs.append(jnp.concatenate([jnp.where(lane < DIFF_QK_DIM, q, zero),
                                        jnp.where(lane >= DIFF_QK_DIM, q, zero)], axis=0))

        def scores(g, j):
            c0 = pl.multiple_of(j * tk, tk)
            k = k_ref[g, pl.ds(c0, tk), :]
            v = v_ref[g, pl.ds(c0, tk), :]
            bias = slopes[g] * (col + (c0 - r0)).astype(jnp.float32)
            return _qk(qqs[g], k) + bias, v

        def body(j, carry):
            return _softmax_steps([scores(g, j) for g in range(heads)], carry)

        init = (jnp.full((2 * tq, 1), NEG_BIG, jnp.float32), jnp.zeros((2 * tq, 1), jnp.float32),
                jnp.zeros((2 * tq, HEAD_DIM), jnp.float32))
        carry = lax.fori_loop(0, qi * ratio, body, (init,) * heads)
        for d in range(ratio):
            carry = _softmax_steps([scores(g, qi * ratio + d) for g in range(heads)], carry,
                                   rows2 - d * tk >= cols2)
        for g in range(heads):
            _, l, acc = carry[g]
            o = acc / l
            o = o[0:tq] - lam * o[tq:2 * tq]
            o = o * lax.rsqrt(jnp.mean(o * o, axis=1, keepdims=True) + SUBLN_EPS)
            o = o * gsub_ref[...] * (1.0 - lam_init)
            out = o * _silu(g_ref[g, pl.ds(r0, tq), :])
            o_ref[pl.ds(r0, tq), g * LANES:(g + 1) * LANES] = out.astype(o_ref.dtype)
        return 0

    lax.fori_loop(0, nq, q_tile, 0)


def _diff_attention(qkv, gates, lam_p, subln_row, batch, seq, tq, tk, lam_init, heads):
    q_s, k_s, v_s, g_s = _attn_specs(DIFF_Q0, DIFF_K0, DIFF_V0, GATE_DIFF0, seq, heads)
    return pl.pallas_call(
        functools.partial(_diff_kernel, tq=tq, tk=tk, lam_init=lam_init),
        grid=(batch, DIFF_HEADS // heads),
        in_specs=[pl.BlockSpec(lam_p.shape, lambda b, hg: (0, 0)),
                  pl.BlockSpec((1, HEAD_DIM), lambda b, hg: (0, 0)),
                  q_s, k_s, v_s, g_s],
        out_specs=pl.BlockSpec((seq, heads * LANES), lambda b, hg: (b, hg)),
        out_shape=jax.ShapeDtypeStruct((batch * seq, DIFF_W), jnp.bfloat16),
        compiler_params=_params("parallel", "parallel"),
        name="diff_attention",
    )(lam_p, subln_row, qkv, qkv, qkv, gates)


def _merge_kernel(of_ref, os_ref, od_ref, wf_ref, ws_ref, wd_ref, x_ref, g_ref, b_ref, y_ref, yb_ref):
    y = (jnp.dot(of_ref[...], wf_ref[...], preferred_element_type=jnp.float32)
         + jnp.dot(os_ref[...], ws_ref[...], preferred_element_type=jnp.float32)
         + jnp.dot(od_ref[...], wd_ref[...], preferred_element_type=jnp.float32))
    z = DEEPNORM_ALPHA * x_ref[...] + y
    mu = jnp.mean(z, axis=1, keepdims=True)
    zc = z - mu
    var = jnp.mean(zc * zc, axis=1, keepdims=True)
    out = zc * lax.rsqrt(var + LN_EPS) * g_ref[...] + b_ref[...]
    y_ref[...] = out
    yb_ref[...] = out.astype(yb_ref.dtype)


def _merge(o_fox, o_sb, o_diff, w_f, w_s, w_d, x, ln_g, ln_b, tm):
    t, d = x.shape
    row = lambda w: pl.BlockSpec((tm, w), lambda i: (i, 0))
    full = lambda a: pl.BlockSpec(a.shape, lambda i: (0, 0))
    return pl.pallas_call(
        _merge_kernel,
        grid=(t // tm,),
        in_specs=[row(FOX_W), row(SB_W), row(DIFF_W), full(w_f), full(w_s), full(w_d),
                  row(d), full(ln_g), full(ln_b)],
        out_specs=[row(d), row(d)],
        out_shape=[jax.ShapeDtypeStruct((t, d), jnp.float32),
                   jax.ShapeDtypeStruct((t, d), jnp.bfloat16)],
        compiler_params=_params("parallel"),
        name="merge_layernorm",
    )(o_fox, o_sb, o_diff, w_f, w_s, w_d, x, ln_g, ln_b)


def _split_in_weights(w_in):
    sizes = (FOX_W,) * 4 + (SB_W,) * 4 + (DIFF_W,) * 4 + (FOX_HEADS,)
    offs = [0]
    for n in sizes:
        offs.append(offs[-1] + n)
    part = lambda i: w_in[:, :, offs[i]:offs[i + 1]]
    fq, fk, fv, fg, sq, sk, sv, sg, dq, dk, dv, dg, ff = (part(i) for i in range(13))
    w_qkv = jnp.concatenate([fq, fk, fv, sq, sk, sv, dq, dk, dv], axis=-1).astype(jnp.bfloat16)
    ff = jnp.pad(ff, ((0, 0), (0, 0), (0, LANES - FOX_HEADS)))
    w_gate = jnp.concatenate([fg, sg, dg, ff], axis=-1).astype(jnp.bfloat16)
    return w_qkv, w_gate


def _qkv_col_scale():
    one = lambda n: jnp.ones((n,), jnp.float32)
    full = lambda n, v: jnp.full((n,), v, jnp.float32)
    return jnp.concatenate([
        full(FOX_W, HEAD_DIM ** -0.5), one(2 * FOX_W),
        full(SB_W, HEAD_DIM ** -0.5), one(2 * SB_W),
        full(DIFF_W, DIFF_QK_DIM ** -0.5), one(2 * DIFF_W)])[None, :]


def kernel(x, w_in, b_f, diff_lambda, diff_subln_g, w_out, ln_g, ln_b):
    batch, seq, d_model = x.shape
    depth = w_in.shape[0]
    t = batch * seq
    tq = min(seq, 256)
    tm = min(t, 1024)

    w_qkv, w_gate = _split_in_weights(w_in)
    w_out_b = w_out.astype(jnp.bfloat16)
    qkv_scale = _qkv_col_scale()
    gate_scale = jnp.ones((1, GATE_BLOCKS * LANES), jnp.float32)
    b_f_rows = jnp.pad(b_f, ((0, 0), (0, LANES - FOX_HEADS)))

    xf = x.reshape(t, d_model)
    xb = xf.astype(jnp.bfloat16)
    for l in range(depth):
        lam_init = 0.8 - 0.6 * math.exp(-0.3 * l)
        qkv = _project(xb, w_qkv[l], qkv_scale, jnp.bfloat16, tm, 1024, "qkv_projection")
        gates = _project(xb, w_gate[l], gate_scale, jnp.float32, min(t, 512), GATE_BLOCKS * LANES,
                         "gate_projection")
        c = _forget_prefix(gates, b_f_rows[l:l + 1], batch, seq, tq)
        ck = c[:, :FOX_HEADS, :].reshape(batch * FOX_HEADS, seq // tq, 1, tq)
        o_fox = _fox_attention(qkv, gates, ck, batch, seq, tq, tq, FOX_GROUP)
        o_sb = _sb_attention(qkv, gates, batch, seq, tq, SB_GROUP)
        o_diff = _diff_attention(qkv, gates, diff_lambda[l], diff_subln_g[l:l + 1], batch, seq, tq, tq,
                                 lam_init, DIFF_GROUP)
        wl = w_out_b[l]
        xf, xb = _merge(o_fox, o_sb, o_diff, wl[:FOX_W], wl[FOX_W:FOX_W + SB_W], wl[FOX_W + SB_W:],
                        xf, ln_g[l:l + 1], ln_b[l:l + 1], min(t, 512))
    return xf.reshape(batch, seq, d_model)
```

```python
import functools
import math

import jax
import jax.numpy as jnp
from jax import lax
from jax.experimental import pallas as pl
from jax.experimental.pallas import tpu as pltpu

HEAD_DIM = 128
FOX_HEADS = 6
SB_HEADS = 6
DIFF_HEADS = 4
DIFF_QK_DIM = HEAD_DIM // 2
FOX_W = FOX_HEADS * HEAD_DIM
SB_W = SB_HEADS * HEAD_DIM
DIFF_W = DIFF_HEADS * HEAD_DIM
DEPTH_FOR_DEEPNORM = 4
DEEPNORM_ALPHA = (2 * DEPTH_FOR_DEEPNORM) ** 0.25
LN_EPS = 1e-5
SUBLN_EPS = 1e-5
NEG_BIG = -1e30
LOG2E = math.log2(math.e)

LANES = 128
VMEM_LIMIT_BYTES = 56 * 1024 * 1024

FOX_Q0, FOX_K0 = 0, FOX_HEADS
SB_Q0, SB_K0 = 2 * FOX_HEADS, 2 * FOX_HEADS + SB_HEADS
DIFF_Q0, DIFF_K0 = 2 * (FOX_HEADS + SB_HEADS), 2 * (FOX_HEADS + SB_HEADS) + DIFF_HEADS
FOX_V0, SB_V0, DIFF_V0 = 0, FOX_HEADS, FOX_HEADS + SB_HEADS
GATE_FOX0, GATE_SB0, GATE_DIFF0 = 0, FOX_HEADS, FOX_HEADS + SB_HEADS
GATE_FF = FOX_HEADS + SB_HEADS + DIFF_HEADS
GATE_BLOCKS = GATE_FF + 1
FOX_GROUP, SB_GROUP, DIFF_GROUP = 6, 6, 4
SB_SUB = 128


def _params(*semantics):
    return pltpu.CompilerParams(dimension_semantics=semantics, vmem_limit_bytes=VMEM_LIMIT_BYTES)


def _proj_kernel(x_ref, w_ref, s_ref, o_ref, *, transpose_out):
    acc = jnp.dot(x_ref[...], w_ref[...], preferred_element_type=jnp.float32)
    acc = acc * s_ref[...]
    for c in range(o_ref.shape[0]):
        blk = acc[:, c * LANES:(c + 1) * LANES]
        o_ref[c] = (blk.T if transpose_out else blk).astype(o_ref.dtype)


def _project(x, w, col_scale, out_dtype, tm, tn, name, transpose_out=False):
    t, k = x.shape
    n = w.shape[1]
    if transpose_out:
        out_spec = pl.BlockSpec((tn // LANES, LANES, tm), lambda i, j: (j, 0, i))
        out_shape = (n // LANES, LANES, t)
    else:
        out_spec = pl.BlockSpec((tn // LANES, tm, LANES), lambda i, j: (j, i, 0))
        out_shape = (n // LANES, t, LANES)
    return pl.pallas_call(
        functools.partial(_proj_kernel, transpose_out=transpose_out),
        grid=(t // tm, n // tn),
        in_specs=[
            pl.BlockSpec((tm, k), lambda i, j: (i, 0)),
            pl.BlockSpec((k, tn), lambda i, j: (0, j)),
            pl.BlockSpec((1, tn), lambda i, j: (0, j)),
        ],
        out_specs=out_spec,
        out_shape=jax.ShapeDtypeStruct(out_shape, out_dtype),
        compiler_params=_params("parallel", "parallel"),
        name=name,
    )(x, w, col_scale)


def _log_sigmoid(z):
    return jnp.minimum(z, 0.0) - jnp.log1p(jnp.exp(-jnp.abs(z)))


def _split3(x):
    x1 = x.astype(jnp.bfloat16)
    r = x - x1.astype(jnp.float32)
    x2 = r.astype(jnp.bfloat16)
    x3 = (r - x2.astype(jnp.float32)).astype(jnp.bfloat16)
    return x1, x2, x3


def _decay_kernel(ff_ref, bf_ref, tri_ref, c_ref, *, chunk):
    s = ff_ref.shape[1]
    tri = tri_ref[...]
    carry = jnp.zeros((1, LANES), jnp.float32)
    for c in range(s // chunk):
        lf = _log_sigmoid(ff_ref[0, c * chunk:(c + 1) * chunk, :] + bf_ref[...])
        x1, x2, x3 = _split3(lf)
        cs = (jnp.dot(tri, x1, preferred_element_type=jnp.float32)
              + jnp.dot(tri, x2, preferred_element_type=jnp.float32)
              + jnp.dot(tri, x3, preferred_element_type=jnp.float32)) + carry
        c_ref[0, c * chunk:(c + 1) * chunk, :] = cs
        carry = cs[chunk - 1:chunk, :]


def _forget_prefix(gates, b_f_row, batch, seq, chunk):
    tri = (lax.broadcasted_iota(jnp.int32, (chunk, chunk), 0)
           >= lax.broadcasted_iota(jnp.int32, (chunk, chunk), 1)).astype(jnp.bfloat16)
    return pl.pallas_call(
        functools.partial(_decay_kernel, chunk=chunk),
        grid=(batch,),
        in_specs=[
            pl.BlockSpec((1, seq, LANES), lambda b: (GATE_FF, b, 0)),
            pl.BlockSpec((1, LANES), lambda b: (0, 0)),
            pl.BlockSpec((chunk, chunk), lambda b: (0, 0)),
        ],
        out_specs=pl.BlockSpec((1, seq, LANES), lambda b: (b, 0, 0)),
        out_shape=jax.ShapeDtypeStruct((batch, seq, LANES), jnp.float32),
        compiler_params=_params("parallel"),
        name="forget_prefix",
    )(gates, b_f_row, tri)


def _kq(k, q):
    return lax.dot_general(k, q, (((1,), (1,)), ((), ())), preferred_element_type=jnp.float32)


def _silu(g):
    return g * (1.0 / (1.0 + jnp.exp(-g)))


def _lane_tiles(a, tile):
    n = a.shape[1] // LANES
    return jnp.concatenate([tile(a[:, i * LANES:(i + 1) * LANES]) for i in range(n)], axis=1)


def _softmax_steps(sts, vts, carry, mask=None):
    stats = []
    for st, (m, l, _) in zip(sts, carry):
        if mask is not None:
            st = jnp.where(mask, st, NEG_BIG)
        m_new = jnp.maximum(m, jnp.max(st, axis=0, keepdims=True))
        alpha = jnp.exp2(m - m_new)
        p = jnp.exp2(st - m_new)
        stats.append((m_new, alpha, alpha * l + jnp.sum(p, axis=0, keepdims=True), p))
    out = []
    for vt, (_, _, acc), (m_new, alpha, l, p) in zip(vts, carry, stats):
        acc = alpha * acc + jnp.dot(vt, p.astype(vt.dtype), preferred_element_type=jnp.float32)
        out.append((m_new, l, acc))
    return tuple(out)


def _attn_specs(q0, k0, v0, g0, seq, heads):
    blk = lambda base: pl.BlockSpec((heads, seq, LANES), lambda b, hg: (base // heads + hg, b, 0))
    vt = pl.BlockSpec((heads, LANES, seq), lambda b, hg: (v0 // heads + hg, 0, b))
    return blk(q0), blk(k0), vt, blk(g0)


def _fox_kernel(q_ref, k_ref, vt_ref, g_ref, c_ref, o_ref, bias_ref, *, tq, tk):
    heads, seq = q_ref.shape[0], q_ref.shape[1]
    nq = seq // tq
    ratio = tq // tk
    rows = lax.broadcasted_iota(jnp.int32, (tk, tq), 0)
    cols = lax.broadcasted_iota(jnp.int32, (tk, tq), 1)
    lane = lax.broadcasted_iota(jnp.int32, (seq, LANES), 1)
    for g in range(heads):
        h = pl.program_id(1) * heads + g
        col = jnp.sum(jnp.where(lane == h, c_ref[0], 0.0), axis=1, keepdims=True)
        bias_ref[g] = jnp.broadcast_to(col * (-LOG2E), (seq, LANES))

    def q_tile(qi, _):
        r0 = pl.multiple_of(qi * tq, tq)
        qs = [q_ref[g, pl.ds(r0, tq), :] for g in range(heads)]

        def scores(g, j):
            c0 = pl.multiple_of(j * tk, tk)
            bias = bias_ref[g, pl.ds(c0, tk), :]
            return _lane_tiles(_kq(k_ref[g, pl.ds(c0, tk), :], qs[g]), lambda a: a + bias)

        def values(j):
            c0 = pl.multiple_of(j * tk, tk)
            return [vt_ref[g, :, pl.ds(c0, tk)] for g in range(heads)]

        def body(j, carry):
            return _softmax_steps([scores(g, j) for g in range(heads)], values(j), carry)

        init = (jnp.full((1, tq), NEG_BIG, jnp.float32), jnp.zeros((1, tq), jnp.float32),
                jnp.zeros((HEAD_DIM, tq), jnp.float32))
        carry = lax.fori_loop(0, qi * ratio, body, (init,) * heads)
        for d in range(ratio):
            j = qi * ratio + d
            carry = _softmax_steps([scores(g, j) for g in range(heads)], values(j), carry,
                                   cols >= rows + d * tk)
        for g in range(heads):
            _, l, acc = carry[g]
            out = (acc / l).T * _silu(g_ref[g, pl.ds(r0, tq), :])
            o_ref[pl.ds(r0, tq), g * LANES:(g + 1) * LANES] = out.astype(o_ref.dtype)
        return 0

    lax.fori_loop(0, nq, q_tile, 0)


def _fox_attention(qk, vt, gates, c, batch, seq, tq, tk, heads):
    q_s, k_s, v_s, g_s = _attn_specs(FOX_Q0, FOX_K0, FOX_V0, GATE_FOX0, seq, heads)
    return pl.pallas_call(
        functools.partial(_fox_kernel, tq=tq, tk=tk),
        grid=(batch, FOX_HEADS // heads),
        in_specs=[q_s, k_s, v_s, g_s, pl.BlockSpec((1, seq, LANES), lambda b, hg: (b, 0, 0))],
        out_specs=pl.BlockSpec((seq, heads * LANES), lambda b, hg: (b, hg)),
        out_shape=jax.ShapeDtypeStruct((batch * seq, FOX_W), jnp.bfloat16),
        scratch_shapes=[pltpu.VMEM((heads, seq, LANES), jnp.float32)],
        compiler_params=_params("parallel", "parallel"),
        name="fox_attention",
    )(qk, qk, vt, gates, c)


def _sb_blocks(qs, ks, vts, ut2, states, strict):
    tk = ks[0].shape[0]
    zs = [_kq(k, q) for k, q in zip(ks, qs)]
    parts = []
    for z in zs:
        neg_abs = lax.bitcast_convert_type(
            lax.bitcast_convert_type(z, jnp.uint32) | jnp.uint32(0x80000000), jnp.float32)
        sp = jnp.log2(1.0 + jnp.exp2(neg_abs))
        log_beta = jnp.minimum(z, 0.0) - sp
        log_keep = log_beta - z
        if strict is not None:
            log_keep = jnp.where(strict, log_keep, 0.0)
        hi = log_keep.astype(jnp.bfloat16)
        lo = (log_keep - hi.astype(jnp.float32)).astype(jnp.bfloat16)
        parts.append((log_beta, log_keep, hi, lo))
    subs = []
    for _, _, hi, lo in parts:
        subs.append([
            jnp.dot(ut2, jnp.concatenate([hi[i:i + SB_SUB], lo[i:i + SB_SUB]], axis=0),
                    preferred_element_type=jnp.float32)
            for i in range(0, tk, SB_SUB)])
    ws, carries = [], []
    for (log_beta, log_keep, _, _), sub, (carry, _) in zip(parts, subs, states):
        laters = [None] * len(sub)
        for n in reversed(range(len(sub))):
            laters[n] = sub[n] + carry
            carry = carry + sub[n][0:1] + log_keep[n * SB_SUB:n * SB_SUB + 1]
        w = jnp.exp2(log_beta + jnp.concatenate(laters, axis=0))
        if strict is not None:
            w = jnp.where(strict, w, 0.0)
        ws.append(w)
        carries.append(carry)
    out = []
    for w, vt, carry, (_, acc) in zip(ws, vts, carries, states):
        out.append((carry, acc + jnp.dot(vt, w.astype(vt.dtype), preferred_element_type=jnp.float32)))
    return tuple(out)


def _sb_kernel(q_ref, k_ref, vt_ref, g_ref, ut_ref, o_ref, *, tq):
    heads, seq = q_ref.shape[0], q_ref.shape[1]
    nq = seq // tq
    tk = tq
    ut2 = ut_ref[...]
    strict = (lax.broadcasted_iota(jnp.int32, (tk, tq), 1)
              > lax.broadcasted_iota(jnp.int32, (tk, tq), 0))

    def q_tile(qi, _):
        r0 = pl.multiple_of(qi * tq, tq)
        qs = [q_ref[g, pl.ds(r0, tq), :] for g in range(heads)]

        def block(c0, state, mask):
            return _sb_blocks(qs, [k_ref[g, pl.ds(c0, tk), :] for g in range(heads)],
                              [vt_ref[g, :, pl.ds(c0, tk)] for g in range(heads)], ut2, state, mask)

        init = (jnp.zeros((1, tq), jnp.float32), jnp.zeros((HEAD_DIM, tq), jnp.float32))
        state = block(r0, (init,) * heads, strict)
        state = lax.fori_loop(
            0, qi, lambda i, st: block(pl.multiple_of((qi - 1 - i) * tk, tk), st, None), state)
        for g in range(heads):
            out = state[g][1].T * _silu(g_ref[g, pl.ds(r0, tq), :])
            o_ref[pl.ds(r0, tq), g * LANES:(g + 1) * LANES] = out.astype(o_ref.dtype)
        return 0

    lax.fori_loop(0, nq, q_tile, 0)


def _sb_attention(qk, vt, gates, batch, seq, tq, heads):
    q_s, k_s, v_s, g_s = _attn_specs(SB_Q0, SB_K0, SB_V0, GATE_SB0, seq, heads)
    upper = (lax.broadcasted_iota(jnp.int32, (SB_SUB, SB_SUB), 1)
             > lax.broadcasted_iota(jnp.int32, (SB_SUB, SB_SUB), 0)).astype(jnp.bfloat16)
    ut2 = jnp.concatenate([upper, upper], axis=1)
    return pl.pallas_call(
        functools.partial(_sb_kernel, tq=tq),
        grid=(batch, SB_HEADS // heads),
        in_specs=[q_s, k_s, v_s, g_s, pl.BlockSpec((SB_SUB, 2 * SB_SUB), lambda b, hg: (0, 0))],
        out_specs=pl.BlockSpec((seq, heads * LANES), lambda b, hg: (b, hg)),
        out_shape=jax.ShapeDtypeStruct((batch * seq, SB_W), jnp.bfloat16),
        compiler_params=_params("parallel", "parallel"),
        name="sb_attention",
    )(qk, qk, vt, gates, ut2)


def _diff_kernel(lam_ref, gsub_ref, q_ref, k_ref, vt_ref, g_ref, o_ref, *, tq, tk, lam_init):
    heads, seq = q_ref.shape[0], q_ref.shape[1]
    nq = seq // tq
    ratio = tq // tk
    lp = lam_ref[...]
    lam = (jnp.exp(jnp.sum(lp[0:1] * lp[1:2], axis=1, keepdims=True))
           - jnp.exp(jnp.sum(lp[2:3] * lp[3:4], axis=1, keepdims=True)) + lam_init)
    key = lax.broadcasted_iota(jnp.int32, (tk, LANES), 0).astype(jnp.float32)
    ramps = []
    for g in range(heads):
        h = pl.program_id(1) * heads + g
        expo = jnp.full((tk, LANES), 127 - (8 // DIFF_HEADS) * (h + 1), jnp.int32)
        slope = lax.bitcast_convert_type(expo << 23, jnp.float32) * LOG2E
        ramps.append((slope, slope * key))
    lane = lax.broadcasted_iota(jnp.int32, (tq, HEAD_DIM), 1)
    rows2 = lax.broadcasted_iota(jnp.int32, (tk, 2 * tq), 0)
    cols2 = lax.broadcasted_iota(jnp.int32, (tk, 2 * tq), 1)
    cols2 = jnp.where(cols2 >= tq, cols2 - tq, cols2)

    def q_tile(qi, _):
        r0 = pl.multiple_of(qi * tq, tq)
        qqs = []
        for g in range(heads):
            q = q_ref[g, pl.ds(r0, tq), :]
            zero = jnp.zeros_like(q)
            qqs.append(jnp.concatenate([jnp.where(lane < DIFF_QK_DIM, q, zero),
                                        jnp.where(lane >= DIFF_QK_DIM, q, zero)], axis=0))

        def scores(g, j):
            c0 = pl.multiple_of(j * tk, tk)
            slope, ramp = ramps[g]
            bias = ramp + slope * (c0 - r0).astype(jnp.float32)
            return _lane_tiles(_kq(k_ref[g, pl.ds(c0, tk), :], qqs[g]), lambda a: a + bias)

        def values(j):
            c0 = pl.multiple_of(j * tk, tk)
            return [vt_ref[g, :, pl.ds(c0, tk)] for g in range(heads)]

        def body(j, carry):
            return _softmax_steps([scores(g, j) for g in range(heads)], values(j), carry)

        init = (jnp.full((1, 2 * tq), NEG_BIG, jnp.float32), jnp.zeros((1, 2 * tq), jnp.float32),
                jnp.zeros((HEAD_DIM, 2 * tq), jnp.float32))
        carry = lax.fori_loop(0, qi * ratio, body, (init,) * heads)
        for d in range(ratio):
            j = qi * ratio + d
            carry = _softmax_steps([scores(g, j) for g in range(heads)], values(j), carry,
                                   cols2 >= rows2 + d * tk)
        for g in range(heads):
            _, l, acc = carry[g]
            o = acc / l
            o = o[:, 0:tq] - lam * o[:, tq:2 * tq]
            o = o * lax.rsqrt(jnp.mean(o * o, axis=0, keepdims=True) + SUBLN_EPS)
            o = o.T * gsub_ref[...] * (1.0 - lam_init)
            out = o * _silu(g_ref[g, pl.ds(r0, tq), :])
            o_ref[pl.ds(r0, tq), g * LANES:(g + 1) * LANES] = out.astype(o_ref.dtype)
        return 0

    lax.fori_loop(0, nq, q_tile, 0)


def _diff_attention(qk, vt, gates, lam_p, subln_row, batch, seq, tq, tk, lam_init, heads):
    q_s, k_s, v_s, g_s = _attn_specs(DIFF_Q0, DIFF_K0, DIFF_V0, GATE_DIFF0, seq, heads)
    return pl.pallas_call(
        functools.partial(_diff_kernel, tq=tq, tk=tk, lam_init=lam_init),
        grid=(batch, DIFF_HEADS // heads),
        in_specs=[pl.BlockSpec(lam_p.shape, lambda b, hg: (0, 0)),
                  pl.BlockSpec((1, HEAD_DIM), lambda b, hg: (0, 0)),
                  q_s, k_s, v_s, g_s],
        out_specs=pl.BlockSpec((seq, heads * LANES), lambda b, hg: (b, hg)),
        out_shape=jax.ShapeDtypeStruct((batch * seq, DIFF_W), jnp.bfloat16),
        compiler_params=_params("parallel", "parallel"),
        name="diff_attention",
    )(lam_p, subln_row, qk, qk, vt, gates)


def _merge_kernel(of_ref, os_ref, od_ref, wf_ref, ws_ref, wd_ref, x_ref, g_ref, b_ref, y_ref, yb_ref):
    y = (jnp.dot(of_ref[...], wf_ref[...], preferred_element_type=jnp.float32)
         + jnp.dot(os_ref[...], ws_ref[...], preferred_element_type=jnp.float32)
         + jnp.dot(od_ref[...], wd_ref[...], preferred_element_type=jnp.float32))
    z = DEEPNORM_ALPHA * x_ref[...] + y
    mu = jnp.mean(z, axis=1, keepdims=True)
    zc = z - mu
    var = jnp.mean(zc * zc, axis=1, keepdims=True)
    out = zc * lax.rsqrt(var + LN_EPS) * g_ref[...] + b_ref[...]
    y_ref[...] = out
    yb_ref[...] = out.astype(yb_ref.dtype)


def _merge(o_fox, o_sb, o_diff, w_f, w_s, w_d, x, ln_g, ln_b, tm):
    t, d = x.shape
    row = lambda w: pl.BlockSpec((tm, w), lambda i: (i, 0))
    full = lambda a: pl.BlockSpec(a.shape, lambda i: (0, 0))
    return pl.pallas_call(
        _merge_kernel,
        grid=(t // tm,),
        in_specs=[row(FOX_W), row(SB_W), row(DIFF_W), full(w_f), full(w_s), full(w_d),
                  row(d), full(ln_g), full(ln_b)],
        out_specs=[row(d), row(d)],
        out_shape=[jax.ShapeDtypeStruct((t, d), jnp.float32),
                   jax.ShapeDtypeStruct((t, d), jnp.bfloat16)],
        compiler_params=_params("parallel"),
        name="merge_layernorm",
    )(o_fox, o_sb, o_diff, w_f, w_s, w_d, x, ln_g, ln_b)


def _split_in_weights(w_in):
    sizes = (FOX_W,) * 4 + (SB_W,) * 4 + (DIFF_W,) * 4 + (FOX_HEADS,)
    offs = [0]
    for n in sizes:
        offs.append(offs[-1] + n)
    part = lambda i: w_in[:, :, offs[i]:offs[i + 1]]
    fq, fk, fv, fg, sq, sk, sv, sg, dq, dk, dv, dg, ff = (part(i) for i in range(13))
    w_qk = jnp.concatenate([fq, fk, sq, sk, dq, dk], axis=-1).astype(jnp.bfloat16)
    w_v = jnp.concatenate([fv, sv, dv], axis=-1).astype(jnp.bfloat16)
    ff = jnp.pad(ff, ((0, 0), (0, 0), (0, LANES - FOX_HEADS)))
    w_gate = jnp.concatenate([fg, sg, dg, ff], axis=-1).astype(jnp.bfloat16)
    return w_qk, w_v, w_gate


def _qk_col_scale():
    one = lambda n: jnp.ones((n,), jnp.float32)
    full = lambda n, v: jnp.full((n,), v, jnp.float32)
    return jnp.concatenate([
        full(FOX_W, HEAD_DIM ** -0.5 * LOG2E), one(FOX_W),
        full(SB_W, HEAD_DIM ** -0.5 * LOG2E), one(SB_W),
        full(DIFF_W, DIFF_QK_DIM ** -0.5 * LOG2E), one(DIFF_W)])[None, :]


def kernel(x, w_in, b_f, diff_lambda, diff_subln_g, w_out, ln_g, ln_b):
    batch, seq, d_model = x.shape
    depth = w_in.shape[0]
    t = batch * seq
    tq = min(seq, 256)
    tm = min(t, 1024)

    w_qk, w_v, w_gate = _split_in_weights(w_in)
    w_out_b = w_out.astype(jnp.bfloat16)
    qk_scale = _qk_col_scale()
    v_scale = jnp.ones((1, FOX_W + SB_W + DIFF_W), jnp.float32)
    gate_scale = jnp.ones((1, GATE_BLOCKS * LANES), jnp.float32)
    b_f_rows = jnp.pad(b_f, ((0, 0), (0, LANES - FOX_HEADS)))

    xf = x.reshape(t, d_model)
    xb = xf.astype(jnp.bfloat16)
    for l in range(depth):
        lam_init = 0.8 - 0.6 * math.exp(-0.3 * l)
        qk = _project(xb, w_qk[l], qk_scale, jnp.bfloat16, tm, 1024, "qk_projection")
        vt = _project(xb, w_v[l], v_scale, jnp.bfloat16, tm, 1024, "v_projection", transpose_out=True)
        gates = _project(xb, w_gate[l], gate_scale, jnp.float32, min(t, 512), GATE_BLOCKS * LANES,
                         "gate_projection")
        c = _forget_prefix(gates, b_f_rows[l:l + 1], batch, seq, tq)
        o_fox = _fox_attention(qk, vt, gates, c, batch, seq, tq, tq, FOX_GROUP)
        o_sb = _sb_attention(qk, vt, gates, batch, seq, tq, SB_GROUP)
        o_diff = _diff_attention(qk, vt, gates, diff_lambda[l], diff_subln_g[l:l + 1], batch, seq, tq, tq,
                                 lam_init, DIFF_GROUP)
        wl = w_out_b[l]
        xf, xb = _merge(o_fox, o_sb, o_diff, wl[:FOX_W], wl[FOX_W:FOX_W + SB_W], wl[FOX_W + SB_W:],
                        xf, ln_g[l:l + 1], ln_b[l:l + 1], min(t, 512))
    return xf.reshape(batch, seq, d_model)
```

```python
import functools
import math

import jax
import jax.numpy as jnp
from jax import lax
from jax.experimental import pallas as pl
from jax.experimental.pallas import tpu as pltpu

HEAD_DIM = 128
FOX_HEADS = 6
SB_HEADS = 6
DIFF_HEADS = 4
DIFF_QK_DIM = HEAD_DIM // 2
FOX_W = FOX_HEADS * HEAD_DIM
SB_W = SB_HEADS * HEAD_DIM
DIFF_W = DIFF_HEADS * HEAD_DIM
DEPTH_FOR_DEEPNORM = 4
DEEPNORM_ALPHA = (2 * DEPTH_FOR_DEEPNORM) ** 0.25
LN_EPS = 1e-5
SUBLN_EPS = 1e-5
NEG_BIG = -1e30
LOG2E = math.log2(math.e)

LANES = 128
VMEM_LIMIT_BYTES = 56 * 1024 * 1024

FOX_Q0, FOX_K0 = 0, FOX_HEADS
SB_Q0, SB_K0 = 2 * FOX_HEADS, 2 * FOX_HEADS + SB_HEADS
DIFF_Q0, DIFF_K0 = 2 * (FOX_HEADS + SB_HEADS), 2 * (FOX_HEADS + SB_HEADS) + DIFF_HEADS
FOX_V0, SB_V0, DIFF_V0 = 0, FOX_HEADS, FOX_HEADS + SB_HEADS
GATE_FOX0, GATE_SB0, GATE_DIFF0 = 0, FOX_HEADS, FOX_HEADS + SB_HEADS
GATE_FF = FOX_HEADS + SB_HEADS + DIFF_HEADS
GATE_BLOCKS = GATE_FF + 1
FOX_GROUP, SB_GROUP, DIFF_GROUP = 3, 3, 2
ATTN_TQ = 2048
KEY_BLOCK = 256
SB_SUB = 128


def _params(*semantics):
    return pltpu.CompilerParams(dimension_semantics=semantics, vmem_limit_bytes=VMEM_LIMIT_BYTES)


def _proj_kernel(x_ref, w_ref, s_ref, o_ref, *, transpose_out):
    acc = jnp.dot(x_ref[...], w_ref[...], preferred_element_type=jnp.float32)
    acc = acc * s_ref[...]
    for c in range(o_ref.shape[0]):
        blk = acc[:, c * LANES:(c + 1) * LANES]
        o_ref[c] = (blk.T if transpose_out else blk).astype(o_ref.dtype)


def _project(x, w, col_scale, out_dtype, tm, tn, name, transpose_out=False):
    t, k = x.shape
    n = w.shape[1]
    if transpose_out:
        out_spec = pl.BlockSpec((tn // LANES, LANES, tm), lambda i, j: (j, 0, i))
        out_shape = (n // LANES, LANES, t)
    else:
        out_spec = pl.BlockSpec((tn // LANES, tm, LANES), lambda i, j: (j, i, 0))
        out_shape = (n // LANES, t, LANES)
    return pl.pallas_call(
        functools.partial(_proj_kernel, transpose_out=transpose_out),
        grid=(t // tm, n // tn),
        in_specs=[
            pl.BlockSpec((tm, k), lambda i, j: (i, 0)),
            pl.BlockSpec((k, tn), lambda i, j: (0, j)),
            pl.BlockSpec((1, tn), lambda i, j: (0, j)),
        ],
        out_specs=out_spec,
        out_shape=jax.ShapeDtypeStruct(out_shape, out_dtype),
        compiler_params=_params("parallel", "parallel"),
        name=name,
    )(x, w, col_scale)


def _log_sigmoid(z):
    return jnp.minimum(z, 0.0) - jnp.log1p(jnp.exp(-jnp.abs(z)))


def _split3(x):
    x1 = x.astype(jnp.bfloat16)
    r = x - x1.astype(jnp.float32)
    x2 = r.astype(jnp.bfloat16)
    x3 = (r - x2.astype(jnp.float32)).astype(jnp.bfloat16)
    return x1, x2, x3


def _decay_kernel(ff_ref, bf_ref, tri_ref, c_ref, *, chunk):
    s = ff_ref.shape[1]
    tri = tri_ref[...]
    carry = jnp.zeros((1, LANES), jnp.float32)
    for c in range(s // chunk):
        lf = _log_sigmoid(ff_ref[0, c * chunk:(c + 1) * chunk, :] + bf_ref[...])
        x1, x2, x3 = _split3(lf)
        cs = (jnp.dot(tri, x1, preferred_element_type=jnp.float32)
              + jnp.dot(tri, x2, preferred_element_type=jnp.float32)
              + jnp.dot(tri, x3, preferred_element_type=jnp.float32)) + carry
        c_ref[0, c * chunk:(c + 1) * chunk, :] = cs
        carry = cs[chunk - 1:chunk, :]


def _forget_prefix(gates, b_f_row, batch, seq, chunk):
    tri = (lax.broadcasted_iota(jnp.int32, (chunk, chunk), 0)
           >= lax.broadcasted_iota(jnp.int32, (chunk, chunk), 1)).astype(jnp.bfloat16)
    return pl.pallas_call(
        functools.partial(_decay_kernel, chunk=chunk),
        grid=(batch,),
        in_specs=[
            pl.BlockSpec((1, seq, LANES), lambda b: (GATE_FF, b, 0)),
            pl.BlockSpec((1, LANES), lambda b: (0, 0)),
            pl.BlockSpec((chunk, chunk), lambda b: (0, 0)),
        ],
        out_specs=pl.BlockSpec((1, seq, LANES), lambda b: (b, 0, 0)),
        out_shape=jax.ShapeDtypeStruct((batch, seq, LANES), jnp.float32),
        compiler_params=_params("parallel"),
        name="forget_prefix",
    )(gates, b_f_row, tri)


def _kq(k, q):
    return lax.dot_general(k, q, (((1,), (1,)), ((), ())), preferred_element_type=jnp.float32)


def _silu(g):
    return g * (1.0 / (1.0 + jnp.exp(-g)))


def _lane_tiles(a, tile):
    n = a.shape[1] // LANES
    return jnp.concatenate([tile(a[:, i * LANES:(i + 1) * LANES]) for i in range(n)], axis=1)


def _gather_lanes(x, ranges):
    if ranges is None:
        return x
    return jnp.concatenate([x[:, a:b] for a, b in ranges], axis=1)


def _scatter_lanes(x, new, ranges):
    if ranges is None:
        return new
    pieces, pos, off = [], 0, 0
    for a, b in ranges:
        if a > pos:
            pieces.append(x[:, pos:a])
        pieces.append(new[:, off:off + b - a])
        off += b - a
        pos = b
    if pos < x.shape[1]:
        pieces.append(x[:, pos:])
    return jnp.concatenate(pieces, axis=1)


def _softmax_steps(sts, vts, carry, ranges=None, mask=None):
    stats = []
    for st, (m, l, _) in zip(sts, carry):
        if mask is not None:
            st = jnp.where(mask, st, NEG_BIG)
        m_old = _gather_lanes(m, ranges)
        m_new = jnp.maximum(m_old, jnp.max(st, axis=0, keepdims=True))
        alpha = jnp.exp2(m_old - m_new)
        p = jnp.exp2(st - m_new)
        stats.append((m_new, alpha, alpha * _gather_lanes(l, ranges) + jnp.sum(p, axis=0, keepdims=True), p))
    out = []
    for vt, (m, l, acc), (m_new, alpha, l_new, p) in zip(vts, carry, stats):
        acc_new = alpha * _gather_lanes(acc, ranges) + jnp.dot(
            vt, p.astype(vt.dtype), preferred_element_type=jnp.float32)
        out.append((_scatter_lanes(m, m_new, ranges), _scatter_lanes(l, l_new, ranges),
                    _scatter_lanes(acc, acc_new, ranges)))
    return tuple(out)


def _attn_specs(q0, k0, v0, g0, seq, heads):
    blk = lambda base: pl.BlockSpec((heads, seq, LANES), lambda b, hg: (base // heads + hg, b, 0))
    vt = pl.BlockSpec((heads, LANES, seq), lambda b, hg: (v0 // heads + hg, 0, b))
    return blk(q0), blk(k0), vt, blk(g0)


def _causal(tk, width, strict=False):
    rows = lax.broadcasted_iota(jnp.int32, (tk, width), 0)
    cols = lax.broadcasted_iota(jnp.int32, (tk, width), 1)
    return cols > rows if strict else cols >= rows


def _fox_kernel(q_ref, k_ref, vt_ref, g_ref, c_ref, o_ref, bias_ref, *, tq, tk):
    heads, seq = q_ref.shape[0], q_ref.shape[1]
    ratio = tq // tk
    lane = lax.broadcasted_iota(jnp.int32, (seq, LANES), 1)
    for g in range(heads):
        h = pl.program_id(1) * heads + g
        col = jnp.sum(jnp.where(lane == h, c_ref[0], 0.0), axis=1, keepdims=True)
        bias_ref[g] = jnp.broadcast_to(col * (-LOG2E), (seq, LANES))

    for qi in range(seq // tq):
        r0 = qi * tq
        qs = [q_ref[g, r0:r0 + tq, :] for g in range(heads)]

        def scores(c0, lane0):
            return [_lane_tiles(_kq(k_ref[g, pl.ds(c0, tk), :], qs[g][lane0:]),
                                lambda a, g=g: a + bias_ref[g, pl.ds(c0, tk), :])
                    for g in range(heads)]

        def values(c0):
            return [vt_ref[g, :, pl.ds(c0, tk)] for g in range(heads)]

        def body(j, carry):
            c0 = pl.multiple_of(j * tk, tk)
            return _softmax_steps(scores(c0, 0), values(c0), carry)

        init = (jnp.full((1, tq), NEG_BIG, jnp.float32), jnp.zeros((1, tq), jnp.float32),
                jnp.zeros((HEAD_DIM, tq), jnp.float32))
        carry = lax.fori_loop(0, qi * ratio, body, (init,) * heads)
        for d in range(ratio):
            ranges = [(d * tk, tq)] if d else None
            carry = _softmax_steps(scores(r0 + d * tk, d * tk), values(r0 + d * tk), carry, ranges,
                                   _causal(tk, tq - d * tk))
        for g in range(heads):
            _, l, acc = carry[g]
            out = (acc / l).T * _silu(g_ref[g, r0:r0 + tq, :])
            o_ref[r0:r0 + tq, g * LANES:(g + 1) * LANES] = out.astype(o_ref.dtype)


def _fox_attention(qk, vt, gates, c, batch, seq, tq, tk, heads):
    q_s, k_s, v_s, g_s = _attn_specs(FOX_Q0, FOX_K0, FOX_V0, GATE_FOX0, seq, heads)
    return pl.pallas_call(
        functools.partial(_fox_kernel, tq=tq, tk=tk),
        grid=(batch, FOX_HEADS // heads),
        in_specs=[q_s, k_s, v_s, g_s, pl.BlockSpec((1, seq, LANES), lambda b, hg: (b, 0, 0))],
        out_specs=pl.BlockSpec((seq, heads * LANES), lambda b, hg: (b, hg)),
        out_shape=jax.ShapeDtypeStruct((batch * seq, FOX_W), jnp.bfloat16),
        scratch_shapes=[pltpu.VMEM((heads, seq, LANES), jnp.float32)],
        compiler_params=_params("parallel", "parallel"),
        name="fox_attention",
    )(qk, qk, vt, gates, c)


def _sb_blocks(qs, ks, vts, ut2, states, ranges, strict):
    tk = ks[0].shape[0]
    zs = [_kq(k, q) for k, q in zip(ks, qs)]
    parts = []
    for z in zs:
        neg_abs = lax.bitcast_convert_type(
            lax.bitcast_convert_type(z, jnp.uint32) | jnp.uint32(0x80000000), jnp.float32)
        sp = jnp.log2(1.0 + jnp.exp2(neg_abs))
        log_beta = jnp.minimum(z, 0.0) - sp
        log_keep = log_beta - z
        if strict is not None:
            log_keep = jnp.where(strict, log_keep, 0.0)
        hi = log_keep.astype(jnp.bfloat16)
        lo = (log_keep - hi.astype(jnp.float32)).astype(jnp.bfloat16)
        parts.append((log_beta, log_keep, hi, lo))
    subs = []
    for _, _, hi, lo in parts:
        subs.append([
            jnp.dot(ut2, jnp.concatenate([hi[i:i + SB_SUB], lo[i:i + SB_SUB]], axis=0),
                    preferred_element_type=jnp.float32)
            for i in range(0, tk, SB_SUB)])
    ws, carries = [], []
    for (log_beta, log_keep, _, _), sub, (carry_all, _) in zip(parts, subs, states):
        carry = _gather_lanes(carry_all, ranges)
        laters = [None] * len(sub)
        for n in reversed(range(len(sub))):
            laters[n] = sub[n] + carry
            carry = carry + sub[n][0:1] + log_keep[n * SB_SUB:n * SB_SUB + 1]
        w = jnp.exp2(log_beta + jnp.concatenate(laters, axis=0))
        if strict is not None:
            w = jnp.where(strict, w, 0.0)
        ws.append(w)
        carries.append(_scatter_lanes(carry_all, carry, ranges))
    out = []
    for w, vt, carry, (_, acc) in zip(ws, vts, carries, states):
        acc_new = _gather_lanes(acc, ranges) + jnp.dot(vt, w.astype(vt.dtype),
                                                       preferred_element_type=jnp.float32)
        out.append((carry, _scatter_lanes(acc, acc_new, ranges)))
    return tuple(out)


def _sb_kernel(q_ref, k_ref, vt_ref, g_ref, ut_ref, o_ref, *, tq, tk):
    heads, seq = q_ref.shape[0], q_ref.shape[1]
    ratio = tq // tk
    ut2 = ut_ref[...]

    for qi in range(seq // tq):
        r0 = qi * tq
        qs = [q_ref[g, r0:r0 + tq, :] for g in range(heads)]

        def block(c0, lane0, state, ranges, mask):
            return _sb_blocks([q[lane0:] for q in qs],
                              [k_ref[g, pl.ds(c0, tk), :] for g in range(heads)],
                              [vt_ref[g, :, pl.ds(c0, tk)] for g in range(heads)], ut2, state, ranges, mask)

        init = (jnp.zeros((1, tq), jnp.float32), jnp.zeros((HEAD_DIM, tq), jnp.float32))
        state = (init,) * heads
        for d in reversed(range(ratio)):
            state = block(r0 + d * tk, d * tk, state, [(d * tk, tq)] if d else None,
                          _causal(tk, tq - d * tk, strict=True))
        n_full = qi * ratio
        state = lax.fori_loop(
            0, n_full,
            lambda i, st: block(pl.multiple_of((n_full - 1 - i) * tk, tk), 0, st, None, None), state)
        for g in range(heads):
            out = state[g][1].T * _silu(g_ref[g, r0:r0 + tq, :])
            o_ref[r0:r0 + tq, g * LANES:(g + 1) * LANES] = out.astype(o_ref.dtype)


def _sb_attention(qk, vt, gates, batch, seq, tq, tk, heads):
    q_s, k_s, v_s, g_s = _attn_specs(SB_Q0, SB_K0, SB_V0, GATE_SB0, seq, heads)
    upper = (lax.broadcasted_iota(jnp.int32, (SB_SUB, SB_SUB), 1)
             > lax.broadcasted_iota(jnp.int32, (SB_SUB, SB_SUB), 0)).astype(jnp.bfloat16)
    ut2 = jnp.concatenate([upper, upper], axis=1)
    return pl.pallas_call(
        functools.partial(_sb_kernel, tq=tq, tk=tk),
        grid=(batch, SB_HEADS // heads),
        in_specs=[q_s, k_s, v_s, g_s, pl.BlockSpec((SB_SUB, 2 * SB_SUB), lambda b, hg: (0, 0))],
        out_specs=pl.BlockSpec((seq, heads * LANES), lambda b, hg: (b, hg)),
        out_shape=jax.ShapeDtypeStruct((batch * seq, SB_W), jnp.bfloat16),
        compiler_params=_params("parallel", "parallel"),
        name="sb_attention",
    )(qk, qk, vt, gates, ut2)


def _diff_kernel(lam_ref, gsub_ref, q_ref, k_ref, vt_ref, g_ref, o_ref, *, tq, tk, lam_init):
    heads, seq = q_ref.shape[0], q_ref.shape[1]
    ratio = tq // tk
    lp = lam_ref[...]
    lam = (jnp.exp(jnp.sum(lp[0:1] * lp[1:2], axis=1, keepdims=True))
           - jnp.exp(jnp.sum(lp[2:3] * lp[3:4], axis=1, keepdims=True)) + lam_init)
    key = lax.broadcasted_iota(jnp.int32, (tk, LANES), 0).astype(jnp.float32)
    ramps = []
    for g in range(heads):
        h = pl.program_id(1) * heads + g
        expo = jnp.full((tk, LANES), 127 - (8 // DIFF_HEADS) * (h + 1), jnp.int32)
        slope = lax.bitcast_convert_type(expo << 23, jnp.float32) * LOG2E
        ramps.append((slope, slope * key))
    lane = lax.broadcasted_iota(jnp.int32, (tq, HEAD_DIM), 1)

    for qi in range(seq // tq):
        r0 = qi * tq
        q1s, q2s = [], []
        for g in range(heads):
            q = q_ref[g, r0:r0 + tq, :]
            zero = jnp.zeros_like(q)
            q1s.append(jnp.where(lane < DIFF_QK_DIM, q, zero))
            q2s.append(jnp.where(lane >= DIFF_QK_DIM, q, zero))

        def scores(c0, lane0):
            out = []
            for g in range(heads):
                slope, ramp = ramps[g]
                bias = ramp + slope * jnp.asarray(c0 - r0, jnp.float32)
                qq = jnp.concatenate([q1s[g][lane0:], q2s[g][lane0:]], axis=0)
                out.append(_lane_tiles(_kq(k_ref[g, pl.ds(c0, tk), :], qq), lambda a, b=bias: a + b))
            return out

        def values(c0):
            return [vt_ref[g, :, pl.ds(c0, tk)] for g in range(heads)]

        def body(j, carry):
            c0 = pl.multiple_of(j * tk, tk)
            return _softmax_steps(scores(c0, 0), values(c0), carry)

        init = (jnp.full((1, 2 * tq), NEG_BIG, jnp.float32), jnp.zeros((1, 2 * tq), jnp.float32),
                jnp.zeros((HEAD_DIM, 2 * tq), jnp.float32))
        carry = lax.fori_loop(0, qi * ratio, body, (init,) * heads)
        for d in range(ratio):
            ranges = [(d * tk, tq), (tq + d * tk, 2 * tq)] if d else None
            half = _causal(tk, tq - d * tk)
            carry = _softmax_steps(scores(r0 + d * tk, d * tk), values(r0 + d * tk), carry, ranges,
                                   jnp.concatenate([half, half], axis=1))
        for g in range(heads):
            _, l, acc = carry[g]
            o = acc / l
            o = o[:, 0:tq] - lam * o[:, tq:2 * tq]
            o = o * lax.rsqrt(jnp.mean(o * o, axis=0, keepdims=True) + SUBLN_EPS)
            o = o.T * gsub_ref[...] * (1.0 - lam_init)
            out = o * _silu(g_ref[g, r0:r0 + tq, :])
            o_ref[r0:r0 + tq, g * LANES:(g + 1) * LANES] = out.astype(o_ref.dtype)


def _diff_attention(qk, vt, gates, lam_p, subln_row, batch, seq, tq, tk, lam_init, heads):
    q_s, k_s, v_s, g_s = _attn_specs(DIFF_Q0, DIFF_K0, DIFF_V0, GATE_DIFF0, seq, heads)
    return pl.pallas_call(
        functools.partial(_diff_kernel, tq=tq, tk=tk, lam_init=lam_init),
        grid=(batch, DIFF_HEADS // heads),
        in_specs=[pl.BlockSpec(lam_p.shape, lambda b, hg: (0, 0)),
                  pl.BlockSpec((1, HEAD_DIM), lambda b, hg: (0, 0)),
                  q_s, k_s, v_s, g_s],
        out_specs=pl.BlockSpec((seq, heads * LANES), lambda b, hg: (b, hg)),
        out_shape=jax.ShapeDtypeStruct((batch * seq, DIFF_W), jnp.bfloat16),
        compiler_params=_params("parallel", "parallel"),
        name="diff_attention",
    )(lam_p, subln_row, qk, qk, vt, gates)


def _merge_kernel(of_ref, os_ref, od_ref, wf_ref, ws_ref, wd_ref, x_ref, g_ref, b_ref, y_ref, yb_ref):
    y = (jnp.dot(of_ref[...], wf_ref[...], preferred_element_type=jnp.float32)
         + jnp.dot(os_ref[...], ws_ref[...], preferred_element_type=jnp.float32)
         + jnp.dot(od_ref[...], wd_ref[...], preferred_element_type=jnp.float32))
    z = DEEPNORM_ALPHA * x_ref[...] + y
    mu = jnp.mean(z, axis=1, keepdims=True)
    zc = z - mu
    var = jnp.mean(zc * zc, axis=1, keepdims=True)
    out = zc * lax.rsqrt(var + LN_EPS) * g_ref[...] + b_ref[...]
    y_ref[...] = out
    yb_ref[...] = out.astype(yb_ref.dtype)


def _merge(o_fox, o_sb, o_diff, w_f, w_s, w_d, x, ln_g, ln_b, tm):
    t, d = x.shape
    row = lambda w: pl.BlockSpec((tm, w), lambda i: (i, 0))
    full = lambda a: pl.BlockSpec(a.shape, lambda i: (0, 0))
    return pl.pallas_call(
        _merge_kernel,
        grid=(t // tm,),
        in_specs=[row(FOX_W), row(SB_W), row(DIFF_W), full(w_f), full(w_s), full(w_d),
                  row(d), full(ln_g), full(ln_b)],
        out_specs=[row(d), row(d)],
        out_shape=[jax.ShapeDtypeStruct((t, d), jnp.float32),
                   jax.ShapeDtypeStruct((t, d), jnp.bfloat16)],
        compiler_params=_params("parallel"),
        name="merge_layernorm",
    )(o_fox, o_sb, o_diff, w_f, w_s, w_d, x, ln_g, ln_b)


def _split_in_weights(w_in):
    sizes = (FOX_W,) * 4 + (SB_W,) * 4 + (DIFF_W,) * 4 + (FOX_HEADS,)
    offs = [0]
    for n in sizes:
        offs.append(offs[-1] + n)
    part = lambda i: w_in[:, :, offs[i]:offs[i + 1]]
    fq, fk, fv, fg, sq, sk, sv, sg, dq, dk, dv, dg, ff = (part(i) for i in range(13))
    w_qk = jnp.concatenate([fq, fk, sq, sk, dq, dk], axis=-1).astype(jnp.bfloat16)
    w_v = jnp.concatenate([fv, sv, dv], axis=-1).astype(jnp.bfloat16)
    ff = jnp.pad(ff, ((0, 0), (0, 0), (0, LANES - FOX_HEADS)))
    w_gate = jnp.concatenate([fg, sg, dg, ff], axis=-1).astype(jnp.bfloat16)
    return w_qk, w_v, w_gate


def _qk_col_scale():
    one = lambda n: jnp.ones((n,), jnp.float32)
    full = lambda n, v: jnp.full((n,), v, jnp.float32)
    return jnp.concatenate([
        full(FOX_W, HEAD_DIM ** -0.5 * LOG2E), one(FOX_W),
        full(SB_W, HEAD_DIM ** -0.5 * LOG2E), one(SB_W),
        full(DIFF_W, DIFF_QK_DIM ** -0.5 * LOG2E), one(DIFF_W)])[None, :]


def kernel(x, w_in, b_f, diff_lambda, diff_subln_g, w_out, ln_g, ln_b):
    batch, seq, d_model = x.shape
    depth = w_in.shape[0]
    t = batch * seq
    tk = min(seq, KEY_BLOCK)
    tq = min(seq, ATTN_TQ)
    tm = min(t, 1024)

    w_qk, w_v, w_gate = _split_in_weights(w_in)
    w_out_b = w_out.astype(jnp.bfloat16)
    qk_scale = _qk_col_scale()
    v_scale = jnp.ones((1, FOX_W + SB_W + DIFF_W), jnp.float32)
    gate_scale = jnp.ones((1, GATE_BLOCKS * LANES), jnp.float32)
    b_f_rows = jnp.pad(b_f, ((0, 0), (0, LANES - FOX_HEADS)))

    xf = x.reshape(t, d_model)
    xb = xf.astype(jnp.bfloat16)
    for l in range(depth):
        lam_init = 0.8 - 0.6 * math.exp(-0.3 * l)
        qk = _project(xb, w_qk[l], qk_scale, jnp.bfloat16, tm, 1024, "qk_projection")
        vt = _project(xb, w_v[l], v_scale, jnp.bfloat16, tm, 1024, "v_projection", transpose_out=True)
        gates = _project(xb, w_gate[l], gate_scale, jnp.float32, min(t, 512), GATE_BLOCKS * LANES,
                         "gate_projection")
        c = _forget_prefix(gates, b_f_rows[l:l + 1], batch, seq, tk)
        o_fox = _fox_attention(qk, vt, gates, c, batch, seq, tq, tk, FOX_GROUP)
        o_sb = _sb_attention(qk, vt, gates, batch, seq, tq, tk, SB_GROUP)
        o_diff = _diff_attention(qk, vt, gates, diff_lambda[l], diff_subln_g[l:l + 1], batch, seq, tq, tk,
                                 lam_init, DIFF_GROUP)
        wl = w_out_b[l]
        xf, xb = _merge(o_fox, o_sb, o_diff, wl[:FOX_W], wl[FOX_W:FOX_W + SB_W], wl[FOX_W + SB_W:],
                        xf, ln_g[l:l + 1], ln_b[l:l + 1], min(t, 512))
    return xf.reshape(batch, seq, d_model)
```

```python
import functools
import math

import jax
import jax.numpy as jnp
from jax import lax
from jax.experimental import pallas as pl
from jax.experimental.pallas import tpu as pltpu

HEAD_DIM = 128
FOX_HEADS = 6
SB_HEADS = 6
DIFF_HEADS = 4
DIFF_QK_DIM = HEAD_DIM // 2
FOX_W = FOX_HEADS * HEAD_DIM
SB_W = SB_HEADS * HEAD_DIM
DIFF_W = DIFF_HEADS * HEAD_DIM
DEPTH_FOR_DEEPNORM = 4
DEEPNORM_ALPHA = (2 * DEPTH_FOR_DEEPNORM) ** 0.25
LN_EPS = 1e-5
SUBLN_EPS = 1e-5
NEG_BIG = -1e30
LOG2E = math.log2(math.e)

LANES = 128
VMEM_LIMIT_BYTES = 56 * 1024 * 1024

FOX_Q0, FOX_K0 = 0, FOX_HEADS
SB_Q0, SB_K0 = 2 * FOX_HEADS, 2 * FOX_HEADS + SB_HEADS
DIFF_Q0, DIFF_K0 = 2 * (FOX_HEADS + SB_HEADS), 2 * (FOX_HEADS + SB_HEADS) + DIFF_HEADS
FOX_V0, SB_V0, DIFF_V0 = 0, FOX_HEADS, FOX_HEADS + SB_HEADS
GATE_FOX0, GATE_SB0, GATE_DIFF0 = 0, FOX_HEADS, FOX_HEADS + SB_HEADS
GATE_FF = FOX_HEADS + SB_HEADS + DIFF_HEADS
GATE_BLOCKS = GATE_FF + 1
FOX_GROUP, SB_GROUP, DIFF_GROUP = 3, 3, 2
ATTN_TQ = 2048
KEY_BLOCK = 256
SB_SUB = 128


def _params(*semantics):
    return pltpu.CompilerParams(dimension_semantics=semantics, vmem_limit_bytes=VMEM_LIMIT_BYTES)


def _proj_kernel(x_ref, w_ref, s_ref, o_ref, *, transpose_out):
    acc = jnp.dot(x_ref[...], w_ref[...], preferred_element_type=jnp.float32)
    acc = acc * s_ref[...]
    for c in range(o_ref.shape[0]):
        blk = acc[:, c * LANES:(c + 1) * LANES]
        o_ref[c] = (blk.T if transpose_out else blk).astype(o_ref.dtype)


def _project(x, w, col_scale, out_dtype, tm, tn, name, transpose_out=False):
    t, k = x.shape
    n = w.shape[1]
    if transpose_out:
        out_spec = pl.BlockSpec((tn // LANES, LANES, tm), lambda i, j: (j, 0, i))
        out_shape = (n // LANES, LANES, t)
    else:
        out_spec = pl.BlockSpec((tn // LANES, tm, LANES), lambda i, j: (j, i, 0))
        out_shape = (n // LANES, t, LANES)
    return pl.pallas_call(
        functools.partial(_proj_kernel, transpose_out=transpose_out),
        grid=(t // tm, n // tn),
        in_specs=[
            pl.BlockSpec((tm, k), lambda i, j: (i, 0)),
            pl.BlockSpec((k, tn), lambda i, j: (0, j)),
            pl.BlockSpec((1, tn), lambda i, j: (0, j)),
        ],
        out_specs=out_spec,
        out_shape=jax.ShapeDtypeStruct(out_shape, out_dtype),
        compiler_params=_params("parallel", "parallel"),
        name=name,
    )(x, w, col_scale)


def _log_sigmoid(z):
    return jnp.minimum(z, 0.0) - jnp.log1p(jnp.exp(-jnp.abs(z)))


def _split3(x):
    x1 = x.astype(jnp.bfloat16)
    r = x - x1.astype(jnp.float32)
    x2 = r.astype(jnp.bfloat16)
    x3 = (r - x2.astype(jnp.float32)).astype(jnp.bfloat16)
    return x1, x2, x3


def _decay_kernel(ff_ref, bf_ref, tri_ref, c_ref, *, chunk):
    s = ff_ref.shape[1]
    tri = tri_ref[...]
    carry = jnp.zeros((1, LANES), jnp.float32)
    for c in range(s // chunk):
        lf = _log_sigmoid(ff_ref[0, c * chunk:(c + 1) * chunk, :] + bf_ref[...])
        x1, x2, x3 = _split3(lf)
        cs = (jnp.dot(tri, x1, preferred_element_type=jnp.float32)
              + jnp.dot(tri, x2, preferred_element_type=jnp.float32)
              + jnp.dot(tri, x3, preferred_element_type=jnp.float32)) + carry
        c_ref[0, c * chunk:(c + 1) * chunk, :] = cs
        carry = cs[chunk - 1:chunk, :]


def _forget_prefix(gates, b_f_row, batch, seq, chunk):
    tri = (lax.broadcasted_iota(jnp.int32, (chunk, chunk), 0)
           >= lax.broadcasted_iota(jnp.int32, (chunk, chunk), 1)).astype(jnp.bfloat16)
    return pl.pallas_call(
        functools.partial(_decay_kernel, chunk=chunk),
        grid=(batch,),
        in_specs=[
            pl.BlockSpec((1, seq, LANES), lambda b: (GATE_FF, b, 0)),
            pl.BlockSpec((1, LANES), lambda b: (0, 0)),
            pl.BlockSpec((chunk, chunk), lambda b: (0, 0)),
        ],
        out_specs=pl.BlockSpec((1, seq, LANES), lambda b: (b, 0, 0)),
        out_shape=jax.ShapeDtypeStruct((batch, seq, LANES), jnp.float32),
        compiler_params=_params("parallel"),
        name="forget_prefix",
    )(gates, b_f_row, tri)


def _kq(k, q):
    return lax.dot_general(k, q, (((1,), (1,)), ((), ())), preferred_element_type=jnp.float32)


def _silu(g):
    return g * (1.0 / (1.0 + jnp.exp(-g)))


def _lane_tiles(a, tile):
    n = a.shape[1] // LANES
    return jnp.concatenate([tile(a[:, i * LANES:(i + 1) * LANES]) for i in range(n)], axis=1)


def _gather_lanes(x, ranges):
    if ranges is None:
        return x
    return jnp.concatenate([x[:, a:b] for a, b in ranges], axis=1)


def _scatter_lanes(x, new, ranges):
    if ranges is None:
        return new
    pieces, pos, off = [], 0, 0
    for a, b in ranges:
        if a > pos:
            pieces.append(x[:, pos:a])
        pieces.append(new[:, off:off + b - a])
        off += b - a
        pos = b
    if pos < x.shape[1]:
        pieces.append(x[:, pos:])
    return jnp.concatenate(pieces, axis=1)


ONES_ROWS = 16


def _with_ones_rows(vt):
    ones = (lax.broadcasted_iota(jnp.int32, (ONES_ROWS, vt.shape[1]), 0) == 0).astype(vt.dtype)
    return jnp.concatenate([vt, ones], axis=0)


def _softmax_steps(sts, vts, carry, ranges=None, mask=None):
    stats = []
    for st, (m, _) in zip(sts, carry):
        if mask is not None:
            st = jnp.where(mask, st, NEG_BIG)
        m_old = _gather_lanes(m, ranges)
        m_new = jnp.maximum(m_old, jnp.max(st, axis=0, keepdims=True))
        stats.append((m_new, jnp.exp2(m_old - m_new), jnp.exp2(st - m_new)))
    out = []
    for vt, (m, acc), (m_new, alpha, p) in zip(vts, carry, stats):
        acc_new = alpha * _gather_lanes(acc, ranges) + jnp.dot(
            vt, p.astype(vt.dtype), preferred_element_type=jnp.float32)
        out.append((_scatter_lanes(m, m_new, ranges), _scatter_lanes(acc, acc_new, ranges)))
    return tuple(out)


def _normalised(acc):
    return acc[0:HEAD_DIM] * (1.0 / acc[HEAD_DIM:HEAD_DIM + 1])


def _attn_specs(q0, k0, v0, g0, seq, heads):
    blk = lambda base: pl.BlockSpec((heads, seq, LANES), lambda b, hg: (base // heads + hg, b, 0))
    vt = pl.BlockSpec((heads, LANES, seq), lambda b, hg: (v0 // heads + hg, 0, b))
    return blk(q0), blk(k0), vt, blk(g0)


def _causal(tk, width, strict=False):
    rows = lax.broadcasted_iota(jnp.int32, (tk, width), 0)
    cols = lax.broadcasted_iota(jnp.int32, (tk, width), 1)
    return cols > rows if strict else cols >= rows


def _fox_kernel(q_ref, k_ref, vt_ref, g_ref, c_ref, o_ref, bias_ref, vta_ref, *, tq, tk):
    heads, seq = q_ref.shape[0], q_ref.shape[1]
    ratio = tq // tk
    lane = lax.broadcasted_iota(jnp.int32, (seq, LANES), 1)
    for g in range(heads):
        h = pl.program_id(1) * heads + g
        col = jnp.sum(jnp.where(lane == h, c_ref[0], 0.0), axis=1, keepdims=True)
        bias_ref[g] = jnp.broadcast_to(col * (-LOG2E), (seq, LANES))
        vta_ref[g] = _with_ones_rows(vt_ref[g])

    for qi in range(seq // tq):
        r0 = qi * tq
        qs = [q_ref[g, r0:r0 + tq, :] for g in range(heads)]

        def scores(c0, lane0):
            return [_lane_tiles(_kq(k_ref[g, pl.ds(c0, tk), :], qs[g][lane0:]),
                                lambda a, g=g: a + bias_ref[g, pl.ds(c0, tk), :])
                    for g in range(heads)]

        def values(c0):
            return [vta_ref[g, :, pl.ds(c0, tk)] for g in range(heads)]

        def body(j, carry):
            c0 = pl.multiple_of(j * tk, tk)
            return _softmax_steps(scores(c0, 0), values(c0), carry)

        init = (jnp.full((1, tq), NEG_BIG, jnp.float32),
                jnp.zeros((HEAD_DIM + ONES_ROWS, tq), jnp.float32))
        carry = lax.fori_loop(0, qi * ratio, body, (init,) * heads)
        for d in range(ratio):
            ranges = [(d * tk, tq)] if d else None
            carry = _softmax_steps(scores(r0 + d * tk, d * tk), values(r0 + d * tk), carry, ranges,
                                   _causal(tk, tq - d * tk))
        for g in range(heads):
            out = _normalised(carry[g][1]).T * _silu(g_ref[g, r0:r0 + tq, :])
            o_ref[r0:r0 + tq, g * LANES:(g + 1) * LANES] = out.astype(o_ref.dtype)


def _fox_attention(qk, vt, gates, c, batch, seq, tq, tk, heads):
    q_s, k_s, v_s, g_s = _attn_specs(FOX_Q0, FOX_K0, FOX_V0, GATE_FOX0, seq, heads)
    return pl.pallas_call(
        functools.partial(_fox_kernel, tq=tq, tk=tk),
        grid=(batch, FOX_HEADS // heads),
        in_specs=[q_s, k_s, v_s, g_s, pl.BlockSpec((1, seq, LANES), lambda b, hg: (b, 0, 0))],
        out_specs=pl.BlockSpec((seq, heads * LANES), lambda b, hg: (b, hg)),
        out_shape=jax.ShapeDtypeStruct((batch * seq, FOX_W), jnp.bfloat16),
        scratch_shapes=[pltpu.VMEM((heads, seq, LANES), jnp.float32),
                        pltpu.VMEM((heads, HEAD_DIM + ONES_ROWS, seq), jnp.bfloat16)],
        compiler_params=_params("parallel", "parallel"),
        name="fox_attention",
    )(qk, qk, vt, gates, c)


def _sb_blocks(qs, ks, vts, ut2, states, ranges, strict):
    tk = ks[0].shape[0]
    zs = [_kq(k, q) for k, q in zip(ks, qs)]
    parts = []
    for z in zs:
        neg_abs = lax.bitcast_convert_type(
            lax.bitcast_convert_type(z, jnp.uint32) | jnp.uint32(0x80000000), jnp.float32)
        sp = jnp.log2(1.0 + jnp.exp2(neg_abs))
        log_beta = jnp.minimum(z, 0.0) - sp
        log_keep = log_beta - z
        if strict is not None:
            log_keep = jnp.where(strict, log_keep, 0.0)
        hi = log_keep.astype(jnp.bfloat16)
        lo = (log_keep - hi.astype(jnp.float32)).astype(jnp.bfloat16)
        parts.append((log_beta, log_keep, hi, lo))
    subs = []
    for _, _, hi, lo in parts:
        subs.append([
            jnp.dot(ut2, jnp.concatenate([hi[i:i + SB_SUB], lo[i:i + SB_SUB]], axis=0),
                    preferred_element_type=jnp.float32)
            for i in range(0, tk, SB_SUB)])
    ws, carries = [], []
    for (log_beta, log_keep, _, _), sub, (carry_all, _) in zip(parts, subs, states):
        carry = _gather_lanes(carry_all, ranges)
        laters = [None] * len(sub)
        for n in reversed(range(len(sub))):
            laters[n] = sub[n] + carry
            carry = carry + sub[n][0:1] + log_keep[n * SB_SUB:n * SB_SUB + 1]
        w = jnp.exp2(log_beta + jnp.concatenate(laters, axis=0))
        if strict is not None:
            w = jnp.where(strict, w, 0.0)
        ws.append(w)
        carries.append(_scatter_lanes(carry_all, carry, ranges))
    out = []
    for w, vt, carry, (_, acc) in zip(ws, vts, carries, states):
        acc_new = _gather_lanes(acc, ranges) + jnp.dot(vt, w.astype(vt.dtype),
                                                       preferred_element_type=jnp.float32)
        out.append((carry, _scatter_lanes(acc, acc_new, ranges)))
    return tuple(out)


def _sb_kernel(q_ref, k_ref, vt_ref, g_ref, ut_ref, o_ref, *, tq, tk):
    heads, seq = q_ref.shape[0], q_ref.shape[1]
    ratio = tq // tk
    ut2 = ut_ref[...]

    for qi in range(seq // tq):
        r0 = qi * tq
        qs = [q_ref[g, r0:r0 + tq, :] for g in range(heads)]

        def block(c0, lane0, state, ranges, mask):
            return _sb_blocks([q[lane0:] for q in qs],
                              [k_ref[g, pl.ds(c0, tk), :] for g in range(heads)],
                              [vt_ref[g, :, pl.ds(c0, tk)] for g in range(heads)], ut2, state, ranges, mask)

        init = (jnp.zeros((1, tq), jnp.float32), jnp.zeros((HEAD_DIM, tq), jnp.float32))
        state = (init,) * heads
        for d in reversed(range(ratio)):
            state = block(r0 + d * tk, d * tk, state, [(d * tk, tq)] if d else None,
                          _causal(tk, tq - d * tk, strict=True))
        n_full = qi * ratio
        state = lax.fori_loop(
            0, n_full,
            lambda i, st: block(pl.multiple_of((n_full - 1 - i) * tk, tk), 0, st, None, None), state)
        for g in range(heads):
            out = state[g][1].T * _silu(g_ref[g, r0:r0 + tq, :])
            o_ref[r0:r0 + tq, g * LANES:(g + 1) * LANES] = out.astype(o_ref.dtype)


def _sb_attention(qk, vt, gates, batch, seq, tq, tk, heads):
    q_s, k_s, v_s, g_s = _attn_specs(SB_Q0, SB_K0, SB_V0, GATE_SB0, seq, heads)
    upper = (lax.broadcasted_iota(jnp.int32, (SB_SUB, SB_SUB), 1)
             > lax.broadcasted_iota(jnp.int32, (SB_SUB, SB_SUB), 0)).astype(jnp.bfloat16)
    ut2 = jnp.concatenate([upper, upper], axis=1)
    return pl.pallas_call(
        functools.partial(_sb_kernel, tq=tq, tk=tk),
        grid=(batch, SB_HEADS // heads),
        in_specs=[q_s, k_s, v_s, g_s, pl.BlockSpec((SB_SUB, 2 * SB_SUB), lambda b, hg: (0, 0))],
        out_specs=pl.BlockSpec((seq, heads * LANES), lambda b, hg: (b, hg)),
        out_shape=jax.ShapeDtypeStruct((batch * seq, SB_W), jnp.bfloat16),
        compiler_params=_params("parallel", "parallel"),
        name="sb_attention",
    )(qk, qk, vt, gates, ut2)


def _diff_kernel(lam_ref, gsub_ref, q_ref, k_ref, vt_ref, g_ref, o_ref, vta_ref, *, tq, tk, lam_init):
    heads, seq = q_ref.shape[0], q_ref.shape[1]
    ratio = tq // tk
    lp = lam_ref[...]
    lam = (jnp.exp(jnp.sum(lp[0:1] * lp[1:2], axis=1, keepdims=True))
           - jnp.exp(jnp.sum(lp[2:3] * lp[3:4], axis=1, keepdims=True)) + lam_init)
    key = lax.broadcasted_iota(jnp.int32, (tk, LANES), 0).astype(jnp.float32)
    ramps = []
    for g in range(heads):
        h = pl.program_id(1) * heads + g
        expo = jnp.full((tk, LANES), 127 - (8 // DIFF_HEADS) * (h + 1), jnp.int32)
        slope = lax.bitcast_convert_type(expo << 23, jnp.float32) * LOG2E
        ramps.append((slope, slope * key))
        vta_ref[g] = _with_ones_rows(vt_ref[g])
    lane = lax.broadcasted_iota(jnp.int32, (tq, HEAD_DIM), 1)

    for qi in range(seq // tq):
        r0 = qi * tq
        q1s, q2s = [], []
        for g in range(heads):
            q = q_ref[g, r0:r0 + tq, :]
            zero = jnp.zeros_like(q)
            q1s.append(jnp.where(lane < DIFF_QK_DIM, q, zero))
            q2s.append(jnp.where(lane >= DIFF_QK_DIM, q, zero))

        def scores(c0, lane0):
            out = []
            for g in range(heads):
                slope, ramp = ramps[g]
                bias = ramp + slope * jnp.asarray(c0 - r0, jnp.float32)
                qq = jnp.concatenate([q1s[g][lane0:], q2s[g][lane0:]], axis=0)
                out.append(_lane_tiles(_kq(k_ref[g, pl.ds(c0, tk), :], qq), lambda a, b=bias: a + b))
            return out

        def values(c0):
            return [vta_ref[g, :, pl.ds(c0, tk)] for g in range(heads)]

        def body(j, carry):
            c0 = pl.multiple_of(j * tk, tk)
            return _softmax_steps(scores(c0, 0), values(c0), carry)

        init = (jnp.full((1, 2 * tq), NEG_BIG, jnp.float32),
                jnp.zeros((HEAD_DIM + ONES_ROWS, 2 * tq), jnp.float32))
        carry = lax.fori_loop(0, qi * ratio, body, (init,) * heads)
        for d in range(ratio):
            ranges = [(d * tk, tq), (tq + d * tk, 2 * tq)] if d else None
            half = _causal(tk, tq - d * tk)
            carry = _softmax_steps(scores(r0 + d * tk, d * tk), values(r0 + d * tk), carry, ranges,
                                   jnp.concatenate([half, half], axis=1))
        for g in range(heads):
            o = _normalised(carry[g][1])
            o = o[:, 0:tq] - lam * o[:, tq:2 * tq]
            o = o * lax.rsqrt(jnp.mean(o * o, axis=0, keepdims=True) + SUBLN_EPS)
            o = o.T * gsub_ref[...] * (1.0 - lam_init)
            out = o * _silu(g_ref[g, r0:r0 + tq, :])
            o_ref[r0:r0 + tq, g * LANES:(g + 1) * LANES] = out.astype(o_ref.dtype)


def _diff_attention(qk, vt, gates, lam_p, subln_row, batch, seq, tq, tk, lam_init, heads):
    q_s, k_s, v_s, g_s = _attn_specs(DIFF_Q0, DIFF_K0, DIFF_V0, GATE_DIFF0, seq, heads)
    return pl.pallas_call(
        functools.partial(_diff_kernel, tq=tq, tk=tk, lam_init=lam_init),
        grid=(batch, DIFF_HEADS // heads),
        in_specs=[pl.BlockSpec(lam_p.shape, lambda b, hg: (0, 0)),
                  pl.BlockSpec((1, HEAD_DIM), lambda b, hg: (0, 0)),
                  q_s, k_s, v_s, g_s],
        out_specs=pl.BlockSpec((seq, heads * LANES), lambda b, hg: (b, hg)),
        out_shape=jax.ShapeDtypeStruct((batch * seq, DIFF_W), jnp.bfloat16),
        scratch_shapes=[pltpu.VMEM((heads, HEAD_DIM + ONES_ROWS, seq), jnp.bfloat16)],
        compiler_params=_params("parallel", "parallel"),
        name="diff_attention",
    )(lam_p, subln_row, qk, qk, vt, gates)


def _merge_kernel(of_ref, os_ref, od_ref, wf_ref, ws_ref, wd_ref, x_ref, g_ref, b_ref, y_ref, yb_ref):
    y = (jnp.dot(of_ref[...], wf_ref[...], preferred_element_type=jnp.float32)
         + jnp.dot(os_ref[...], ws_ref[...], preferred_element_type=jnp.float32)
         + jnp.dot(od_ref[...], wd_ref[...], preferred_element_type=jnp.float32))
    z = DEEPNORM_ALPHA * x_ref[...] + y
    mu = jnp.mean(z, axis=1, keepdims=True)
    zc = z - mu
    var = jnp.mean(zc * zc, axis=1, keepdims=True)
    out = zc * lax.rsqrt(var + LN_EPS) * g_ref[...] + b_ref[...]
    y_ref[...] = out
    yb_ref[...] = out.astype(yb_ref.dtype)


def _merge(o_fox, o_sb, o_diff, w_f, w_s, w_d, x, ln_g, ln_b, tm):
    t, d = x.shape
    row = lambda w: pl.BlockSpec((tm, w), lambda i: (i, 0))
    full = lambda a: pl.BlockSpec(a.shape, lambda i: (0, 0))
    return pl.pallas_call(
        _merge_kernel,
        grid=(t // tm,),
        in_specs=[row(FOX_W), row(SB_W), row(DIFF_W), full(w_f), full(w_s), full(w_d),
                  row(d), full(ln_g), full(ln_b)],
        out_specs=[row(d), row(d)],
        out_shape=[jax.ShapeDtypeStruct((t, d), jnp.float32),
                   jax.ShapeDtypeStruct((t, d), jnp.bfloat16)],
        compiler_params=_params("parallel"),
        name="merge_layernorm",
    )(o_fox, o_sb, o_diff, w_f, w_s, w_d, x, ln_g, ln_b)


def _split_in_weights(w_in):
    sizes = (FOX_W,) * 4 + (SB_W,) * 4 + (DIFF_W,) * 4 + (FOX_HEADS,)
    offs = [0]
    for n in sizes:
        offs.append(offs[-1] + n)
    part = lambda i: w_in[:, :, offs[i]:offs[i + 1]]
    fq, fk, fv, fg, sq, sk, sv, sg, dq, dk, dv, dg, ff = (part(i) for i in range(13))
    w_qk = jnp.concatenate([fq, fk, sq, sk, dq, dk], axis=-1).astype(jnp.bfloat16)
    w_v = jnp.concatenate([fv, sv, dv], axis=-1).astype(jnp.bfloat16)
    ff = jnp.pad(ff, ((0, 0), (0, 0), (0, LANES - FOX_HEADS)))
    w_gate = jnp.concatenate([fg, sg, dg, ff], axis=-1).astype(jnp.bfloat16)
    return w_qk, w_v, w_gate


def _qk_col_scale():
    one = lambda n: jnp.ones((n,), jnp.float32)
    full = lambda n, v: jnp.full((n,), v, jnp.float32)
    return jnp.concatenate([
        full(FOX_W, HEAD_DIM ** -0.5 * LOG2E), one(FOX_W),
        full(SB_W, HEAD_DIM ** -0.5 * LOG2E), one(SB_W),
        full(DIFF_W, DIFF_QK_DIM ** -0.5 * LOG2E), one(DIFF_W)])[None, :]


def kernel(x, w_in, b_f, diff_lambda, diff_subln_g, w_out, ln_g, ln_b):
    batch, seq, d_model = x.shape
    depth = w_in.shape[0]
    t = batch * seq
    tk = min(seq, KEY_BLOCK)
    tq = min(seq, ATTN_TQ)
    tm = min(t, 1024)

    w_qk, w_v, w_gate = _split_in_weights(w_in)
    w_out_b = w_out.astype(jnp.bfloat16)
    qk_scale = _qk_col_scale()
    v_scale = jnp.ones((1, FOX_W + SB_W + DIFF_W), jnp.float32)
    gate_scale = jnp.ones((1, GATE_BLOCKS * LANES), jnp.float32)
    b_f_rows = jnp.pad(b_f, ((0, 0), (0, LANES - FOX_HEADS)))

    xf = x.reshape(t, d_model)
    xb = xf.astype(jnp.bfloat16)
    for l in range(depth):
        lam_init = 0.8 - 0.6 * math.exp(-0.3 * l)
        qk = _project(xb, w_qk[l], qk_scale, jnp.bfloat16, tm, 1024, "qk_projection")
        vt = _project(xb, w_v[l], v_scale, jnp.bfloat16, tm, 1024, "v_projection", transpose_out=True)
        gates = _project(xb, w_gate[l], gate_scale, jnp.float32, min(t, 512), GATE_BLOCKS * LANES,
                         "gate_projection")
        c = _forget_prefix(gates, b_f_rows[l:l + 1], batch, seq, tk)
        o_fox = _fox_attention(qk, vt, gates, c, batch, seq, tq, tk, FOX_GROUP)
        o_sb = _sb_attention(qk, vt, gates, batch, seq, tq, tk, SB_GROUP)
        o_diff = _diff_attention(qk, vt, gates, diff_lambda[l], diff_subln_g[l:l + 1], batch, seq, tq, tk,
                                 lam_init, DIFF_GROUP)
        wl = w_out_b[l]
        xf, xb = _merge(o_fox, o_sb, o_diff, wl[:FOX_W], wl[FOX_W:FOX_W + SB_W], wl[FOX_W + SB_W:],
                        xf, ln_g[l:l + 1], ln_b[l:l + 1], min(t, 512))
    return xf.reshape(batch, seq, d_model)
```

```python
import functools
import math

import jax
import jax.numpy as jnp
from jax import lax
from jax.experimental import pallas as pl
from jax.experimental.pallas import tpu as pltpu

HEAD_DIM = 128
FOX_HEADS = 6
SB_HEADS = 6
DIFF_HEADS = 4
DIFF_QK_DIM = HEAD_DIM // 2
FOX_W = FOX_HEADS * HEAD_DIM
SB_W = SB_HEADS * HEAD_DIM
DIFF_W = DIFF_HEADS * HEAD_DIM
DEPTH_FOR_DEEPNORM = 4
DEEPNORM_ALPHA = (2 * DEPTH_FOR_DEEPNORM) ** 0.25
LN_EPS = 1e-5
SUBLN_EPS = 1e-5
NEG_BIG = -1e30
LOG2E = math.log2(math.e)

LANES = 128
VMEM_LIMIT_BYTES = 56 * 1024 * 1024

FOX_Q0, FOX_K0 = 0, FOX_HEADS
SB_Q0, SB_K0 = 2 * FOX_HEADS, 2 * FOX_HEADS + SB_HEADS
DIFF_Q0, DIFF_K0 = 2 * (FOX_HEADS + SB_HEADS), 2 * (FOX_HEADS + SB_HEADS) + DIFF_HEADS
FOX_V0, SB_V0, DIFF_V0 = 0, FOX_HEADS, FOX_HEADS + SB_HEADS
GATE_FOX0, GATE_SB0, GATE_DIFF0 = 0, FOX_HEADS, FOX_HEADS + SB_HEADS
GATE_FF = FOX_HEADS + SB_HEADS + DIFF_HEADS
GATE_BLOCKS = GATE_FF + 1
FOX_GROUP, SB_GROUP, DIFF_GROUP = 3, 3, 2
ATTN_TQ = 2048
KEY_BLOCK = 256
SB_SUB = 128


def _params(*semantics):
    return pltpu.CompilerParams(dimension_semantics=semantics, vmem_limit_bytes=VMEM_LIMIT_BYTES)


def _proj_kernel(x_ref, w_ref, *rest, transpose_out):
    o_ref = rest[-1]
    acc = jnp.dot(x_ref[...].astype(w_ref.dtype), w_ref[...], preferred_element_type=jnp.float32)
    if len(rest) == 2:
        acc = acc * rest[0][...]
    for c in range(o_ref.shape[0]):
        blk = acc[:, c * LANES:(c + 1) * LANES]
        o_ref[c] = (blk.T if transpose_out else blk).astype(o_ref.dtype)


def _project(x, w, layer, col_scale, out_dtype, tm, tn, name, transpose_out=False):
    t, k = x.shape
    n = w.shape[2]
    if transpose_out:
        out_spec = pl.BlockSpec((tn // LANES, LANES, tm), lambda i, j: (j, 0, i))
        out_shape = (n // LANES, LANES, t)
    else:
        out_spec = pl.BlockSpec((tn // LANES, tm, LANES), lambda i, j: (j, i, 0))
        out_shape = (n // LANES, t, LANES)
    in_specs = [pl.BlockSpec((tm, k), lambda i, j: (i, 0)),
                pl.BlockSpec((None, k, tn), lambda i, j: (layer, 0, j))]
    operands = [x, w]
    if col_scale is not None:
        in_specs.append(pl.BlockSpec((1, tn), lambda i, j: (0, j)))
        operands.append(col_scale)
    return pl.pallas_call(
        functools.partial(_proj_kernel, transpose_out=transpose_out),
        grid=(t // tm, n // tn),
        in_specs=in_specs,
        out_specs=out_spec,
        out_shape=jax.ShapeDtypeStruct(out_shape, out_dtype),
        compiler_params=_params("parallel", "parallel"),
        name=name,
    )(*operands)


def _log_sigmoid(z):
    return jnp.minimum(z, 0.0) - jnp.log1p(jnp.exp(-jnp.abs(z)))


def _split3(x):
    x1 = x.astype(jnp.bfloat16)
    r = x - x1.astype(jnp.float32)
    x2 = r.astype(jnp.bfloat16)
    x3 = (r - x2.astype(jnp.float32)).astype(jnp.bfloat16)
    return x1, x2, x3


def _decay_kernel(ff_ref, bf_ref, tri_ref, c_ref, *, chunk):
    s = ff_ref.shape[1]
    tri = tri_ref[...]
    carry = jnp.zeros((1, LANES), jnp.float32)
    for c in range(s // chunk):
        lf = _log_sigmoid(ff_ref[0, c * chunk:(c + 1) * chunk, :] + bf_ref[...])
        x1, x2, x3 = _split3(lf)
        cs = (jnp.dot(tri, x1, preferred_element_type=jnp.float32)
              + jnp.dot(tri, x2, preferred_element_type=jnp.float32)
              + jnp.dot(tri, x3, preferred_element_type=jnp.float32)) + carry
        c_ref[0, c * chunk:(c + 1) * chunk, :] = cs
        carry = cs[chunk - 1:chunk, :]


def _forget_prefix(gates, b_f_row, batch, seq, chunk):
    tri = (lax.broadcasted_iota(jnp.int32, (chunk, chunk), 0)
           >= lax.broadcasted_iota(jnp.int32, (chunk, chunk), 1)).astype(jnp.bfloat16)
    return pl.pallas_call(
        functools.partial(_decay_kernel, chunk=chunk),
        grid=(batch,),
        in_specs=[
            pl.BlockSpec((1, seq, LANES), lambda b: (GATE_FF, b, 0)),
            pl.BlockSpec((1, LANES), lambda b: (0, 0)),
            pl.BlockSpec((chunk, chunk), lambda b: (0, 0)),
        ],
        out_specs=pl.BlockSpec((1, seq, LANES), lambda b: (b, 0, 0)),
        out_shape=jax.ShapeDtypeStruct((batch, seq, LANES), jnp.float32),
        compiler_params=_params("parallel"),
        name="forget_prefix",
    )(gates, b_f_row, tri)


def _kq(k, q):
    return lax.dot_general(k, q, (((1,), (1,)), ((), ())), preferred_element_type=jnp.float32)


def _silu(g):
    return g * (1.0 / (1.0 + jnp.exp(-g)))


def _lane_tiles(a, tile):
    n = a.shape[1] // LANES
    return jnp.concatenate([tile(a[:, i * LANES:(i + 1) * LANES]) for i in range(n)], axis=1)


def _gather_lanes(x, ranges):
    if ranges is None:
        return x
    return jnp.concatenate([x[:, a:b] for a, b in ranges], axis=1)


def _scatter_lanes(x, new, ranges):
    if ranges is None:
        return new
    pieces, pos, off = [], 0, 0
    for a, b in ranges:
        if a > pos:
            pieces.append(x[:, pos:a])
        pieces.append(new[:, off:off + b - a])
        off += b - a
        pos = b
    if pos < x.shape[1]:
        pieces.append(x[:, pos:])
    return jnp.concatenate(pieces, axis=1)


ONES_ROWS = 16


def _with_ones_rows(vt):
    ones = (lax.broadcasted_iota(jnp.int32, (ONES_ROWS, vt.shape[1]), 0) == 0).astype(vt.dtype)
    return jnp.concatenate([vt, ones], axis=0)


def _softmax_steps(sts, vts, carry, ranges=None, mask=None):
    stats = []
    for st, (m, _) in zip(sts, carry):
        if mask is not None:
            st = jnp.where(mask, st, NEG_BIG)
        m_old = _gather_lanes(m, ranges)
        m_new = jnp.maximum(m_old, jnp.max(st, axis=0, keepdims=True))
        stats.append((m_new, jnp.exp2(m_old - m_new), jnp.exp2(st - m_new)))
    out = []
    for vt, (m, acc), (m_new, alpha, p) in zip(vts, carry, stats):
        acc_new = alpha * _gather_lanes(acc, ranges) + jnp.dot(
            vt, p.astype(vt.dtype), preferred_element_type=jnp.float32)
        out.append((_scatter_lanes(m, m_new, ranges), _scatter_lanes(acc, acc_new, ranges)))
    return tuple(out)


def _normalised(acc):
    return acc[0:HEAD_DIM] * (1.0 / acc[HEAD_DIM:HEAD_DIM + 1])


def _attn_specs(q0, k0, v0, g0, seq, heads):
    blk = lambda base: pl.BlockSpec((heads, seq, LANES), lambda b, hg: (base // heads + hg, b, 0))
    vt = pl.BlockSpec((heads, LANES, seq), lambda b, hg: (v0 // heads + hg, 0, b))
    return blk(q0), blk(k0), vt, blk(g0)


def _causal(tk, width, strict=False):
    rows = lax.broadcasted_iota(jnp.int32, (tk, width), 0)
    cols = lax.broadcasted_iota(jnp.int32, (tk, width), 1)
    return cols > rows if strict else cols >= rows


def _fox_kernel(q_ref, k_ref, vt_ref, g_ref, c_ref, o_ref, bias_ref, vta_ref, *, tq, tk):
    heads, seq = q_ref.shape[0], q_ref.shape[1]
    ratio = tq // tk
    lane = lax.broadcasted_iota(jnp.int32, (seq, LANES), 1)
    for g in range(heads):
        h = pl.program_id(1) * heads + g
        col = jnp.sum(jnp.where(lane == h, c_ref[0], 0.0), axis=1, keepdims=True)
        bias_ref[g] = jnp.broadcast_to(col * (-LOG2E), (seq, LANES))
        vta_ref[g] = _with_ones_rows(vt_ref[g])

    for qi in range(seq // tq):
        r0 = qi * tq
        qs = [q_ref[g, r0:r0 + tq, :] for g in range(heads)]

        def scores(c0, lane0):
            return [_lane_tiles(_kq(k_ref[g, pl.ds(c0, tk), :], qs[g][lane0:]),
                                lambda a, g=g: a + bias_ref[g, pl.ds(c0, tk), :])
                    for g in range(heads)]

        def values(c0):
            return [vta_ref[g, :, pl.ds(c0, tk)] for g in range(heads)]

        def body(j, carry):
            c0 = pl.multiple_of(j * tk, tk)
            return _softmax_steps(scores(c0, 0), values(c0), carry)

        init = (jnp.full((1, tq), NEG_BIG, jnp.float32),
                jnp.zeros((HEAD_DIM + ONES_ROWS, tq), jnp.float32))
        carry = lax.fori_loop(0, qi * ratio, body, (init,) * heads)
        for d in range(ratio):
            ranges = [(d * tk, tq)] if d else None
            carry = _softmax_steps(scores(r0 + d * tk, d * tk), values(r0 + d * tk), carry, ranges,
                                   _causal(tk, tq - d * tk))
        for g in range(heads):
            out = _normalised(carry[g][1]).T * _silu(g_ref[g, r0:r0 + tq, :])
            o_ref[r0:r0 + tq, g * LANES:(g + 1) * LANES] = out.astype(o_ref.dtype)


def _fox_attention(qk, vt, gates, c, batch, seq, tq, tk, heads):
    q_s, k_s, v_s, g_s = _attn_specs(FOX_Q0, FOX_K0, FOX_V0, GATE_FOX0, seq, heads)
    return pl.pallas_call(
        functools.partial(_fox_kernel, tq=tq, tk=tk),
        grid=(batch, FOX_HEADS // heads),
        in_specs=[q_s, k_s, v_s, g_s, pl.BlockSpec((1, seq, LANES), lambda b, hg: (b, 0, 0))],
        out_specs=pl.BlockSpec((seq, heads * LANES), lambda b, hg: (b, hg)),
        out_shape=jax.ShapeDtypeStruct((batch * seq, FOX_W), jnp.bfloat16),
        scratch_shapes=[pltpu.VMEM((heads, seq, LANES), jnp.float32),
                        pltpu.VMEM((heads, HEAD_DIM + ONES_ROWS, seq), jnp.bfloat16)],
        compiler_params=_params("parallel", "parallel"),
        name="fox_attention",
    )(qk, qk, vt, gates, c)


def _sb_blocks(qs, ks, vts, ut2, states, ranges, strict):
    tk = ks[0].shape[0]
    zs = [_kq(k, q) for k, q in zip(ks, qs)]
    parts = []
    for z in zs:
        neg_abs = lax.bitcast_convert_type(
            lax.bitcast_convert_type(z, jnp.uint32) | jnp.uint32(0x80000000), jnp.float32)
        sp = jnp.log2(1.0 + jnp.exp2(neg_abs))
        log_beta = jnp.minimum(z, 0.0) - sp
        log_keep = log_beta - z
        if strict is not None:
            log_keep = jnp.where(strict, log_keep, 0.0)
        hi = log_keep.astype(jnp.bfloat16)
        lo = (log_keep - hi.astype(jnp.float32)).astype(jnp.bfloat16)
        parts.append((log_beta, log_keep, hi, lo))
    subs = []
    for _, _, hi, lo in parts:
        subs.append([
            jnp.dot(ut2, jnp.concatenate([hi[i:i + SB_SUB], lo[i:i + SB_SUB]], axis=0),
                    preferred_element_type=jnp.float32)
            for i in range(0, tk, SB_SUB)])
    ws, carries = [], []
    for (log_beta, log_keep, _, _), sub, (carry_all, _) in zip(parts, subs, states):
        carry = _gather_lanes(carry_all, ranges)
        laters = [None] * len(sub)
        for n in reversed(range(len(sub))):
            laters[n] = sub[n] + carry
            carry = carry + sub[n][0:1] + log_keep[n * SB_SUB:n * SB_SUB + 1]
        w = jnp.exp2(log_beta + jnp.concatenate(laters, axis=0))
        if strict is not None:
            w = jnp.where(strict, w, 0.0)
        ws.append(w)
        carries.append(_scatter_lanes(carry_all, carry, ranges))
    out = []
    for w, vt, carry, (_, acc) in zip(ws, vts, carries, states):
        acc_new = _gather_lanes(acc, ranges) + jnp.dot(vt, w.astype(vt.dtype),
                                                       preferred_element_type=jnp.float32)
        out.append((carry, _scatter_lanes(acc, acc_new, ranges)))
    return tuple(out)


def _sb_kernel(q_ref, k_ref, vt_ref, g_ref, ut_ref, o_ref, *, tq, tk):
    heads, seq = q_ref.shape[0], q_ref.shape[1]
    ratio = tq // tk
    ut2 = ut_ref[...]

    for qi in range(seq // tq):
        r0 = qi * tq
        qs = [q_ref[g, r0:r0 + tq, :] for g in range(heads)]

        def block(c0, lane0, state, ranges, mask):
            return _sb_blocks([q[lane0:] for q in qs],
                              [k_ref[g, pl.ds(c0, tk), :] for g in range(heads)],
                              [vt_ref[g, :, pl.ds(c0, tk)] for g in range(heads)], ut2, state, ranges, mask)

        init = (jnp.zeros((1, tq), jnp.float32), jnp.zeros((HEAD_DIM, tq), jnp.float32))
        state = (init,) * heads
        for d in reversed(range(ratio)):
            state = block(r0 + d * tk, d * tk, state, [(d * tk, tq)] if d else None,
                          _causal(tk, tq - d * tk, strict=True))
        n_full = qi * ratio
        state = lax.fori_loop(
            0, n_full,
            lambda i, st: block(pl.multiple_of((n_full - 1 - i) * tk, tk), 0, st, None, None), state)
        for g in range(heads):
            out = state[g][1].T * _silu(g_ref[g, r0:r0 + tq, :])
            o_ref[r0:r0 + tq, g * LANES:(g + 1) * LANES] = out.astype(o_ref.dtype)


def _sb_attention(qk, vt, gates, batch, seq, tq, tk, heads):
    q_s, k_s, v_s, g_s = _attn_specs(SB_Q0, SB_K0, SB_V0, GATE_SB0, seq, heads)
    upper = (lax.broadcasted_iota(jnp.int32, (SB_SUB, SB_SUB), 1)
             > lax.broadcasted_iota(jnp.int32, (SB_SUB, SB_SUB), 0)).astype(jnp.bfloat16)
    ut2 = jnp.concatenate([upper, upper], axis=1)
    return pl.pallas_call(
        functools.partial(_sb_kernel, tq=tq, tk=tk),
        grid=(batch, SB_HEADS // heads),
        in_specs=[q_s, k_s, v_s, g_s, pl.BlockSpec((SB_SUB, 2 * SB_SUB), lambda b, hg: (0, 0))],
        out_specs=pl.BlockSpec((seq, heads * LANES), lambda b, hg: (b, hg)),
        out_shape=jax.ShapeDtypeStruct((batch * seq, SB_W), jnp.bfloat16),
        compiler_params=_params("parallel", "parallel"),
        name="sb_attention",
    )(qk, qk, vt, gates, ut2)


def _diff_kernel(lam_ref, gsub_ref, q_ref, k_ref, vt_ref, g_ref, o_ref, vta_ref, *, tq, tk, lam_init):
    heads, seq = q_ref.shape[0], q_ref.shape[1]
    ratio = tq // tk
    lp = lam_ref[...]
    lam = (jnp.exp(jnp.sum(lp[0:1] * lp[1:2], axis=1, keepdims=True))
           - jnp.exp(jnp.sum(lp[2:3] * lp[3:4], axis=1, keepdims=True)) + lam_init)
    key = lax.broadcasted_iota(jnp.int32, (tk, LANES), 0).astype(jnp.float32)
    ramps = []
    for g in range(heads):
        h = pl.program_id(1) * heads + g
        expo = jnp.full((tk, LANES), 127 - (8 // DIFF_HEADS) * (h + 1), jnp.int32)
        slope = lax.bitcast_convert_type(expo << 23, jnp.float32) * LOG2E
        ramps.append((slope, slope * key))
        vta_ref[g] = _with_ones_rows(vt_ref[g])
    lane = lax.broadcasted_iota(jnp.int32, (tq, HEAD_DIM), 1)

    for qi in range(seq // tq):
        r0 = qi * tq
        q1s, q2s = [], []
        for g in range(heads):
            q = q_ref[g, r0:r0 + tq, :]
            zero = jnp.zeros_like(q)
            q1s.append(jnp.where(lane < DIFF_QK_DIM, q, zero))
            q2s.append(jnp.where(lane >= DIFF_QK_DIM, q, zero))

        def scores(c0, lane0):
            out = []
            for g in range(heads):
                slope, ramp = ramps[g]
                bias = ramp + slope * jnp.asarray(c0 - r0, jnp.float32)
                qq = jnp.concatenate([q1s[g][lane0:], q2s[g][lane0:]], axis=0)
                out.append(_lane_tiles(_kq(k_ref[g, pl.ds(c0, tk), :], qq), lambda a, b=bias: a + b))
            return out

        def values(c0):
            return [vta_ref[g, :, pl.ds(c0, tk)] for g in range(heads)]

        def body(j, carry):
            c0 = pl.multiple_of(j * tk, tk)
            return _softmax_steps(scores(c0, 0), values(c0), carry)

        init = (jnp.full((1, 2 * tq), NEG_BIG, jnp.float32),
                jnp.zeros((HEAD_DIM + ONES_ROWS, 2 * tq), jnp.float32))
        carry = lax.fori_loop(0, qi * ratio, body, (init,) * heads)
        for d in range(ratio):
            ranges = [(d * tk, tq), (tq + d * tk, 2 * tq)] if d else None
            half = _causal(tk, tq - d * tk)
            carry = _softmax_steps(scores(r0 + d * tk, d * tk), values(r0 + d * tk), carry, ranges,
                                   jnp.concatenate([half, half], axis=1))
        for g in range(heads):
            o = _normalised(carry[g][1])
            o = o[:, 0:tq] - lam * o[:, tq:2 * tq]
            o = o * lax.rsqrt(jnp.mean(o * o, axis=0, keepdims=True) + SUBLN_EPS)
            o = o.T * gsub_ref[...] * (1.0 - lam_init)
            out = o * _silu(g_ref[g, r0:r0 + tq, :])
            o_ref[r0:r0 + tq, g * LANES:(g + 1) * LANES] = out.astype(o_ref.dtype)


def _diff_attention(qk, vt, gates, lam_p, subln_row, batch, seq, tq, tk, lam_init, heads):
    q_s, k_s, v_s, g_s = _attn_specs(DIFF_Q0, DIFF_K0, DIFF_V0, GATE_DIFF0, seq, heads)
    return pl.pallas_call(
        functools.partial(_diff_kernel, tq=tq, tk=tk, lam_init=lam_init),
        grid=(batch, DIFF_HEADS // heads),
        in_specs=[pl.BlockSpec(lam_p.shape, lambda b, hg: (0, 0)),
                  pl.BlockSpec((1, HEAD_DIM), lambda b, hg: (0, 0)),
                  q_s, k_s, v_s, g_s],
        out_specs=pl.BlockSpec((seq, heads * LANES), lambda b, hg: (b, hg)),
        out_shape=jax.ShapeDtypeStruct((batch * seq, DIFF_W), jnp.bfloat16),
        scratch_shapes=[pltpu.VMEM((heads, HEAD_DIM + ONES_ROWS, seq), jnp.bfloat16)],
        compiler_params=_params("parallel", "parallel"),
        name="diff_attention",
    )(lam_p, subln_row, qk, qk, vt, gates)


def _merge_kernel(of_ref, os_ref, od_ref, wf_ref, ws_ref, wd_ref, x_ref, g_ref, b_ref, y_ref, yb_ref):
    y = (jnp.dot(of_ref[...], wf_ref[...], preferred_element_type=jnp.float32)
         + jnp.dot(os_ref[...], ws_ref[...], preferred_element_type=jnp.float32)
         + jnp.dot(od_ref[...], wd_ref[...], preferred_element_type=jnp.float32))
    z = DEEPNORM_ALPHA * x_ref[...] + y
    mu = jnp.mean(z, axis=1, keepdims=True)
    zc = z - mu
    var = jnp.mean(zc * zc, axis=1, keepdims=True)
    out = zc * lax.rsqrt(var + LN_EPS) * g_ref[...] + b_ref[...]
    y_ref[...] = out
    yb_ref[...] = out.astype(yb_ref.dtype)


def _merge(o_fox, o_sb, o_diff, w_out, layer, x, ln_g, ln_b, tm):
    t, d = x.shape
    row = lambda w: pl.BlockSpec((tm, w), lambda i: (i, 0))
    w_rows = lambda rows, start: pl.BlockSpec((None, rows, d), lambda i: (layer, start // rows, 0))
    vec = lambda: pl.BlockSpec((None, 1, d), lambda i: (layer, 0, 0))
    return pl.pallas_call(
        _merge_kernel,
        grid=(t // tm,),
        in_specs=[row(FOX_W), row(SB_W), row(DIFF_W),
                  w_rows(FOX_W, 0), w_rows(SB_W, FOX_W), w_rows(DIFF_W, FOX_W + SB_W),
                  row(d), vec(), vec()],
        out_specs=[row(d), row(d)],
        out_shape=[jax.ShapeDtypeStruct((t, d), jnp.float32),
                   jax.ShapeDtypeStruct((t, d), jnp.bfloat16)],
        compiler_params=_params("parallel"),
        name="merge_layernorm",
    )(o_fox, o_sb, o_diff, w_out, w_out, w_out, x, ln_g, ln_b)


def _split_in_weights(w_in):
    sizes = (FOX_W,) * 4 + (SB_W,) * 4 + (DIFF_W,) * 4 + (FOX_HEADS,)
    offs = [0]
    for n in sizes:
        offs.append(offs[-1] + n)
    part = lambda i: w_in[:, :, offs[i]:offs[i + 1]].astype(jnp.bfloat16)
    fq, fk, fv, fg, sq, sk, sv, sg, dq, dk, dv, dg, ff = (part(i) for i in range(13))
    w_qk = jnp.concatenate([fq, fk, sq, sk, dq, dk], axis=-1)
    w_v = jnp.concatenate([fv, sv, dv], axis=-1)
    ff = jnp.pad(ff, ((0, 0), (0, 0), (0, LANES - FOX_HEADS)))
    w_gate = jnp.concatenate([fg, sg, dg, ff], axis=-1)
    return w_qk, w_v, w_gate


def _qk_col_scale():
    one = lambda n: jnp.ones((n,), jnp.float32)
    full = lambda n, v: jnp.full((n,), v, jnp.float32)
    return jnp.concatenate([
        full(FOX_W, HEAD_DIM ** -0.5 * LOG2E), one(FOX_W),
        full(SB_W, HEAD_DIM ** -0.5 * LOG2E), one(SB_W),
        full(DIFF_W, DIFF_QK_DIM ** -0.5 * LOG2E), one(DIFF_W)])[None, :]


def kernel(x, w_in, b_f, diff_lambda, diff_subln_g, w_out, ln_g, ln_b):
    batch, seq, d_model = x.shape
    depth = w_in.shape[0]
    t = batch * seq
    tk = min(seq, KEY_BLOCK)
    tq = min(seq, ATTN_TQ)
    tm = min(t, 1024)

    w_qk, w_v, w_gate = _split_in_weights(w_in)
    w_out_b = w_out.astype(jnp.bfloat16)
    qk_scale = _qk_col_scale()
    b_f_rows = jnp.pad(b_f, ((0, 0), (0, LANES - FOX_HEADS)))
    ln_g3 = ln_g.reshape(depth, 1, d_model)
    ln_b3 = ln_b.reshape(depth, 1, d_model)

    xf = x.reshape(t, d_model)
    xb = xf
    for l in range(depth):
        lam_init = 0.8 - 0.6 * math.exp(-0.3 * l)
        qk = _project(xb, w_qk, l, qk_scale, jnp.bfloat16, tm, 1024, "qk_projection")
        vt = _project(xb, w_v, l, None, jnp.bfloat16, tm, 1024, "v_projection", transpose_out=True)
        gates = _project(xb, w_gate, l, None, jnp.float32, min(t, 512), GATE_BLOCKS * LANES,
                         "gate_projection")
        c = _forget_prefix(gates, b_f_rows[l:l + 1], batch, seq, tk)
        o_fox = _fox_attention(qk, vt, gates, c, batch, seq, tq, tk, FOX_GROUP)
        o_sb = _sb_attention(qk, vt, gates, batch, seq, tq, tk, SB_GROUP)
        o_diff = _diff_attention(qk, vt, gates, diff_lambda[l], diff_subln_g[l:l + 1], batch, seq, tq, tk,
                                 lam_init, DIFF_GROUP)
        xf, xb = _merge(o_fox, o_sb, o_diff, w_out_b, l, xf, ln_g3, ln_b3, min(t, 512))
    return xf.reshape(batch, seq, d_model)
```

```python
import functools
import math

import jax
import jax.numpy as jnp
from jax import lax
from jax.experimental import pallas as pl
from jax.experimental.pallas import tpu as pltpu

HEAD_DIM = 128
FOX_HEADS = 6
SB_HEADS = 6
DIFF_HEADS = 4
DIFF_QK_DIM = HEAD_DIM // 2
FOX_W = FOX_HEADS * HEAD_DIM
SB_W = SB_HEADS * HEAD_DIM
DIFF_W = DIFF_HEADS * HEAD_DIM
DEPTH_FOR_DEEPNORM = 4
DEEPNORM_ALPHA = (2 * DEPTH_FOR_DEEPNORM) ** 0.25
LN_EPS = 1e-5
SUBLN_EPS = 1e-5
NEG_BIG = -1e30
LOG2E = math.log2(math.e)

LANES = 128
VMEM_LIMIT_BYTES = 56 * 1024 * 1024

FOX_Q0, FOX_K0 = 0, FOX_HEADS
SB_Q0, SB_K0 = 2 * FOX_HEADS, 2 * FOX_HEADS + SB_HEADS
DIFF_Q0, DIFF_K0 = 2 * (FOX_HEADS + SB_HEADS), 2 * (FOX_HEADS + SB_HEADS) + DIFF_HEADS
FOX_V0, SB_V0, DIFF_V0 = 0, FOX_HEADS, FOX_HEADS + SB_HEADS
GATE_FOX0, GATE_SB0, GATE_DIFF0 = 0, FOX_HEADS, FOX_HEADS + SB_HEADS
GATE_FF = FOX_HEADS + SB_HEADS + DIFF_HEADS
GATE_BLOCKS = GATE_FF + 1
FOX_GROUP, SB_GROUP, DIFF_GROUP = 3, 3, 2
ATTN_TQ = 2048
KEY_BLOCK = 256
SB_SUB = 128


def _params(*semantics):
    return pltpu.CompilerParams(dimension_semantics=semantics, vmem_limit_bytes=VMEM_LIMIT_BYTES)


def _proj_kernel(x_ref, w_ref, *rest, transpose_out):
    o_ref = rest[-1]
    acc = jnp.dot(x_ref[...].astype(w_ref.dtype), w_ref[...], preferred_element_type=jnp.float32)
    if len(rest) == 2:
        acc = acc * rest[0][...]
    for c in range(o_ref.shape[0]):
        blk = acc[:, c * LANES:(c + 1) * LANES]
        o_ref[c] = (blk.T if transpose_out else blk).astype(o_ref.dtype)


def _gate_proj_kernel(x_ref, w_ref, wf_ref, o_ref):
    x = x_ref[...].astype(w_ref.dtype)
    acc = jnp.dot(x, w_ref[...], preferred_element_type=jnp.float32)
    for c in range(GATE_FF):
        o_ref[c] = acc[:, c * LANES:(c + 1) * LANES]
    o_ref[GATE_FF] = jnp.dot(x, wf_ref[...], preferred_element_type=jnp.float32)


def _project_gates(x, w, w_ff, layer, col0, tm):
    t, k = x.shape
    return pl.pallas_call(
        _gate_proj_kernel,
        grid=(t // tm,),
        in_specs=[pl.BlockSpec((tm, k), lambda i: (i, 0)),
                  pl.BlockSpec((None, k, GATE_COLS), lambda i: (layer, 0, col0 // GATE_COLS)),
                  pl.BlockSpec((None, k, LANES), lambda i: (layer, 0, 0))],
        out_specs=pl.BlockSpec((GATE_BLOCKS, tm, LANES), lambda i: (0, i, 0)),
        out_shape=jax.ShapeDtypeStruct((GATE_BLOCKS, t, LANES), jnp.float32),
        compiler_params=_params("parallel"),
        name="gate_projection",
    )(x, w, w_ff)


def _project(x, w, layer, col0, n, col_scale, out_dtype, tm, tn, name, transpose_out=False):
    t, k = x.shape
    if transpose_out:
        out_spec = pl.BlockSpec((tn // LANES, LANES, tm), lambda i, j: (j, 0, i))
        out_shape = (n // LANES, LANES, t)
    else:
        out_spec = pl.BlockSpec((tn // LANES, tm, LANES), lambda i, j: (j, i, 0))
        out_shape = (n // LANES, t, LANES)
    in_specs = [pl.BlockSpec((tm, k), lambda i, j: (i, 0)),
                pl.BlockSpec((None, k, tn), lambda i, j: (layer, 0, col0 // tn + j))]
    operands = [x, w]
    if col_scale is not None:
        in_specs.append(pl.BlockSpec((1, tn), lambda i, j: (0, j)))
        operands.append(col_scale)
    return pl.pallas_call(
        functools.partial(_proj_kernel, transpose_out=transpose_out),
        grid=(t // tm, n // tn),
        in_specs=in_specs,
        out_specs=out_spec,
        out_shape=jax.ShapeDtypeStruct(out_shape, out_dtype),
        compiler_params=_params("parallel", "parallel"),
        name=name,
    )(*operands)


def _log_sigmoid(z):
    return jnp.minimum(z, 0.0) - jnp.log1p(jnp.exp(-jnp.abs(z)))


def _split3(x):
    x1 = x.astype(jnp.bfloat16)
    r = x - x1.astype(jnp.float32)
    x2 = r.astype(jnp.bfloat16)
    x3 = (r - x2.astype(jnp.float32)).astype(jnp.bfloat16)
    return x1, x2, x3


def _decay_kernel(ff_ref, bf_ref, tri_ref, c_ref, *, chunk):
    s = ff_ref.shape[1]
    tri = tri_ref[...]
    carry = jnp.zeros((1, LANES), jnp.float32)
    for c in range(s // chunk):
        lf = _log_sigmoid(ff_ref[0, c * chunk:(c + 1) * chunk, :] + bf_ref[...])
        x1, x2, x3 = _split3(lf)
        cs = (jnp.dot(tri, x1, preferred_element_type=jnp.float32)
              + jnp.dot(tri, x2, preferred_element_type=jnp.float32)
              + jnp.dot(tri, x3, preferred_element_type=jnp.float32)) + carry
        c_ref[0, c * chunk:(c + 1) * chunk, :] = cs
        carry = cs[chunk - 1:chunk, :]


def _forget_prefix(gates, b_f_row, batch, seq, chunk):
    tri = (lax.broadcasted_iota(jnp.int32, (chunk, chunk), 0)
           >= lax.broadcasted_iota(jnp.int32, (chunk, chunk), 1)).astype(jnp.bfloat16)
    return pl.pallas_call(
        functools.partial(_decay_kernel, chunk=chunk),
        grid=(batch,),
        in_specs=[
            pl.BlockSpec((1, seq, LANES), lambda b: (GATE_FF, b, 0)),
            pl.BlockSpec((1, LANES), lambda b: (0, 0)),
            pl.BlockSpec((chunk, chunk), lambda b: (0, 0)),
        ],
        out_specs=pl.BlockSpec((1, seq, LANES), lambda b: (b, 0, 0)),
        out_shape=jax.ShapeDtypeStruct((batch, seq, LANES), jnp.float32),
        compiler_params=_params("parallel"),
        name="forget_prefix",
    )(gates, b_f_row, tri)


def _kq(k, q):
    return lax.dot_general(k, q, (((1,), (1,)), ((), ())), preferred_element_type=jnp.float32)


def _silu(g):
    return g * (1.0 / (1.0 + jnp.exp(-g)))


def _lane_tiles(a, tile):
    n = a.shape[1] // LANES
    return jnp.concatenate([tile(a[:, i * LANES:(i + 1) * LANES]) for i in range(n)], axis=1)


def _gather_lanes(x, ranges):
    if ranges is None:
        return x
    return jnp.concatenate([x[:, a:b] for a, b in ranges], axis=1)


def _scatter_lanes(x, new, ranges):
    if ranges is None:
        return new
    pieces, pos, off = [], 0, 0
    for a, b in ranges:
        if a > pos:
            pieces.append(x[:, pos:a])
        pieces.append(new[:, off:off + b - a])
        off += b - a
        pos = b
    if pos < x.shape[1]:
        pieces.append(x[:, pos:])
    return jnp.concatenate(pieces, axis=1)


ONES_ROWS = 16


def _with_ones_rows(vt):
    ones = (lax.broadcasted_iota(jnp.int32, (ONES_ROWS, vt.shape[1]), 0) == 0).astype(vt.dtype)
    return jnp.concatenate([vt, ones], axis=0)


def _softmax_steps(sts, vts, carry, ranges=None, mask=None):
    stats = []
    for st, (m, _) in zip(sts, carry):
        if mask is not None:
            st = jnp.where(mask, st, NEG_BIG)
        m_old = _gather_lanes(m, ranges)
        m_new = jnp.maximum(m_old, jnp.max(st, axis=0, keepdims=True))
        stats.append((m_new, jnp.exp2(m_old - m_new), jnp.exp2(st - m_new)))
    out = []
    for vt, (m, acc), (m_new, alpha, p) in zip(vts, carry, stats):
        acc_new = alpha * _gather_lanes(acc, ranges) + jnp.dot(
            vt, p.astype(vt.dtype), preferred_element_type=jnp.float32)
        out.append((_scatter_lanes(m, m_new, ranges), _scatter_lanes(acc, acc_new, ranges)))
    return tuple(out)


def _normalised(acc):
    return acc[0:HEAD_DIM] * (1.0 / acc[HEAD_DIM:HEAD_DIM + 1])


def _attn_specs(q0, k0, v0, g0, seq, heads):
    blk = lambda base: pl.BlockSpec((heads, seq, LANES), lambda b, hg: (base // heads + hg, b, 0))
    vt = pl.BlockSpec((heads, LANES, seq), lambda b, hg: (v0 // heads + hg, 0, b))
    return blk(q0), blk(k0), vt, blk(g0)


def _causal(tk, width, strict=False):
    rows = lax.broadcasted_iota(jnp.int32, (tk, width), 0)
    cols = lax.broadcasted_iota(jnp.int32, (tk, width), 1)
    return cols > rows if strict else cols >= rows


def _fox_kernel(q_ref, k_ref, vt_ref, g_ref, c_ref, o_ref, bias_ref, vta_ref, *, tq, tk):
    heads, seq = q_ref.shape[0], q_ref.shape[1]
    ratio = tq // tk
    lane = lax.broadcasted_iota(jnp.int32, (seq, LANES), 1)
    for g in range(heads):
        h = pl.program_id(1) * heads + g
        col = jnp.sum(jnp.where(lane == h, c_ref[0], 0.0), axis=1, keepdims=True)
        bias_ref[g] = jnp.broadcast_to(col * (-LOG2E), (seq, LANES))
        vta_ref[g] = _with_ones_rows(vt_ref[g])

    for qi in range(seq // tq):
        r0 = qi * tq
        qs = [q_ref[g, r0:r0 + tq, :] for g in range(heads)]

        def scores(c0, lane0):
            return [_lane_tiles(_kq(k_ref[g, pl.ds(c0, tk), :], qs[g][lane0:]),
                                lambda a, g=g: a + bias_ref[g, pl.ds(c0, tk), :])
                    for g in range(heads)]

        def values(c0):
            return [vta_ref[g, :, pl.ds(c0, tk)] for g in range(heads)]

        def body(j, carry):
            c0 = pl.multiple_of(j * tk, tk)
            return _softmax_steps(scores(c0, 0), values(c0), carry)

        init = (jnp.full((1, tq), NEG_BIG, jnp.float32),
                jnp.zeros((HEAD_DIM + ONES_ROWS, tq), jnp.float32))
        carry = lax.fori_loop(0, qi * ratio, body, (init,) * heads)
        for d in range(ratio):
            ranges = [(d * tk, tq)] if d else None
            carry = _softmax_steps(scores(r0 + d * tk, d * tk), values(r0 + d * tk), carry, ranges,
                                   _causal(tk, tq - d * tk))
        for g in range(heads):
            out = _normalised(carry[g][1]).T * _silu(g_ref[g, r0:r0 + tq, :])
            o_ref[r0:r0 + tq, g * LANES:(g + 1) * LANES] = out.astype(o_ref.dtype)


def _fox_attention(qk, vt, gates, c, batch, seq, tq, tk, heads):
    q_s, k_s, v_s, g_s = _attn_specs(FOX_Q0, FOX_K0, FOX_V0, GATE_FOX0, seq, heads)
    return pl.pallas_call(
        functools.partial(_fox_kernel, tq=tq, tk=tk),
        grid=(batch, FOX_HEADS // heads),
        in_specs=[q_s, k_s, v_s, g_s, pl.BlockSpec((1, seq, LANES), lambda b, hg: (b, 0, 0))],
        out_specs=pl.BlockSpec((seq, heads * LANES), lambda b, hg: (b, hg)),
        out_shape=jax.ShapeDtypeStruct((batch * seq, FOX_W), jnp.bfloat16),
        scratch_shapes=[pltpu.VMEM((heads, seq, LANES), jnp.float32),
                        pltpu.VMEM((heads, HEAD_DIM + ONES_ROWS, seq), jnp.bfloat16)],
        compiler_params=_params("parallel", "parallel"),
        name="fox_attention",
    )(qk, qk, vt, gates, c)


def _sb_blocks(qs, ks, vts, ut2, states, ranges, strict):
    tk = ks[0].shape[0]
    zs = [_kq(k, q) for k, q in zip(ks, qs)]
    parts = []
    for z in zs:
        neg_abs = lax.bitcast_convert_type(
            lax.bitcast_convert_type(z, jnp.uint32) | jnp.uint32(0x80000000), jnp.float32)
        sp = jnp.log2(1.0 + jnp.exp2(neg_abs))
        log_beta = jnp.minimum(z, 0.0) - sp
        log_keep = log_beta - z
        if strict is not None:
            log_keep = jnp.where(strict, log_keep, 0.0)
        hi = log_keep.astype(jnp.bfloat16)
        lo = (log_keep - hi.astype(jnp.float32)).astype(jnp.bfloat16)
        parts.append((log_beta, log_keep, hi, lo))
    subs = []
    for _, _, hi, lo in parts:
        subs.append([
            jnp.dot(ut2, jnp.concatenate([hi[i:i + SB_SUB], lo[i:i + SB_SUB]], axis=0),
                    preferred_element_type=jnp.float32)
            for i in range(0, tk, SB_SUB)])
    ws, carries = [], []
    for (log_beta, log_keep, _, _), sub, (carry_all, _) in zip(parts, subs, states):
        carry = _gather_lanes(carry_all, ranges)
        laters = [None] * len(sub)
        for n in reversed(range(len(sub))):
            laters[n] = sub[n] + carry
            carry = carry + sub[n][0:1] + log_keep[n * SB_SUB:n * SB_SUB + 1]
        w = jnp.exp2(log_beta + jnp.concatenate(laters, axis=0))
        if strict is not None:
            w = jnp.where(strict, w, 0.0)
        ws.append(w)
        carries.append(_scatter_lanes(carry_all, carry, ranges))
    out = []
    for w, vt, carry, (_, acc) in zip(ws, vts, carries, states):
        acc_new = _gather_lanes(acc, ranges) + jnp.dot(vt, w.astype(vt.dtype),
                                                       preferred_element_type=jnp.float32)
        out.append((carry, _scatter_lanes(acc, acc_new, ranges)))
    return tuple(out)


def _sb_kernel(q_ref, k_ref, vt_ref, g_ref, ut_ref, o_ref, *, tq, tk):
    heads, seq = q_ref.shape[0], q_ref.shape[1]
    ratio = tq // tk
    ut2 = ut_ref[...]

    for qi in range(seq // tq):
        r0 = qi * tq
        qs = [q_ref[g, r0:r0 + tq, :] for g in range(heads)]

        def block(c0, lane0, state, ranges, mask):
            return _sb_blocks([q[lane0:] for q in qs],
                              [k_ref[g, pl.ds(c0, tk), :] for g in range(heads)],
                              [vt_ref[g, :, pl.ds(c0, tk)] for g in range(heads)], ut2, state, ranges, mask)

        init = (jnp.zeros((1, tq), jnp.float32), jnp.zeros((HEAD_DIM, tq), jnp.float32))
        state = (init,) * heads
        for d in reversed(range(ratio)):
            state = block(r0 + d * tk, d * tk, state, [(d * tk, tq)] if d else None,
                          _causal(tk, tq - d * tk, strict=True))
        n_full = qi * ratio
        state = lax.fori_loop(
            0, n_full,
            lambda i, st: block(pl.multiple_of((n_full - 1 - i) * tk, tk), 0, st, None, None), state)
        for g in range(heads):
            out = state[g][1].T * _silu(g_ref[g, r0:r0 + tq, :])
            o_ref[r0:r0 + tq, g * LANES:(g + 1) * LANES] = out.astype(o_ref.dtype)


def _sb_attention(qk, vt, gates, batch, seq, tq, tk, heads):
    q_s, k_s, v_s, g_s = _attn_specs(SB_Q0, SB_K0, SB_V0, GATE_SB0, seq, heads)
    upper = (lax.broadcasted_iota(jnp.int32, (SB_SUB, SB_SUB), 1)
             > lax.broadcasted_iota(jnp.int32, (SB_SUB, SB_SUB), 0)).astype(jnp.bfloat16)
    ut2 = jnp.concatenate([upper, upper], axis=1)
    return pl.pallas_call(
        functools.partial(_sb_kernel, tq=tq, tk=tk),
        grid=(batch, SB_HEADS // heads),
        in_specs=[q_s, k_s, v_s, g_s, pl.BlockSpec((SB_SUB, 2 * SB_SUB), lambda b, hg: (0, 0))],
        out_specs=pl.BlockSpec((seq, heads * LANES), lambda b, hg: (b, hg)),
        out_shape=jax.ShapeDtypeStruct((batch * seq, SB_W), jnp.bfloat16),
        compiler_params=_params("parallel", "parallel"),
        name="sb_attention",
    )(qk, qk, vt, gates, ut2)


def _diff_kernel(lam_ref, gsub_ref, q_ref, k_ref, vt_ref, g_ref, o_ref, vta_ref, *, tq, tk, lam_init):
    heads, seq = q_ref.shape[0], q_ref.shape[1]
    ratio = tq // tk
    lp = lam_ref[...]
    lam = (jnp.exp(jnp.sum(lp[0:1] * lp[1:2], axis=1, keepdims=True))
           - jnp.exp(jnp.sum(lp[2:3] * lp[3:4], axis=1, keepdims=True)) + lam_init)
    key = lax.broadcasted_iota(jnp.int32, (tk, LANES), 0).astype(jnp.float32)
    ramps = []
    for g in range(heads):
        h = pl.program_id(1) * heads + g
        expo = jnp.full((tk, LANES), 127 - (8 // DIFF_HEADS) * (h + 1), jnp.int32)
        slope = lax.bitcast_convert_type(expo << 23, jnp.float32) * LOG2E
        ramps.append((slope, slope * key))
        vta_ref[g] = _with_ones_rows(vt_ref[g])
    lane = lax.broadcasted_iota(jnp.int32, (tq, HEAD_DIM), 1)

    for qi in range(seq // tq):
        r0 = qi * tq
        q1s, q2s = [], []
        for g in range(heads):
            q = q_ref[g, r0:r0 + tq, :]
            zero = jnp.zeros_like(q)
            q1s.append(jnp.where(lane < DIFF_QK_DIM, q, zero))
            q2s.append(jnp.where(lane >= DIFF_QK_DIM, q, zero))

        def scores(c0, lane0):
            out = []
            for g in range(heads):
                slope, ramp = ramps[g]
                bias = ramp + slope * jnp.asarray(c0 - r0, jnp.float32)
                qq = jnp.concatenate([q1s[g][lane0:], q2s[g][lane0:]], axis=0)
                out.append(_lane_tiles(_kq(k_ref[g, pl.ds(c0, tk), :], qq), lambda a, b=bias: a + b))
            return out

        def values(c0):
            return [vta_ref[g, :, pl.ds(c0, tk)] for g in range(heads)]

        def body(j, carry):
            c0 = pl.multiple_of(j * tk, tk)
            return _softmax_steps(scores(c0, 0), values(c0), carry)

        init = (jnp.full((1, 2 * tq), NEG_BIG, jnp.float32),
                jnp.zeros((HEAD_DIM + ONES_ROWS, 2 * tq), jnp.float32))
        carry = lax.fori_loop(0, qi * ratio, body, (init,) * heads)
        for d in range(ratio):
            ranges = [(d * tk, tq), (tq + d * tk, 2 * tq)] if d else None
            half = _causal(tk, tq - d * tk)
            carry = _softmax_steps(scores(r0 + d * tk, d * tk), values(r0 + d * tk), carry, ranges,
                                   jnp.concatenate([half, half], axis=1))
        for g in range(heads):
            o = _normalised(carry[g][1])
            o = o[:, 0:tq] - lam * o[:, tq:2 * tq]
            o = o * lax.rsqrt(jnp.mean(o * o, axis=0, keepdims=True) + SUBLN_EPS)
            o = o.T * gsub_ref[...] * (1.0 - lam_init)
            out = o * _silu(g_ref[g, r0:r0 + tq, :])
            o_ref[r0:r0 + tq, g * LANES:(g + 1) * LANES] = out.astype(o_ref.dtype)


def _diff_attention(qk, vt, gates, lam_p, subln_row, batch, seq, tq, tk, lam_init, heads):
    q_s, k_s, v_s, g_s = _attn_specs(DIFF_Q0, DIFF_K0, DIFF_V0, GATE_DIFF0, seq, heads)
    return pl.pallas_call(
        functools.partial(_diff_kernel, tq=tq, tk=tk, lam_init=lam_init),
        grid=(batch, DIFF_HEADS // heads),
        in_specs=[pl.BlockSpec(lam_p.shape, lambda b, hg: (0, 0)),
                  pl.BlockSpec((1, HEAD_DIM), lambda b, hg: (0, 0)),
                  q_s, k_s, v_s, g_s],
        out_specs=pl.BlockSpec((seq, heads * LANES), lambda b, hg: (b, hg)),
        out_shape=jax.ShapeDtypeStruct((batch * seq, DIFF_W), jnp.bfloat16),
        scratch_shapes=[pltpu.VMEM((heads, HEAD_DIM + ONES_ROWS, seq), jnp.bfloat16)],
        compiler_params=_params("parallel", "parallel"),
        name="diff_attention",
    )(lam_p, subln_row, qk, qk, vt, gates)


def _merge_kernel(of_ref, os_ref, od_ref, wf_ref, ws_ref, wd_ref, x_ref, g_ref, b_ref, y_ref, yb_ref):
    y = (jnp.dot(of_ref[...], wf_ref[...], preferred_element_type=jnp.float32)
         + jnp.dot(os_ref[...], ws_ref[...], preferred_element_type=jnp.float32)
         + jnp.dot(od_ref[...], wd_ref[...], preferred_element_type=jnp.float32))
    z = DEEPNORM_ALPHA * x_ref[...] + y
    mu = jnp.mean(z, axis=1, keepdims=True)
    zc = z - mu
    var = jnp.mean(zc * zc, axis=1, keepdims=True)
    out = zc * lax.rsqrt(var + LN_EPS) * g_ref[...] + b_ref[...]
    y_ref[...] = out
    yb_ref[...] = out.astype(yb_ref.dtype)


def _merge(o_fox, o_sb, o_diff, w_out, layer, x, ln_g, ln_b, tm):
    t, d = x.shape
    row = lambda w: pl.BlockSpec((tm, w), lambda i: (i, 0))
    w_rows = lambda rows, start: pl.BlockSpec((None, rows, d), lambda i: (layer, start // rows, 0))
    vec = lambda: pl.BlockSpec((None, 1, d), lambda i: (layer, 0, 0))
    return pl.pallas_call(
        _merge_kernel,
        grid=(t // tm,),
        in_specs=[row(FOX_W), row(SB_W), row(DIFF_W),
                  w_rows(FOX_W, 0), w_rows(SB_W, FOX_W), w_rows(DIFF_W, FOX_W + SB_W),
                  row(d), vec(), vec()],
        out_specs=[row(d), row(d)],
        out_shape=[jax.ShapeDtypeStruct((t, d), jnp.float32),
                   jax.ShapeDtypeStruct((t, d), jnp.bfloat16)],
        compiler_params=_params("parallel"),
        name="merge_layernorm",
    )(o_fox, o_sb, o_diff, w_out, w_out, w_out, x, ln_g, ln_b)


PREP_UNIT = 256
QK_COLS = 2 * (FOX_W + SB_W + DIFF_W)
V_COLS = FOX_W + SB_W + DIFF_W
GATE_COLS = FOX_W + SB_W + DIFF_W


def _cast_kernel(w_ref, o_ref):
    o_ref[...] = w_ref[...].astype(o_ref.dtype)


def _permuted_in_weights(w_in):
    depth, k, _ = w_in.shape
    sizes = (FOX_W,) * 4 + (SB_W,) * 4 + (DIFF_W,) * 4
    starts = [sum(sizes[:i]) for i in range(len(sizes))]
    fq, fk, fv, fg, sq, sk, sv, sg, dq, dk, dv, dg = range(12)
    order = [fq, fk, sq, sk, dq, dk, fv, sv, dv, fg, sg, dg]
    steps, dst, shift = [], 0, 0
    for grp in order:
        new_shift = (starts[grp] - dst) // PREP_UNIT
        steps.append((dst // PREP_UNIT, new_shift - shift))
        shift = new_shift
        dst += sizes[grp]
    n_units = dst // PREP_UNIT

    def src_unit(j):
        return j + sum(jnp.where(j >= first, delta, 0) for first, delta in steps)

    return pl.pallas_call(
        _cast_kernel,
        grid=(depth, n_units),
        in_specs=[pl.BlockSpec((None, k, PREP_UNIT), lambda l, j: (l, 0, src_unit(j)))],
        out_specs=pl.BlockSpec((None, k, PREP_UNIT), lambda l, j: (l, 0, j)),
        out_shape=jax.ShapeDtypeStruct((depth, k, n_units * PREP_UNIT), jnp.bfloat16),
        compiler_params=_params("parallel", "parallel"),
        name="permute_in_weights",
    )(w_in)


def _forget_in_weights(w_in):
    ff = w_in[:, :, QK_COLS + V_COLS + GATE_COLS:]
    return jnp.pad(ff, ((0, 0), (0, 0), (0, LANES - FOX_HEADS))).astype(jnp.bfloat16)


def _qk_col_scale():
    one = lambda n: jnp.ones((n,), jnp.float32)
    full = lambda n, v: jnp.full((n,), v, jnp.float32)
    return jnp.concatenate([
        full(FOX_W, HEAD_DIM ** -0.5 * LOG2E), one(FOX_W),
        full(SB_W, HEAD_DIM ** -0.5 * LOG2E), one(SB_W),
        full(DIFF_W, DIFF_QK_DIM ** -0.5 * LOG2E), one(DIFF_W)])[None, :]


def kernel(x, w_in, b_f, diff_lambda, diff_subln_g, w_out, ln_g, ln_b):
    batch, seq, d_model = x.shape
    depth = w_in.shape[0]
    t = batch * seq
    tk = min(seq, KEY_BLOCK)
    tq = min(seq, ATTN_TQ)
    tm = min(t, 1024)

    w_perm = _permuted_in_weights(w_in)
    w_ff = _forget_in_weights(w_in)
    w_out_b = w_out.astype(jnp.bfloat16)
    qk_scale = _qk_col_scale()
    b_f_rows = jnp.pad(b_f, ((0, 0), (0, LANES - FOX_HEADS)))
    ln_g3 = ln_g.reshape(depth, 1, d_model)
    ln_b3 = ln_b.reshape(depth, 1, d_model)

    xf = x.reshape(t, d_model)
    xb = xf
    for l in range(depth):
        lam_init = 0.8 - 0.6 * math.exp(-0.3 * l)
        qk = _project(xb, w_perm, l, 0, QK_COLS, qk_scale, jnp.bfloat16, tm, 1024, "qk_projection")
        vt = _project(xb, w_perm, l, QK_COLS, V_COLS, None, jnp.bfloat16, tm, 1024, "v_projection",
                      transpose_out=True)
        gates = _project_gates(xb, w_perm, w_ff, l, QK_COLS + V_COLS, min(t, 512))
        c = _forget_prefix(gates, b_f_rows[l:l + 1], batch, seq, tk)
        o_fox = _fox_attention(qk, vt, gates, c, batch, seq, tq, tk, FOX_GROUP)
        o_sb = _sb_attention(qk, vt, gates, batch, seq, tq, tk, SB_GROUP)
        o_diff = _diff_attention(qk, vt, gates, diff_lambda[l], diff_subln_g[l:l + 1], batch, seq, tq, tk,
                                 lam_init, DIFF_GROUP)
        xf, xb = _merge(o_fox, o_sb, o_diff, w_out_b, l, xf, ln_g3, ln_b3, min(t, 512))
    return xf.reshape(batch, seq, d_model)
```

```python
import functools
import math

import jax
import jax.numpy as jnp
from jax import lax
from jax.experimental import pallas as pl
from jax.experimental.pallas import tpu as pltpu

HEAD_DIM = 128
FOX_HEADS = 6
SB_HEADS = 6
DIFF_HEADS = 4
DIFF_QK_DIM = HEAD_DIM // 2
FOX_W = FOX_HEADS * HEAD_DIM
SB_W = SB_HEADS * HEAD_DIM
DIFF_W = DIFF_HEADS * HEAD_DIM
DEPTH_FOR_DEEPNORM = 4
DEEPNORM_ALPHA = (2 * DEPTH_FOR_DEEPNORM) ** 0.25
LN_EPS = 1e-5
SUBLN_EPS = 1e-5
NEG_BIG = -1e30
LOG2E = math.log2(math.e)

LANES = 128
VMEM_LIMIT_BYTES = 56 * 1024 * 1024

FOX_Q0, FOX_K0 = 0, FOX_HEADS
SB_Q0, SB_K0 = 2 * FOX_HEADS, 2 * FOX_HEADS + SB_HEADS
DIFF_Q0, DIFF_K0 = 2 * (FOX_HEADS + SB_HEADS), 2 * (FOX_HEADS + SB_HEADS) + DIFF_HEADS
FOX_V0, SB_V0, DIFF_V0 = 0, FOX_HEADS, FOX_HEADS + SB_HEADS
GATE_FOX0, GATE_SB0, GATE_DIFF0 = 0, FOX_HEADS, FOX_HEADS + SB_HEADS
GATE_FF = FOX_HEADS + SB_HEADS + DIFF_HEADS
GATE_BLOCKS = GATE_FF + 1
FOX_GROUP, SB_GROUP, DIFF_GROUP = 3, 3, 2
ATTN_TQ = 2048
KEY_BLOCK = 256
SB_SUB = 256


def _params(*semantics):
    return pltpu.CompilerParams(dimension_semantics=semantics, vmem_limit_bytes=VMEM_LIMIT_BYTES)


def _proj_kernel(x_ref, w_ref, *rest, transpose_out):
    o_ref = rest[-1]
    acc = jnp.dot(x_ref[...].astype(w_ref.dtype), w_ref[...], preferred_element_type=jnp.float32)
    if len(rest) == 2:
        acc = acc * rest[0][...]
    for c in range(o_ref.shape[0]):
        blk = acc[:, c * LANES:(c + 1) * LANES]
        o_ref[c] = (blk.T if transpose_out else blk).astype(o_ref.dtype)


def _gate_proj_kernel(x_ref, w_ref, wf_ref, o_ref):
    x = x_ref[...].astype(w_ref.dtype)
    acc = jnp.dot(x, w_ref[...], preferred_element_type=jnp.float32)
    for c in range(GATE_FF):
        o_ref[c] = acc[:, c * LANES:(c + 1) * LANES]
    o_ref[GATE_FF] = jnp.dot(x, wf_ref[...], preferred_element_type=jnp.float32)


def _project_gates(x, w, w_ff, layer, col0, tm):
    t, k = x.shape
    return pl.pallas_call(
        _gate_proj_kernel,
        grid=(t // tm,),
        in_specs=[pl.BlockSpec((tm, k), lambda i: (i, 0)),
                  pl.BlockSpec((None, k, GATE_COLS), lambda i: (layer, 0, col0 // GATE_COLS)),
                  pl.BlockSpec((None, k, LANES), lambda i: (layer, 0, 0))],
        out_specs=pl.BlockSpec((GATE_BLOCKS, tm, LANES), lambda i: (0, i, 0)),
        out_shape=jax.ShapeDtypeStruct((GATE_BLOCKS, t, LANES), jnp.float32),
        compiler_params=_params("parallel"),
        name="gate_projection",
    )(x, w, w_ff)


def _project(x, w, layer, col0, n, col_scale, out_dtype, tm, tn, name, transpose_out=False):
    t, k = x.shape
    if transpose_out:
        out_spec = pl.BlockSpec((tn // LANES, LANES, tm), lambda i, j: (j, 0, i))
        out_shape = (n // LANES, LANES, t)
    else:
        out_spec = pl.BlockSpec((tn // LANES, tm, LANES), lambda i, j: (j, i, 0))
        out_shape = (n // LANES, t, LANES)
    in_specs = [pl.BlockSpec((tm, k), lambda i, j: (i, 0)),
                pl.BlockSpec((None, k, tn), lambda i, j: (layer, 0, col0 // tn + j))]
    operands = [x, w]
    if col_scale is not None:
        in_specs.append(pl.BlockSpec((1, tn), lambda i, j: (0, j)))
        operands.append(col_scale)
    return pl.pallas_call(
        functools.partial(_proj_kernel, transpose_out=transpose_out),
        grid=(t // tm, n // tn),
        in_specs=in_specs,
        out_specs=out_spec,
        out_shape=jax.ShapeDtypeStruct(out_shape, out_dtype),
        compiler_params=_params("parallel", "parallel"),
        name=name,
    )(*operands)


def _log_sigmoid(z):
    return jnp.minimum(z, 0.0) - jnp.log1p(jnp.exp(-jnp.abs(z)))


def _split3(x):
    x1 = x.astype(jnp.bfloat16)
    r = x - x1.astype(jnp.float32)
    x2 = r.astype(jnp.bfloat16)
    x3 = (r - x2.astype(jnp.float32)).astype(jnp.bfloat16)
    return x1, x2, x3


def _decay_kernel(ff_ref, bf_ref, tri_ref, c_ref, *, chunk):
    s = ff_ref.shape[1]
    tri = tri_ref[...]
    carry = jnp.zeros((1, LANES), jnp.float32)
    for c in range(s // chunk):
        lf = _log_sigmoid(ff_ref[0, c * chunk:(c + 1) * chunk, :] + bf_ref[...])
        x1, x2, x3 = _split3(lf)
        cs = (jnp.dot(tri, x1, preferred_element_type=jnp.float32)
              + jnp.dot(tri, x2, preferred_element_type=jnp.float32)
              + jnp.dot(tri, x3, preferred_element_type=jnp.float32)) + carry
        c_ref[0, c * chunk:(c + 1) * chunk, :] = cs
        carry = cs[chunk - 1:chunk, :]


def _forget_prefix(gates, b_f_row, batch, seq, chunk):
    tri = (lax.broadcasted_iota(jnp.int32, (chunk, chunk), 0)
           >= lax.broadcasted_iota(jnp.int32, (chunk, chunk), 1)).astype(jnp.bfloat16)
    return pl.pallas_call(
        functools.partial(_decay_kernel, chunk=chunk),
        grid=(batch,),
        in_specs=[
            pl.BlockSpec((1, seq, LANES), lambda b: (GATE_FF, b, 0)),
            pl.BlockSpec((1, LANES), lambda b: (0, 0)),
            pl.BlockSpec((chunk, chunk), lambda b: (0, 0)),
        ],
        out_specs=pl.BlockSpec((1, seq, LANES), lambda b: (b, 0, 0)),
        out_shape=jax.ShapeDtypeStruct((batch, seq, LANES), jnp.float32),
        compiler_params=_params("parallel"),
        name="forget_prefix",
    )(gates, b_f_row, tri)


def _kq(k, q):
    return lax.dot_general(k, q, (((1,), (1,)), ((), ())), preferred_element_type=jnp.float32)


def _silu(g):
    return g * (1.0 / (1.0 + jnp.exp(-g)))


def _lane_tiles(a, tile):
    n = a.shape[1] // LANES
    return jnp.concatenate([tile(a[:, i * LANES:(i + 1) * LANES]) for i in range(n)], axis=1)


def _gather_lanes(x, ranges):
    if ranges is None:
        return x
    return jnp.concatenate([x[:, a:b] for a, b in ranges], axis=1)


def _scatter_lanes(x, new, ranges):
    if ranges is None:
        return new
    pieces, pos, off = [], 0, 0
    for a, b in ranges:
        if a > pos:
            pieces.append(x[:, pos:a])
        pieces.append(new[:, off:off + b - a])
        off += b - a
        pos = b
    if pos < x.shape[1]:
        pieces.append(x[:, pos:])
    return jnp.concatenate(pieces, axis=1)


ONES_ROWS = 16


def _with_ones_rows(vt):
    ones = (lax.broadcasted_iota(jnp.int32, (ONES_ROWS, vt.shape[1]), 0) == 0).astype(vt.dtype)
    return jnp.concatenate([vt, ones], axis=0)


def _softmax_steps(sts, vts, carry, ranges=None, mask=None):
    stats = []
    for st, (m, _) in zip(sts, carry):
        if mask is not None:
            st = jnp.where(mask, st, NEG_BIG)
        m_old = _gather_lanes(m, ranges)
        m_new = jnp.maximum(m_old, jnp.max(st, axis=0, keepdims=True))
        stats.append((m_new, jnp.exp2(m_old - m_new), jnp.exp2(st - m_new)))
    out = []
    for vt, (m, acc), (m_new, alpha, p) in zip(vts, carry, stats):
        acc_new = alpha * _gather_lanes(acc, ranges) + jnp.dot(
            vt, p.astype(vt.dtype), preferred_element_type=jnp.float32)
        out.append((_scatter_lanes(m, m_new, ranges), _scatter_lanes(acc, acc_new, ranges)))
    return tuple(out)


def _normalised(acc):
    return acc[0:HEAD_DIM] * (1.0 / acc[HEAD_DIM:HEAD_DIM + 1])


def _attn_specs(q0, k0, v0, g0, seq, heads):
    blk = lambda base: pl.BlockSpec((heads, seq, LANES), lambda b, hg: (base // heads + hg, b, 0))
    vt = pl.BlockSpec((heads, LANES, seq), lambda b, hg: (v0 // heads + hg, 0, b))
    return blk(q0), blk(k0), vt, blk(g0)


def _causal(tk, width, strict=False):
    rows = lax.broadcasted_iota(jnp.int32, (tk, width), 0)
    cols = lax.broadcasted_iota(jnp.int32, (tk, width), 1)
    return cols > rows if strict else cols >= rows


def _fox_kernel(q_ref, k_ref, vt_ref, g_ref, c_ref, o_ref, bias_ref, vta_ref, *, tq, tk):
    heads, seq = q_ref.shape[0], q_ref.shape[1]
    ratio = tq // tk
    lane = lax.broadcasted_iota(jnp.int32, (seq, LANES), 1)
    for g in range(heads):
        h = pl.program_id(1) * heads + g
        col = jnp.sum(jnp.where(lane == h, c_ref[0], 0.0), axis=1, keepdims=True)
        bias_ref[g] = jnp.broadcast_to(col * (-LOG2E), (seq, LANES))
        vta_ref[g] = _with_ones_rows(vt_ref[g])

    for qi in range(seq // tq):
        r0 = qi * tq
        qs = [q_ref[g, r0:r0 + tq, :] for g in range(heads)]

        def scores(c0, lane0):
            return [_lane_tiles(_kq(k_ref[g, pl.ds(c0, tk), :], qs[g][lane0:]),
                                lambda a, g=g: a + bias_ref[g, pl.ds(c0, tk), :])
                    for g in range(heads)]

        def values(c0):
            return [vta_ref[g, :, pl.ds(c0, tk)] for g in range(heads)]

        def body(j, carry):
            c0 = pl.multiple_of(j * tk, tk)
            return _softmax_steps(scores(c0, 0), values(c0), carry)

        init = (jnp.full((1, tq), NEG_BIG, jnp.float32),
                jnp.zeros((HEAD_DIM + ONES_ROWS, tq), jnp.float32))
        carry = lax.fori_loop(0, qi * ratio, body, (init,) * heads)
        for d in range(ratio):
            ranges = [(d * tk, tq)] if d else None
            carry = _softmax_steps(scores(r0 + d * tk, d * tk), values(r0 + d * tk), carry, ranges,
                                   _causal(tk, tq - d * tk))
        for g in range(heads):
            out = _normalised(carry[g][1]).T * _silu(g_ref[g, r0:r0 + tq, :])
            o_ref[r0:r0 + tq, g * LANES:(g + 1) * LANES] = out.astype(o_ref.dtype)


def _fox_attention(qk, vt, gates, c, batch, seq, tq, tk, heads):
    q_s, k_s, v_s, g_s = _attn_specs(FOX_Q0, FOX_K0, FOX_V0, GATE_FOX0, seq, heads)
    return pl.pallas_call(
        functools.partial(_fox_kernel, tq=tq, tk=tk),
        grid=(batch, FOX_HEADS // heads),
        in_specs=[q_s, k_s, v_s, g_s, pl.BlockSpec((1, seq, LANES), lambda b, hg: (b, 0, 0))],
        out_specs=pl.BlockSpec((seq, heads * LANES), lambda b, hg: (b, hg)),
        out_shape=jax.ShapeDtypeStruct((batch * seq, FOX_W), jnp.bfloat16),
        scratch_shapes=[pltpu.VMEM((heads, seq, LANES), jnp.float32),
                        pltpu.VMEM((heads, HEAD_DIM + ONES_ROWS, seq), jnp.bfloat16)],
        compiler_params=_params("parallel", "parallel"),
        name="fox_attention",
    )(qk, qk, vt, gates, c)


def _sb_blocks(qs, ks, vts, ut, states, ranges, strict):
    tk = ks[0].shape[0]
    sub_keys = ut.shape[0]
    zs = [_kq(k, q) for k, q in zip(ks, qs)]
    parts = []
    for z in zs:
        neg_abs = lax.bitcast_convert_type(
            lax.bitcast_convert_type(z, jnp.uint32) | jnp.uint32(0x80000000), jnp.float32)
        sp = jnp.log2(1.0 + jnp.exp2(neg_abs))
        log_beta = jnp.minimum(z, 0.0) - sp
        log_keep = log_beta - z
        if strict is not None:
            log_keep = jnp.where(strict, log_keep, 0.0)
        parts.append((log_beta, log_keep, log_keep.astype(jnp.bfloat16)))
    subs = []
    for _, _, terms in parts:
        subs.append([jnp.dot(ut, terms[i:i + sub_keys], preferred_element_type=jnp.float32)
                     for i in range(0, tk, sub_keys)])
    ws, carries = [], []
    for (log_beta, log_keep, _), sub, (carry_all, _) in zip(parts, subs, states):
        carry = _gather_lanes(carry_all, ranges)
        laters = [None] * len(sub)
        for n in reversed(range(len(sub))):
            laters[n] = sub[n] + carry
            carry = carry + sub[n][0:1] + log_keep[n * sub_keys:n * sub_keys + 1]
        w = jnp.exp2(log_beta + jnp.concatenate(laters, axis=0))
        if strict is not None:
            w = jnp.where(strict, w, 0.0)
        ws.append(w)
        carries.append(_scatter_lanes(carry_all, carry, ranges))
    out = []
    for w, vt, carry, (_, acc) in zip(ws, vts, carries, states):
        acc_new = _gather_lanes(acc, ranges) + jnp.dot(vt, w.astype(vt.dtype),
                                                       preferred_element_type=jnp.float32)
        out.append((carry, _scatter_lanes(acc, acc_new, ranges)))
    return tuple(out)


def _sb_kernel(q_ref, k_ref, vt_ref, g_ref, ut_ref, o_ref, *, tq, tk):
    heads, seq = q_ref.shape[0], q_ref.shape[1]
    ratio = tq // tk
    ut = ut_ref[...]

    for qi in range(seq // tq):
        r0 = qi * tq
        qs = [q_ref[g, r0:r0 + tq, :] for g in range(heads)]

        def block(c0, lane0, state, ranges, mask):
            return _sb_blocks([q[lane0:] for q in qs],
                              [k_ref[g, pl.ds(c0, tk), :] for g in range(heads)],
                              [vt_ref[g, :, pl.ds(c0, tk)] for g in range(heads)], ut, state, ranges, mask)

        init = (jnp.zeros((1, tq), jnp.float32), jnp.zeros((HEAD_DIM, tq), jnp.float32))
        state = (init,) * heads
        for d in reversed(range(ratio)):
            state = block(r0 + d * tk, d * tk, state, [(d * tk, tq)] if d else None,
                          _causal(tk, tq - d * tk, strict=True))
        n_full = qi * ratio
        state = lax.fori_loop(
            0, n_full,
            lambda i, st: block(pl.multiple_of((n_full - 1 - i) * tk, tk), 0, st, None, None), state)
        for g in range(heads):
            out = state[g][1].T * _silu(g_ref[g, r0:r0 + tq, :])
            o_ref[r0:r0 + tq, g * LANES:(g + 1) * LANES] = out.astype(o_ref.dtype)


def _sb_attention(qk, vt, gates, batch, seq, tq, tk, heads):
    q_s, k_s, v_s, g_s = _attn_specs(SB_Q0, SB_K0, SB_V0, GATE_SB0, seq, heads)
    sub = min(SB_SUB, tk)
    ut = (lax.broadcasted_iota(jnp.int32, (sub, sub), 1)
          > lax.broadcasted_iota(jnp.int32, (sub, sub), 0)).astype(jnp.bfloat16)
    return pl.pallas_call(
        functools.partial(_sb_kernel, tq=tq, tk=tk),
        grid=(batch, SB_HEADS // heads),
        in_specs=[q_s, k_s, v_s, g_s, pl.BlockSpec((sub, sub), lambda b, hg: (0, 0))],
        out_specs=pl.BlockSpec((seq, heads * LANES), lambda b, hg: (b, hg)),
        out_shape=jax.ShapeDtypeStruct((batch * seq, SB_W), jnp.bfloat16),
        compiler_params=_params("parallel", "parallel"),
        name="sb_attention",
    )(qk, qk, vt, gates, ut)


def _diff_kernel(lam_ref, gsub_ref, q_ref, k_ref, vt_ref, g_ref, o_ref, vta_ref, *, tq, tk, lam_init):
    heads, seq = q_ref.shape[0], q_ref.shape[1]
    ratio = tq // tk
    lp = lam_ref[...]
    lam = (jnp.exp(jnp.sum(lp[0:1] * lp[1:2], axis=1, keepdims=True))
           - jnp.exp(jnp.sum(lp[2:3] * lp[3:4], axis=1, keepdims=True)) + lam_init)
    key = lax.broadcasted_iota(jnp.int32, (tk, LANES), 0).astype(jnp.float32)
    ramps = []
    for g in range(heads):
        h = pl.program_id(1) * heads + g
        expo = jnp.full((tk, LANES), 127 - (8 // DIFF_HEADS) * (h + 1), jnp.int32)
        slope = lax.bitcast_convert_type(expo << 23, jnp.float32) * LOG2E
        ramps.append((slope, slope * key))
        vta_ref[g] = _with_ones_rows(vt_ref[g])
    lane = lax.broadcasted_iota(jnp.int32, (tq, HEAD_DIM), 1)

    for qi in range(seq // tq):
        r0 = qi * tq
        q1s, q2s = [], []
        for g in range(heads):
            q = q_ref[g, r0:r0 + tq, :]
            zero = jnp.zeros_like(q)
            q1s.append(jnp.where(lane < DIFF_QK_DIM, q, zero))
            q2s.append(jnp.where(lane >= DIFF_QK_DIM, q, zero))

        def scores(c0, lane0):
            out = []
            for g in range(heads):
                slope, ramp = ramps[g]
                bias = ramp + slope * jnp.asarray(c0 - r0, jnp.float32)
                qq = jnp.concatenate([q1s[g][lane0:], q2s[g][lane0:]], axis=0)
                out.append(_lane_tiles(_kq(k_ref[g, pl.ds(c0, tk), :], qq), lambda a, b=bias: a + b))
            return out

        def values(c0):
            return [vta_ref[g, :, pl.ds(c0, tk)] for g in range(heads)]

        def body(j, carry):
            c0 = pl.multiple_of(j * tk, tk)
            return _softmax_steps(scores(c0, 0), values(c0), carry)

        init = (jnp.full((1, 2 * tq), NEG_BIG, jnp.float32),
                jnp.zeros((HEAD_DIM + ONES_ROWS, 2 * tq), jnp.float32))
        carry = lax.fori_loop(0, qi * ratio, body, (init,) * heads)
        for d in range(ratio):
            ranges = [(d * tk, tq), (tq + d * tk, 2 * tq)] if d else None
            half = _causal(tk, tq - d * tk)
            carry = _softmax_steps(scores(r0 + d * tk, d * tk), values(r0 + d * tk), carry, ranges,
                                   jnp.concatenate([half, half], axis=1))
        for g in range(heads):
            o = _normalised(carry[g][1])
            o = o[:, 0:tq] - lam * o[:, tq:2 * tq]
            o = o * lax.rsqrt(jnp.mean(o * o, axis=0, keepdims=True) + SUBLN_EPS)
            o = o.T * gsub_ref[...] * (1.0 - lam_init)
            out = o * _silu(g_ref[g, r0:r0 + tq, :])
            o_ref[r0:r0 + tq, g * LANES:(g + 1) * LANES] = out.astype(o_ref.dtype)


def _diff_attention(qk, vt, gates, lam_p, subln_row, batch, seq, tq, tk, lam_init, heads):
    q_s, k_s, v_s, g_s = _attn_specs(DIFF_Q0, DIFF_K0, DIFF_V0, GATE_DIFF0, seq, heads)
    return pl.pallas_call(
        functools.partial(_diff_kernel, tq=tq, tk=tk, lam_init=lam_init),
        grid=(batch, DIFF_HEADS // heads),
        in_specs=[pl.BlockSpec(lam_p.shape, lambda b, hg: (0, 0)),
                  pl.BlockSpec((1, HEAD_DIM), lambda b, hg: (0, 0)),
                  q_s, k_s, v_s, g_s],
        out_specs=pl.BlockSpec((seq, heads * LANES), lambda b, hg: (b, hg)),
        out_shape=jax.ShapeDtypeStruct((batch * seq, DIFF_W), jnp.bfloat16),
        scratch_shapes=[pltpu.VMEM((heads, HEAD_DIM + ONES_ROWS, seq), jnp.bfloat16)],
        compiler_params=_params("parallel", "parallel"),
        name="diff_attention",
    )(lam_p, subln_row, qk, qk, vt, gates)


def _merge_kernel(of_ref, os_ref, od_ref, wf_ref, ws_ref, wd_ref, x_ref, g_ref, b_ref, y_ref, yb_ref):
    y = (jnp.dot(of_ref[...], wf_ref[...], preferred_element_type=jnp.float32)
         + jnp.dot(os_ref[...], ws_ref[...], preferred_element_type=jnp.float32)
         + jnp.dot(od_ref[...], wd_ref[...], preferred_element_type=jnp.float32))
    z = DEEPNORM_ALPHA * x_ref[...] + y
    mu = jnp.mean(z, axis=1, keepdims=True)
    zc = z - mu
    var = jnp.mean(zc * zc, axis=1, keepdims=True)
    out = zc * lax.rsqrt(var + LN_EPS) * g_ref[...] + b_ref[...]
    y_ref[...] = out
    yb_ref[...] = out.astype(yb_ref.dtype)


def _merge(o_fox, o_sb, o_diff, w_out, layer, x, ln_g, ln_b, tm):
    t, d = x.shape
    row = lambda w: pl.BlockSpec((tm, w), lambda i: (i, 0))
    w_rows = lambda rows, start: pl.BlockSpec((None, rows, d), lambda i: (layer, start // rows, 0))
    vec = lambda: pl.BlockSpec((None, 1, d), lambda i: (layer, 0, 0))
    return pl.pallas_call(
        _merge_kernel,
        grid=(t // tm,),
        in_specs=[row(FOX_W), row(SB_W), row(DIFF_W),
                  w_rows(FOX_W, 0), w_rows(SB_W, FOX_W), w_rows(DIFF_W, FOX_W + SB_W),
                  row(d), vec(), vec()],
        out_specs=[row(d), row(d)],
        out_shape=[jax.ShapeDtypeStruct((t, d), jnp.float32),
                   jax.ShapeDtypeStruct((t, d), jnp.bfloat16)],
        compiler_params=_params("parallel"),
        name="merge_layernorm",
    )(o_fox, o_sb, o_diff, w_out, w_out, w_out, x, ln_g, ln_b)


PREP_UNIT = 256
QK_COLS = 2 * (FOX_W + SB_W + DIFF_W)
V_COLS = FOX_W + SB_W + DIFF_W
GATE_COLS = FOX_W + SB_W + DIFF_W


def _cast_kernel(w_ref, o_ref):
    o_ref[...] = w_ref[...].astype(o_ref.dtype)


def _permuted_in_weights(w_in):
    depth, k, _ = w_in.shape
    sizes = (FOX_W,) * 4 + (SB_W,) * 4 + (DIFF_W,) * 4
    starts = [sum(sizes[:i]) for i in range(len(sizes))]
    fq, fk, fv, fg, sq, sk, sv, sg, dq, dk, dv, dg = range(12)
    order = [fq, fk, sq, sk, dq, dk, fv, sv, dv, fg, sg, dg]
    steps, dst, shift = [], 0, 0
    for grp in order:
        new_shift = (starts[grp] - dst) // PREP_UNIT
        steps.append((dst // PREP_UNIT, new_shift - shift))
        shift = new_shift
        dst += sizes[grp]
    n_units = dst // PREP_UNIT

    def src_unit(j):
        return j + sum(jnp.where(j >= first, delta, 0) for first, delta in steps)

    return pl.pallas_call(
        _cast_kernel,
        grid=(depth, n_units),
        in_specs=[pl.BlockSpec((None, k, PREP_UNIT), lambda l, j: (l, 0, src_unit(j)))],
        out_specs=pl.BlockSpec((None, k, PREP_UNIT), lambda l, j: (l, 0, j)),
        out_shape=jax.ShapeDtypeStruct((depth, k, n_units * PREP_UNIT), jnp.bfloat16),
        compiler_params=_params("parallel", "parallel"),
        name="permute_in_weights",
    )(w_in)


def _forget_in_weights(w_in):
    ff = w_in[:, :, QK_COLS + V_COLS + GATE_COLS:]
    return jnp.pad(ff, ((0, 0), (0, 0), (0, LANES - FOX_HEADS))).astype(jnp.bfloat16)


def _qk_col_scale():
    one = lambda n: jnp.ones((n,), jnp.float32)
    full = lambda n, v: jnp.full((n,), v, jnp.float32)
    return jnp.concatenate([
        full(FOX_W, HEAD_DIM ** -0.5 * LOG2E), one(FOX_W),
        full(SB_W, HEAD_DIM ** -0.5 * LOG2E), one(SB_W),
        full(DIFF_W, DIFF_QK_DIM ** -0.5 * LOG2E), one(DIFF_W)])[None, :]


def kernel(x, w_in, b_f, diff_lambda, diff_subln_g, w_out, ln_g, ln_b):
    batch, seq, d_model = x.shape
    depth = w_in.shape[0]
    t = batch * seq
    tk = min(seq, KEY_BLOCK)
    tq = min(seq, ATTN_TQ)
    tm = min(t, 1024)

    w_perm = _permuted_in_weights(w_in)
    w_ff = _forget_in_weights(w_in)
    w_out_b = w_out.astype(jnp.bfloat16)
    qk_scale = _qk_col_scale()
    b_f_rows = jnp.pad(b_f, ((0, 0), (0, LANES - FOX_HEADS)))
    ln_g3 = ln_g.reshape(depth, 1, d_model)
    ln_b3 = ln_b.reshape(depth, 1, d_model)

    xf = x.reshape(t, d_model)
    xb = xf
    for l in range(depth):
        lam_init = 0.8 - 0.6 * math.exp(-0.3 * l)
        qk = _project(xb, w_perm, l, 0, QK_COLS, qk_scale, jnp.bfloat16, tm, 1024, "qk_projection")
        vt = _project(xb, w_perm, l, QK_COLS, V_COLS, None, jnp.bfloat16, tm, 1024, "v_projection",
                      transpose_out=True)
        gates = _project_gates(xb, w_perm, w_ff, l, QK_COLS + V_COLS, min(t, 512))
        c = _forget_prefix(gates, b_f_rows[l:l + 1], batch, seq, tk)
        o_fox = _fox_attention(qk, vt, gates, c, batch, seq, tq, tk, FOX_GROUP)
        o_sb = _sb_attention(qk, vt, gates, batch, seq, tq, tk, SB_GROUP)
        o_diff = _diff_attention(qk, vt, gates, diff_lambda[l], diff_subln_g[l:l + 1], batch, seq, tq, tk,
                                 lam_init, DIFF_GROUP)
        xf, xb = _merge(o_fox, o_sb, o_diff, w_out_b, l, xf, ln_g3, ln_b3, min(t, 512))
    return xf.reshape(batch, seq, d_model)
```

```python
import functools
import math

import jax
import jax.numpy as jnp
from jax import lax
from jax.experimental import pallas as pl
from jax.experimental.pallas import tpu as pltpu

HEAD_DIM = 128
FOX_HEADS = 6
SB_HEADS = 6
DIFF_HEADS = 4
DIFF_QK_DIM = HEAD_DIM // 2
FOX_W = FOX_HEADS * HEAD_DIM
SB_W = SB_HEADS * HEAD_DIM
DIFF_W = DIFF_HEADS * HEAD_DIM
DEPTH_FOR_DEEPNORM = 4
DEEPNORM_ALPHA = (2 * DEPTH_FOR_DEEPNORM) ** 0.25
LN_EPS = 1e-5
SUBLN_EPS = 1e-5
NEG_BIG = -1e30
LOG2E = math.log2(math.e)

LANES = 128
VMEM_LIMIT_BYTES = 56 * 1024 * 1024

FOX_Q0, FOX_K0 = 0, FOX_HEADS
SB_Q0, SB_K0 = 2 * FOX_HEADS, 2 * FOX_HEADS + SB_HEADS
DIFF_Q0, DIFF_K0 = 2 * (FOX_HEADS + SB_HEADS), 2 * (FOX_HEADS + SB_HEADS) + DIFF_HEADS
FOX_V0, SB_V0, DIFF_V0 = 0, FOX_HEADS, FOX_HEADS + SB_HEADS
GATE_FOX0, GATE_SB0, GATE_DIFF0 = 0, FOX_HEADS, FOX_HEADS + SB_HEADS
GATE_FF = FOX_HEADS + SB_HEADS + DIFF_HEADS
GATE_BLOCKS = GATE_FF + 1
FOX_GROUP, SB_GROUP, DIFF_GROUP = 3, 3, 2
ATTN_TQ = 2048
KEY_BLOCK = 256
SB_SUB = 256
PROJ_ROWS_F32, PROJ_ROWS_BF16 = 1024, 2048


def _params(*semantics):
    return pltpu.CompilerParams(dimension_semantics=semantics, vmem_limit_bytes=VMEM_LIMIT_BYTES)


def _proj_kernel(x_ref, w_ref, *rest, transpose_out):
    o_ref = rest[-1]
    acc = jnp.dot(x_ref[...].astype(w_ref.dtype), w_ref[...], preferred_element_type=jnp.float32)
    if len(rest) == 2:
        acc = acc * rest[0][...]
    for c in range(o_ref.shape[0]):
        blk = acc[:, c * LANES:(c + 1) * LANES]
        o_ref[c] = (blk.T if transpose_out else blk).astype(o_ref.dtype)


def _gate_proj_kernel(x_ref, w_ref, wf_ref, o_ref):
    x = x_ref[...].astype(w_ref.dtype)
    acc = jnp.dot(x, w_ref[...], preferred_element_type=jnp.float32)
    for c in range(GATE_FF):
        o_ref[c] = acc[:, c * LANES:(c + 1) * LANES]
    o_ref[GATE_FF] = jnp.dot(x, wf_ref[...], preferred_element_type=jnp.float32)


def _project_gates(x, w, w_ff, layer, col0, tm):
    t, k = x.shape
    return pl.pallas_call(
        _gate_proj_kernel,
        grid=(t // tm,),
        in_specs=[pl.BlockSpec((tm, k), lambda i: (i, 0)),
                  pl.BlockSpec((None, k, GATE_COLS), lambda i: (layer, 0, col0 // GATE_COLS)),
                  pl.BlockSpec((None, k, LANES), lambda i: (layer, 0, 0))],
        out_specs=pl.BlockSpec((GATE_BLOCKS, tm, LANES), lambda i: (0, i, 0)),
        out_shape=jax.ShapeDtypeStruct((GATE_BLOCKS, t, LANES), jnp.float32),
        compiler_params=_params("parallel"),
        name="gate_projection",
    )(x, w, w_ff)


def _project(x, w, layer, col0, n, col_scale, out_dtype, tm, tn, name, transpose_out=False):
    t, k = x.shape
    if transpose_out:
        out_spec = pl.BlockSpec((tn // LANES, LANES, tm), lambda i, j: (j, 0, i))
        out_shape = (n // LANES, LANES, t)
    else:
        out_spec = pl.BlockSpec((tn // LANES, tm, LANES), lambda i, j: (j, i, 0))
        out_shape = (n // LANES, t, LANES)
    in_specs = [pl.BlockSpec((tm, k), lambda i, j: (i, 0)),
                pl.BlockSpec((None, k, tn), lambda i, j: (layer, 0, col0 // tn + j))]
    operands = [x, w]
    if col_scale is not None:
        in_specs.append(pl.BlockSpec((1, tn), lambda i, j: (0, j)))
        operands.append(col_scale)
    return pl.pallas_call(
        functools.partial(_proj_kernel, transpose_out=transpose_out),
        grid=(t // tm, n // tn),
        in_specs=in_specs,
        out_specs=out_spec,
        out_shape=jax.ShapeDtypeStruct(out_shape, out_dtype),
        compiler_params=_params("parallel", "parallel"),
        name=name,
    )(*operands)


def _log_sigmoid(z):
    return jnp.minimum(z, 0.0) - jnp.log1p(jnp.exp(-jnp.abs(z)))


def _split3(x):
    x1 = x.astype(jnp.bfloat16)
    r = x - x1.astype(jnp.float32)
    x2 = r.astype(jnp.bfloat16)
    x3 = (r - x2.astype(jnp.float32)).astype(jnp.bfloat16)
    return x1, x2, x3


def _decay_kernel(ff_ref, bf_ref, tri_ref, c_ref, *, chunk):
    s = ff_ref.shape[1]
    tri = tri_ref[...]
    carry = jnp.zeros((1, LANES), jnp.float32)
    for c in range(s // chunk):
        lf = _log_sigmoid(ff_ref[0, c * chunk:(c + 1) * chunk, :] + bf_ref[...])
        x1, x2, x3 = _split3(lf)
        cs = (jnp.dot(tri, x1, preferred_element_type=jnp.float32)
              + jnp.dot(tri, x2, preferred_element_type=jnp.float32)
              + jnp.dot(tri, x3, preferred_element_type=jnp.float32)) + carry
        c_ref[0, c * chunk:(c + 1) * chunk, :] = cs
        carry = cs[chunk - 1:chunk, :]


def _forget_prefix(gates, b_f_row, batch, seq, chunk):
    tri = (lax.broadcasted_iota(jnp.int32, (chunk, chunk), 0)
           >= lax.broadcasted_iota(jnp.int32, (chunk, chunk), 1)).astype(jnp.bfloat16)
    return pl.pallas_call(
        functools.partial(_decay_kernel, chunk=chunk),
        grid=(batch,),
        in_specs=[
            pl.BlockSpec((1, seq, LANES), lambda b: (GATE_FF, b, 0)),
            pl.BlockSpec((1, LANES), lambda b: (0, 0)),
            pl.BlockSpec((chunk, chunk), lambda b: (0, 0)),
        ],
        out_specs=pl.BlockSpec((1, seq, LANES), lambda b: (b, 0, 0)),
        out_shape=jax.ShapeDtypeStruct((batch, seq, LANES), jnp.float32),
        compiler_params=_params("parallel"),
        name="forget_prefix",
    )(gates, b_f_row, tri)


def _kq(k, q):
    return lax.dot_general(k, q, (((1,), (1,)), ((), ())), preferred_element_type=jnp.float32)


def _silu(g):
    return g * (1.0 / (1.0 + jnp.exp(-g)))


def _lane_tiles(a, tile):
    n = a.shape[1] // LANES
    return jnp.concatenate([tile(a[:, i * LANES:(i + 1) * LANES]) for i in range(n)], axis=1)


def _gather_lanes(x, ranges):
    if ranges is None:
        return x
    return jnp.concatenate([x[:, a:b] for a, b in ranges], axis=1)


def _scatter_lanes(x, new, ranges):
    if ranges is None:
        return new
    pieces, pos, off = [], 0, 0
    for a, b in ranges:
        if a > pos:
            pieces.append(x[:, pos:a])
        pieces.append(new[:, off:off + b - a])
        off += b - a
        pos = b
    if pos < x.shape[1]:
        pieces.append(x[:, pos:])
    return jnp.concatenate(pieces, axis=1)


ONES_ROWS = 16


def _with_ones_rows(vt):
    ones = (lax.broadcasted_iota(jnp.int32, (ONES_ROWS, vt.shape[1]), 0) == 0).astype(vt.dtype)
    return jnp.concatenate([vt, ones], axis=0)


def _softmax_steps(sts, vts, carry, ranges=None, mask=None):
    stats = []
    for st, (m, _) in zip(sts, carry):
        if mask is not None:
            st = jnp.where(mask, st, NEG_BIG)
        m_old = _gather_lanes(m, ranges)
        m_new = jnp.maximum(m_old, jnp.max(st, axis=0, keepdims=True))
        stats.append((m_new, jnp.exp2(m_old - m_new), jnp.exp2(st - m_new)))
    out = []
    for vt, (m, acc), (m_new, alpha, p) in zip(vts, carry, stats):
        acc_new = alpha * _gather_lanes(acc, ranges) + jnp.dot(
            vt, p.astype(vt.dtype), preferred_element_type=jnp.float32)
        out.append((_scatter_lanes(m, m_new, ranges), _scatter_lanes(acc, acc_new, ranges)))
    return tuple(out)


def _normalised(acc):
    return acc[0:HEAD_DIM] * (1.0 / acc[HEAD_DIM:HEAD_DIM + 1])


def _attn_specs(q0, k0, v0, g0, seq, heads):
    blk = lambda base: pl.BlockSpec((heads, seq, LANES), lambda b, hg: (base // heads + hg, b, 0))
    vt = pl.BlockSpec((heads, LANES, seq), lambda b, hg: (v0 // heads + hg, 0, b))
    return blk(q0), blk(k0), vt, blk(g0)


def _causal(tk, width, strict=False):
    rows = lax.broadcasted_iota(jnp.int32, (tk, width), 0)
    cols = lax.broadcasted_iota(jnp.int32, (tk, width), 1)
    return cols > rows if strict else cols >= rows


def _fox_kernel(q_ref, k_ref, vt_ref, g_ref, c_ref, o_ref, bias_ref, vta_ref, *, tq, tk):
    heads, seq = q_ref.shape[0], q_ref.shape[1]
    ratio = tq // tk
    lane = lax.broadcasted_iota(jnp.int32, (seq, LANES), 1)
    for g in range(heads):
        h = pl.program_id(1) * heads + g
        col = jnp.sum(jnp.where(lane == h, c_ref[0], 0.0), axis=1, keepdims=True)
        bias_ref[g] = jnp.broadcast_to(col * (-LOG2E), (seq, LANES))
        vta_ref[g] = _with_ones_rows(vt_ref[g])

    for qi in range(seq // tq):
        r0 = qi * tq
        qs = [q_ref[g, r0:r0 + tq, :] for g in range(heads)]

        def scores(c0, lane0):
            return [_lane_tiles(_kq(k_ref[g, pl.ds(c0, tk), :], qs[g][lane0:]),
                                lambda a, g=g: a + bias_ref[g, pl.ds(c0, tk), :])
                    for g in range(heads)]

        def values(c0):
            return [vta_ref[g, :, pl.ds(c0, tk)] for g in range(heads)]

        def body(j, carry):
            c0 = pl.multiple_of(j * tk, tk)
            return _softmax_steps(scores(c0, 0), values(c0), carry)

        init = (jnp.full((1, tq), NEG_BIG, jnp.float32),
                jnp.zeros((HEAD_DIM + ONES_ROWS, tq), jnp.float32))
        carry = lax.fori_loop(0, qi * ratio, body, (init,) * heads)
        for d in range(ratio):
            ranges = [(d * tk, tq)] if d else None
            carry = _softmax_steps(scores(r0 + d * tk, d * tk), values(r0 + d * tk), carry, ranges,
                                   _causal(tk, tq - d * tk))
        for g in range(heads):
            out = _normalised(carry[g][1]).T * _silu(g_ref[g, r0:r0 + tq, :])
            o_ref[r0:r0 + tq, g * LANES:(g + 1) * LANES] = out.astype(o_ref.dtype)


def _fox_attention(qk, vt, gates, c, batch, seq, tq, tk, heads):
    q_s, k_s, v_s, g_s = _attn_specs(FOX_Q0, FOX_K0, FOX_V0, GATE_FOX0, seq, heads)
    return pl.pallas_call(
        functools.partial(_fox_kernel, tq=tq, tk=tk),
        grid=(batch, FOX_HEADS // heads),
        in_specs=[q_s, k_s, v_s, g_s, pl.BlockSpec((1, seq, LANES), lambda b, hg: (b, 0, 0))],
        out_specs=pl.BlockSpec((seq, heads * LANES), lambda b, hg: (b, hg)),
        out_shape=jax.ShapeDtypeStruct((batch * seq, FOX_W), jnp.bfloat16),
        scratch_shapes=[pltpu.VMEM((heads, seq, LANES), jnp.float32),
                        pltpu.VMEM((heads, HEAD_DIM + ONES_ROWS, seq), jnp.bfloat16)],
        compiler_params=_params("parallel", "parallel"),
        name="fox_attention",
    )(qk, qk, vt, gates, c)


def _sb_blocks(qs, ks, vts, ut, states, ranges, strict):
    tk = ks[0].shape[0]
    sub_keys = ut.shape[0]
    zs = [_kq(k, q) for k, q in zip(ks, qs)]
    parts = []
    for z in zs:
        neg_abs = lax.bitcast_convert_type(
            lax.bitcast_convert_type(z, jnp.uint32) | jnp.uint32(0x80000000), jnp.float32)
        sp = jnp.log2(1.0 + jnp.exp2(neg_abs))
        log_beta = jnp.minimum(z, 0.0) - sp
        log_keep = log_beta - z
        if strict is not None:
            log_keep = jnp.where(strict, log_keep, 0.0)
        parts.append((log_beta, log_keep, log_keep.astype(jnp.bfloat16)))
    subs = []
    for _, _, terms in parts:
        subs.append([jnp.dot(ut, terms[i:i + sub_keys], preferred_element_type=jnp.float32)
                     for i in range(0, tk, sub_keys)])
    ws, carries = [], []
    for (log_beta, log_keep, _), sub, (carry_all, _) in zip(parts, subs, states):
        carry = _gather_lanes(carry_all, ranges)
        laters = [None] * len(sub)
        for n in reversed(range(len(sub))):
            laters[n] = sub[n] + carry
            carry = carry + sub[n][0:1] + log_keep[n * sub_keys:n * sub_keys + 1]
        w = jnp.exp2(log_beta + jnp.concatenate(laters, axis=0))
        if strict is not None:
            w = jnp.where(strict, w, 0.0)
        ws.append(w)
        carries.append(_scatter_lanes(carry_all, carry, ranges))
    out = []
    for w, vt, carry, (_, acc) in zip(ws, vts, carries, states):
        acc_new = _gather_lanes(acc, ranges) + jnp.dot(vt, w.astype(vt.dtype),
                                                       preferred_element_type=jnp.float32)
        out.append((carry, _scatter_lanes(acc, acc_new, ranges)))
    return tuple(out)


def _sb_kernel(q_ref, k_ref, vt_ref, g_ref, ut_ref, o_ref, *, tq, tk):
    heads, seq = q_ref.shape[0], q_ref.shape[1]
    ratio = tq // tk
    ut = ut_ref[...]

    for qi in range(seq // tq):
        r0 = qi * tq
        qs = [q_ref[g, r0:r0 + tq, :] for g in range(heads)]

        def block(c0, lane0, state, ranges, mask):
            return _sb_blocks([q[lane0:] for q in qs],
                              [k_ref[g, pl.ds(c0, tk), :] for g in range(heads)],
                              [vt_ref[g, :, pl.ds(c0, tk)] for g in range(heads)], ut, state, ranges, mask)

        init = (jnp.zeros((1, tq), jnp.float32), jnp.zeros((HEAD_DIM, tq), jnp.float32))
        state = (init,) * heads
        for d in reversed(range(ratio)):
            state = block(r0 + d * tk, d * tk, state, [(d * tk, tq)] if d else None,
                          _causal(tk, tq - d * tk, strict=True))
        n_full = qi * ratio
        state = lax.fori_loop(
            0, n_full,
            lambda i, st: block(pl.multiple_of((n_full - 1 - i) * tk, tk), 0, st, None, None), state)
        for g in range(heads):
            out = state[g][1].T * _silu(g_ref[g, r0:r0 + tq, :])
            o_ref[r0:r0 + tq, g * LANES:(g + 1) * LANES] = out.astype(o_ref.dtype)


def _sb_attention(qk, vt, gates, batch, seq, tq, tk, heads):
    q_s, k_s, v_s, g_s = _attn_specs(SB_Q0, SB_K0, SB_V0, GATE_SB0, seq, heads)
    sub = min(SB_SUB, tk)
    ut = (lax.broadcasted_iota(jnp.int32, (sub, sub), 1)
          > lax.broadcasted_iota(jnp.int32, (sub, sub), 0)).astype(jnp.bfloat16)
    return pl.pallas_call(
        functools.partial(_sb_kernel, tq=tq, tk=tk),
        grid=(batch, SB_HEADS // heads),
        in_specs=[q_s, k_s, v_s, g_s, pl.BlockSpec((sub, sub), lambda b, hg: (0, 0))],
        out_specs=pl.BlockSpec((seq, heads * LANES), lambda b, hg: (b, hg)),
        out_shape=jax.ShapeDtypeStruct((batch * seq, SB_W), jnp.bfloat16),
        compiler_params=_params("parallel", "parallel"),
        name="sb_attention",
    )(qk, qk, vt, gates, ut)


def _diff_kernel(lam_ref, gsub_ref, q_ref, k_ref, vt_ref, g_ref, o_ref, vta_ref, *, tq, tk, lam_init):
    heads, seq = q_ref.shape[0], q_ref.shape[1]
    ratio = tq // tk
    lp = lam_ref[...]
    lam = (jnp.exp(jnp.sum(lp[0:1] * lp[1:2], axis=1, keepdims=True))
           - jnp.exp(jnp.sum(lp[2:3] * lp[3:4], axis=1, keepdims=True)) + lam_init)
    key = lax.broadcasted_iota(jnp.int32, (tk, LANES), 0).astype(jnp.float32)
    ramps = []
    for g in range(heads):
        h = pl.program_id(1) * heads + g
        expo = jnp.full((tk, LANES), 127 - (8 // DIFF_HEADS) * (h + 1), jnp.int32)
        slope = lax.bitcast_convert_type(expo << 23, jnp.float32) * LOG2E
        ramps.append((slope, slope * key))
        vta_ref[g] = _with_ones_rows(vt_ref[g])
    lane = lax.broadcasted_iota(jnp.int32, (tq, HEAD_DIM), 1)

    for qi in range(seq // tq):
        r0 = qi * tq
        q1s, q2s = [], []
        for g in range(heads):
            q = q_ref[g, r0:r0 + tq, :]
            zero = jnp.zeros_like(q)
            q1s.append(jnp.where(lane < DIFF_QK_DIM, q, zero))
            q2s.append(jnp.where(lane >= DIFF_QK_DIM, q, zero))

        def scores(c0, lane0):
            out = []
            for g in range(heads):
                slope, ramp = ramps[g]
                bias = ramp + slope * jnp.asarray(c0 - r0, jnp.float32)
                qq = jnp.concatenate([q1s[g][lane0:], q2s[g][lane0:]], axis=0)
                out.append(_lane_tiles(_kq(k_ref[g, pl.ds(c0, tk), :], qq), lambda a, b=bias: a + b))
            return out

        def values(c0):
            return [vta_ref[g, :, pl.ds(c0, tk)] for g in range(heads)]

        def body(j, carry):
            c0 = pl.multiple_of(j * tk, tk)
            return _softmax_steps(scores(c0, 0), values(c0), carry)

        init = (jnp.full((1, 2 * tq), NEG_BIG, jnp.float32),
                jnp.zeros((HEAD_DIM + ONES_ROWS, 2 * tq), jnp.float32))
        carry = lax.fori_loop(0, qi * ratio, body, (init,) * heads)
        for d in range(ratio):
            ranges = [(d * tk, tq), (tq + d * tk, 2 * tq)] if d else None
            half = _causal(tk, tq - d * tk)
            carry = _softmax_steps(scores(r0 + d * tk, d * tk), values(r0 + d * tk), carry, ranges,
                                   jnp.concatenate([half, half], axis=1))
        for g in range(heads):
            o = _normalised(carry[g][1])
            o = o[:, 0:tq] - lam * o[:, tq:2 * tq]
            o = o * lax.rsqrt(jnp.mean(o * o, axis=0, keepdims=True) + SUBLN_EPS)
            o = o.T * gsub_ref[...] * (1.0 - lam_init)
            out = o * _silu(g_ref[g, r0:r0 + tq, :])
            o_ref[r0:r0 + tq, g * LANES:(g + 1) * LANES] = out.astype(o_ref.dtype)


def _diff_attention(qk, vt, gates, lam_p, subln_row, batch, seq, tq, tk, lam_init, heads):
    q_s, k_s, v_s, g_s = _attn_specs(DIFF_Q0, DIFF_K0, DIFF_V0, GATE_DIFF0, seq, heads)
    return pl.pallas_call(
        functools.partial(_diff_kernel, tq=tq, tk=tk, lam_init=lam_init),
        grid=(batch, DIFF_HEADS // heads),
        in_specs=[pl.BlockSpec(lam_p.shape, lambda b, hg: (0, 0)),
                  pl.BlockSpec((1, HEAD_DIM), lambda b, hg: (0, 0)),
                  q_s, k_s, v_s, g_s],
        out_specs=pl.BlockSpec((seq, heads * LANES), lambda b, hg: (b, hg)),
        out_shape=jax.ShapeDtypeStruct((batch * seq, DIFF_W), jnp.bfloat16),
        scratch_shapes=[pltpu.VMEM((heads, HEAD_DIM + ONES_ROWS, seq), jnp.bfloat16)],
        compiler_params=_params("parallel", "parallel"),
        name="diff_attention",
    )(lam_p, subln_row, qk, qk, vt, gates)


def _merge_kernel(of_ref, os_ref, od_ref, wf_ref, ws_ref, wd_ref, x_ref, g_ref, b_ref, y_ref, yb_ref):
    y = (jnp.dot(of_ref[...], wf_ref[...], preferred_element_type=jnp.float32)
         + jnp.dot(os_ref[...], ws_ref[...], preferred_element_type=jnp.float32)
         + jnp.dot(od_ref[...], wd_ref[...], preferred_element_type=jnp.float32))
    z = DEEPNORM_ALPHA * x_ref[...] + y
    mu = jnp.mean(z, axis=1, keepdims=True)
    zc = z - mu
    var = jnp.mean(zc * zc, axis=1, keepdims=True)
    out = zc * lax.rsqrt(var + LN_EPS) * g_ref[...] + b_ref[...]
    y_ref[...] = out
    yb_ref[...] = out.astype(yb_ref.dtype)


def _merge(o_fox, o_sb, o_diff, w_out, layer, x, ln_g, ln_b, tm):
    t, d = x.shape
    row = lambda w: pl.BlockSpec((tm, w), lambda i: (i, 0))
    w_rows = lambda rows, start: pl.BlockSpec((None, rows, d), lambda i: (layer, start // rows, 0))
    vec = lambda: pl.BlockSpec((None, 1, d), lambda i: (layer, 0, 0))
    return pl.pallas_call(
        _merge_kernel,
        grid=(t // tm,),
        in_specs=[row(FOX_W), row(SB_W), row(DIFF_W),
                  w_rows(FOX_W, 0), w_rows(SB_W, FOX_W), w_rows(DIFF_W, FOX_W + SB_W),
                  row(d), vec(), vec()],
        out_specs=[row(d), row(d)],
        out_shape=[jax.ShapeDtypeStruct((t, d), jnp.float32),
                   jax.ShapeDtypeStruct((t, d), jnp.bfloat16)],
        compiler_params=_params("parallel"),
        name="merge_layernorm",
    )(o_fox, o_sb, o_diff, w_out, w_out, w_out, x, ln_g, ln_b)


PREP_UNIT = 256
QK_COLS = 2 * (FOX_W + SB_W + DIFF_W)
V_COLS = FOX_W + SB_W + DIFF_W
GATE_COLS = FOX_W + SB_W + DIFF_W


def _cast_kernel(w_ref, o_ref):
    o_ref[...] = w_ref[...].astype(o_ref.dtype)


def _permuted_in_weights(w_in):
    depth, k, _ = w_in.shape
    sizes = (FOX_W,) * 4 + (SB_W,) * 4 + (DIFF_W,) * 4
    starts = [sum(sizes[:i]) for i in range(len(sizes))]
    fq, fk, fv, fg, sq, sk, sv, sg, dq, dk, dv, dg = range(12)
    order = [fq, fk, sq, sk, dq, dk, fv, sv, dv, fg, sg, dg]
    steps, dst, shift = [], 0, 0
    for grp in order:
        new_shift = (starts[grp] - dst) // PREP_UNIT
        steps.append((dst // PREP_UNIT, new_shift - shift))
        shift = new_shift
        dst += sizes[grp]
    n_units = dst // PREP_UNIT

    def src_unit(j):
        return j + sum(jnp.where(j >= first, delta, 0) for first, delta in steps)

    return pl.pallas_call(
        _cast_kernel,
        grid=(depth, n_units),
        in_specs=[pl.BlockSpec((None, k, PREP_UNIT), lambda l, j: (l, 0, src_unit(j)))],
        out_specs=pl.BlockSpec((None, k, PREP_UNIT), lambda l, j: (l, 0, j)),
        out_shape=jax.ShapeDtypeStruct((depth, k, n_units * PREP_UNIT), jnp.bfloat16),
        compiler_params=_params("parallel", "parallel"),
        name="permute_in_weights",
    )(w_in)


def _forget_in_weights(w_in):
    ff = w_in[:, :, QK_COLS + V_COLS + GATE_COLS:]
    return jnp.pad(ff, ((0, 0), (0, 0), (0, LANES - FOX_HEADS))).astype(jnp.bfloat16)


def _qk_col_scale():
    one = lambda n: jnp.ones((n,), jnp.float32)
    full = lambda n, v: jnp.full((n,), v, jnp.float32)
    return jnp.concatenate([
        full(FOX_W, HEAD_DIM ** -0.5 * LOG2E), one(FOX_W),
        full(SB_W, HEAD_DIM ** -0.5 * LOG2E), one(SB_W),
        full(DIFF_W, DIFF_QK_DIM ** -0.5 * LOG2E), one(DIFF_W)])[None, :]


def kernel(x, w_in, b_f, diff_lambda, diff_subln_g, w_out, ln_g, ln_b):
    batch, seq, d_model = x.shape
    depth = w_in.shape[0]
    t = batch * seq
    tk = min(seq, KEY_BLOCK)
    tq = min(seq, ATTN_TQ)

    w_in_b = w_in.astype(jnp.bfloat16)
    w_perm = _permuted_in_weights(w_in_b)
    w_ff = _forget_in_weights(w_in_b)
    w_out_b = w_out.astype(jnp.bfloat16)
    qk_scale = _qk_col_scale()
    b_f_rows = jnp.pad(b_f, ((0, 0), (0, LANES - FOX_HEADS)))
    ln_g3 = ln_g.reshape(depth, 1, d_model)
    ln_b3 = ln_b.reshape(depth, 1, d_model)

    xf = x.reshape(t, d_model)
    xb = xf
    for l in range(depth):
        lam_init = 0.8 - 0.6 * math.exp(-0.3 * l)
        tm = min(t, PROJ_ROWS_F32 if xb.dtype == jnp.float32 else PROJ_ROWS_BF16)
        qk = _project(xb, w_perm, l, 0, QK_COLS, qk_scale, jnp.bfloat16, tm, 1024, "qk_projection")
        vt = _project(xb, w_perm, l, QK_COLS, V_COLS, None, jnp.bfloat16, tm, 1024, "v_projection",
                      transpose_out=True)
        gates = _project_gates(xb, w_perm, w_ff, l, QK_COLS + V_COLS, min(t, 512))
        c = _forget_prefix(gates, b_f_rows[l:l + 1], batch, seq, tk)
        o_fox = _fox_attention(qk, vt, gates, c, batch, seq, tq, tk, FOX_GROUP)
        o_sb = _sb_attention(qk, vt, gates, batch, seq, tq, tk, SB_GROUP)
        o_diff = _diff_attention(qk, vt, gates, diff_lambda[l], diff_subln_g[l:l + 1], batch, seq, tq, tk,
                                 lam_init, DIFF_GROUP)
        xf, xb = _merge(o_fox, o_sb, o_diff, w_out_b, l, xf, ln_g3, ln_b3, min(t, 512))
    return xf.reshape(batch, seq, d_model)
```

```python
import functools
import math

import jax
import jax.numpy as jnp
from jax import lax
from jax.experimental import pallas as pl
from jax.experimental.pallas import tpu as pltpu

HEAD_DIM = 128
FOX_HEADS = 6
SB_HEADS = 6
DIFF_HEADS = 4
DIFF_QK_DIM = HEAD_DIM // 2
FOX_W = FOX_HEADS * HEAD_DIM
SB_W = SB_HEADS * HEAD_DIM
DIFF_W = DIFF_HEADS * HEAD_DIM
DEPTH_FOR_DEEPNORM = 4
DEEPNORM_ALPHA = (2 * DEPTH_FOR_DEEPNORM) ** 0.25
LN_EPS = 1e-5
SUBLN_EPS = 1e-5
NEG_BIG = -1e30
LOG2E = math.log2(math.e)

LANES = 128
VMEM_LIMIT_BYTES = 56 * 1024 * 1024

FOX_Q0, FOX_K0 = 0, FOX_HEADS
SB_Q0, SB_K0 = 2 * FOX_HEADS, 2 * FOX_HEADS + SB_HEADS
DIFF_Q0, DIFF_K0 = 2 * (FOX_HEADS + SB_HEADS), 2 * (FOX_HEADS + SB_HEADS) + DIFF_HEADS
FOX_V0, SB_V0, DIFF_V0 = 0, FOX_HEADS, FOX_HEADS + SB_HEADS
GATE_FOX0, GATE_SB0, GATE_DIFF0 = 0, FOX_HEADS, FOX_HEADS + SB_HEADS
GATE_FF = FOX_HEADS + SB_HEADS + DIFF_HEADS
GATE_BLOCKS = GATE_FF + 1
FOX_GROUP, SB_GROUP, DIFF_GROUP = 3, 3, 2
ATTN_TQ = 2048
KEY_BLOCK = 256
SB_SUB = 256
PROJ_ROWS_F32, PROJ_ROWS_BF16 = 1024, 2048


def _params(*semantics):
    return pltpu.CompilerParams(dimension_semantics=semantics, vmem_limit_bytes=VMEM_LIMIT_BYTES)


def _proj_kernel(x_ref, w_ref, *rest, transpose_out):
    o_ref = rest[-1]
    acc = jnp.dot(x_ref[...].astype(w_ref.dtype), w_ref[...], preferred_element_type=jnp.float32)
    if len(rest) == 2:
        acc = acc * rest[0][...]
    for c in range(o_ref.shape[0]):
        blk = acc[:, c * LANES:(c + 1) * LANES]
        o_ref[c] = (blk.T if transpose_out else blk).astype(o_ref.dtype)


def _silu(g):
    return g * (1.0 / (1.0 + jnp.exp(-g)))


def _gate_proj_kernel(x_ref, w_ref, wf_ref, o_ref):
    x = x_ref[...].astype(w_ref.dtype)
    acc = jnp.dot(x, w_ref[...], preferred_element_type=jnp.float32)
    for c in range(GATE_FF):
        o_ref[c] = _silu(acc[:, c * LANES:(c + 1) * LANES])
    o_ref[GATE_FF] = jnp.dot(x, wf_ref[...], preferred_element_type=jnp.float32)


def _project_gates(x, w, w_ff, layer, col0, tm):
    t, k = x.shape
    return pl.pallas_call(
        _gate_proj_kernel,
        grid=(t // tm,),
        in_specs=[pl.BlockSpec((tm, k), lambda i: (i, 0)),
                  pl.BlockSpec((None, k, GATE_COLS), lambda i: (layer, 0, col0 // GATE_COLS)),
                  pl.BlockSpec((None, k, LANES), lambda i: (layer, 0, 0))],
        out_specs=pl.BlockSpec((GATE_BLOCKS, tm, LANES), lambda i: (0, i, 0)),
        out_shape=jax.ShapeDtypeStruct((GATE_BLOCKS, t, LANES), jnp.float32),
        compiler_params=_params("parallel"),
        name="gate_projection",
    )(x, w, w_ff)


def _project(x, w, layer, col0, n, col_scale, out_dtype, tm, tn, name, transpose_out=False):
    t, k = x.shape
    if transpose_out:
        out_spec = pl.BlockSpec((tn // LANES, LANES, tm), lambda i, j: (j, 0, i))
        out_shape = (n // LANES, LANES, t)
    else:
        out_spec = pl.BlockSpec((tn // LANES, tm, LANES), lambda i, j: (j, i, 0))
        out_shape = (n // LANES, t, LANES)
    in_specs = [pl.BlockSpec((tm, k), lambda i, j: (i, 0)),
                pl.BlockSpec((None, k, tn), lambda i, j: (layer, 0, col0 // tn + j))]
    operands = [x, w]
    if col_scale is not None:
        in_specs.append(pl.BlockSpec((1, tn), lambda i, j: (0, j)))
        operands.append(col_scale)
    return pl.pallas_call(
        functools.partial(_proj_kernel, transpose_out=transpose_out),
        grid=(t // tm, n // tn),
        in_specs=in_specs,
        out_specs=out_spec,
        out_shape=jax.ShapeDtypeStruct(out_shape, out_dtype),
        compiler_params=_params("parallel", "parallel"),
        name=name,
    )(*operands)


def _log_sigmoid(z):
    return jnp.minimum(z, 0.0) - jnp.log1p(jnp.exp(-jnp.abs(z)))


def _split3(x):
    x1 = x.astype(jnp.bfloat16)
    r = x - x1.astype(jnp.float32)
    x2 = r.astype(jnp.bfloat16)
    x3 = (r - x2.astype(jnp.float32)).astype(jnp.bfloat16)
    return x1, x2, x3


def _decay_kernel(ff_ref, bf_ref, tri_ref, c_ref, *, chunk):
    s = ff_ref.shape[1]
    tri = tri_ref[...]
    carry = jnp.zeros((1, LANES), jnp.float32)
    for c in range(s // chunk):
        lf = _log_sigmoid(ff_ref[0, c * chunk:(c + 1) * chunk, :] + bf_ref[...])
        x1, x2, x3 = _split3(lf)
        cs = (jnp.dot(tri, x1, preferred_element_type=jnp.float32)
              + jnp.dot(tri, x2, preferred_element_type=jnp.float32)
              + jnp.dot(tri, x3, preferred_element_type=jnp.float32)) + carry
        c_ref[0, c * chunk:(c + 1) * chunk, :] = cs * (-LOG2E)
        carry = cs[chunk - 1:chunk, :]


def _forget_prefix(gates, b_f_row, batch, seq, chunk):
    tri = (lax.broadcasted_iota(jnp.int32, (chunk, chunk), 0)
           >= lax.broadcasted_iota(jnp.int32, (chunk, chunk), 1)).astype(jnp.bfloat16)
    return pl.pallas_call(
        functools.partial(_decay_kernel, chunk=chunk),
        grid=(batch,),
        in_specs=[
            pl.BlockSpec((1, seq, LANES), lambda b: (GATE_FF, b, 0)),
            pl.BlockSpec((1, LANES), lambda b: (0, 0)),
            pl.BlockSpec((chunk, chunk), lambda b: (0, 0)),
        ],
        out_specs=pl.BlockSpec((1, seq, LANES), lambda b: (b, 0, 0)),
        out_shape=jax.ShapeDtypeStruct((batch, seq, LANES), jnp.float32),
        compiler_params=_params("parallel"),
        name="forget_prefix",
    )(gates, b_f_row, tri)


def _kq(k, q):
    return lax.dot_general(k, q, (((1,), (1,)), ((), ())), preferred_element_type=jnp.float32)


def _lane_tiles(a, tile):
    n = a.shape[1] // LANES
    return jnp.concatenate([tile(a[:, i * LANES:(i + 1) * LANES]) for i in range(n)], axis=1)


def _gather_lanes(x, ranges):
    if ranges is None:
        return x
    return jnp.concatenate([x[:, a:b] for a, b in ranges], axis=1)


def _scatter_lanes(x, new, ranges):
    if ranges is None:
        return new
    pieces, pos, off = [], 0, 0
    for a, b in ranges:
        if a > pos:
            pieces.append(x[:, pos:a])
        pieces.append(new[:, off:off + b - a])
        off += b - a
        pos = b
    if pos < x.shape[1]:
        pieces.append(x[:, pos:])
    return jnp.concatenate(pieces, axis=1)


ONES_ROWS = 16


def _with_ones_rows(vt):
    ones = (lax.broadcasted_iota(jnp.int32, (ONES_ROWS, vt.shape[1]), 0) == 0).astype(vt.dtype)
    return jnp.concatenate([vt, ones], axis=0)


def _softmax_steps(sts, vts, carry, ranges=None, mask=None):
    stats = []
    for st, (m, _) in zip(sts, carry):
        if mask is not None:
            st = jnp.where(mask, st, NEG_BIG)
        m_old = _gather_lanes(m, ranges)
        m_new = jnp.maximum(m_old, jnp.max(st, axis=0, keepdims=True))
        stats.append((m_new, jnp.exp2(m_old - m_new), jnp.exp2(st - m_new)))
    out = []
    for vt, (m, acc), (m_new, alpha, p) in zip(vts, carry, stats):
        acc_new = alpha * _gather_lanes(acc, ranges) + jnp.dot(
            vt, p.astype(vt.dtype), preferred_element_type=jnp.float32)
        out.append((_scatter_lanes(m, m_new, ranges), _scatter_lanes(acc, acc_new, ranges)))
    return tuple(out)


def _normalised(acc):
    return acc[0:HEAD_DIM] * (1.0 / acc[HEAD_DIM:HEAD_DIM + 1])


def _attn_specs(q0, k0, v0, g0, seq, heads):
    blk = lambda base: pl.BlockSpec((heads, seq, LANES), lambda b, hg: (base // heads + hg, b, 0))
    vt = pl.BlockSpec((heads, LANES, seq), lambda b, hg: (v0 // heads + hg, 0, b))
    return blk(q0), blk(k0), vt, blk(g0)


def _causal(tk, width, strict=False):
    rows = lax.broadcasted_iota(jnp.int32, (tk, width), 0)
    cols = lax.broadcasted_iota(jnp.int32, (tk, width), 1)
    return cols > rows if strict else cols >= rows


def _fox_kernel(q_ref, k_ref, vt_ref, g_ref, c_ref, o_ref, bias_ref, vta_ref, *, tq, tk):
    heads, seq = q_ref.shape[0], q_ref.shape[1]
    ratio = tq // tk
    lane = lax.broadcasted_iota(jnp.int32, (seq, LANES), 1)
    for g in range(heads):
        h = pl.program_id(1) * heads + g
        col = jnp.sum(jnp.where(lane == h, c_ref[0], 0.0), axis=1, keepdims=True)
        bias_ref[g] = jnp.broadcast_to(col, (seq, LANES))
        vta_ref[g] = _with_ones_rows(vt_ref[g])

    for qi in range(seq // tq):
        r0 = qi * tq
        qs = [q_ref[g, r0:r0 + tq, :] for g in range(heads)]

        def scores(c0, lane0):
            return [_lane_tiles(_kq(k_ref[g, pl.ds(c0, tk), :], qs[g][lane0:]),
                                lambda a, g=g: a + bias_ref[g, pl.ds(c0, tk), :])
                    for g in range(heads)]

        def values(c0):
            return [vta_ref[g, :, pl.ds(c0, tk)] for g in range(heads)]

        def body(j, carry):
            c0 = pl.multiple_of(j * tk, tk)
            return _softmax_steps(scores(c0, 0), values(c0), carry)

        init = (jnp.full((1, tq), NEG_BIG, jnp.float32),
                jnp.zeros((HEAD_DIM + ONES_ROWS, tq), jnp.float32))
        carry = lax.fori_loop(0, qi * ratio, body, (init,) * heads)
        for d in range(ratio):
            ranges = [(d * tk, tq)] if d else None
            carry = _softmax_steps(scores(r0 + d * tk, d * tk), values(r0 + d * tk), carry, ranges,
                                   _causal(tk, tq - d * tk))
        for g in range(heads):
            out = _normalised(carry[g][1]).T * g_ref[g, r0:r0 + tq, :]
            o_ref[r0:r0 + tq, g * LANES:(g + 1) * LANES] = out.astype(o_ref.dtype)


def _fox_attention(qk, vt, gates, c, batch, seq, tq, tk, heads):
    q_s, k_s, v_s, g_s = _attn_specs(FOX_Q0, FOX_K0, FOX_V0, GATE_FOX0, seq, heads)
    return pl.pallas_call(
        functools.partial(_fox_kernel, tq=tq, tk=tk),
        grid=(batch, FOX_HEADS // heads),
        in_specs=[q_s, k_s, v_s, g_s, pl.BlockSpec((1, seq, LANES), lambda b, hg: (b, 0, 0))],
        out_specs=pl.BlockSpec((seq, heads * LANES), lambda b, hg: (b, hg)),
        out_shape=jax.ShapeDtypeStruct((batch * seq, FOX_W), jnp.bfloat16),
        scratch_shapes=[pltpu.VMEM((heads, seq, LANES), jnp.float32),
                        pltpu.VMEM((heads, HEAD_DIM + ONES_ROWS, seq), jnp.bfloat16)],
        compiler_params=_params("parallel", "parallel"),
        name="fox_attention",
    )(qk, qk, vt, gates, c)


def _sb_blocks(qs, ks, vts, ut, states, ranges, strict):
    tk = ks[0].shape[0]
    sub_keys = ut.shape[0]
    zs = [_kq(k, q) for k, q in zip(ks, qs)]
    parts = []
    for z in zs:
        neg_abs = lax.bitcast_convert_type(
            lax.bitcast_convert_type(z, jnp.uint32) | jnp.uint32(0x80000000), jnp.float32)
        sp = jnp.log2(1.0 + jnp.exp2(neg_abs))
        log_beta = jnp.minimum(z, 0.0) - sp
        log_keep = log_beta - z
        if strict is not None:
            log_keep = jnp.where(strict, log_keep, 0.0)
        parts.append((log_beta, log_keep, log_keep.astype(jnp.bfloat16)))
    subs = []
    for _, _, terms in parts:
        subs.append([jnp.dot(ut, terms[i:i + sub_keys], preferred_element_type=jnp.float32)
                     for i in range(0, tk, sub_keys)])
    ws, carries = [], []
    for (log_beta, log_keep, _), sub, (carry_all, _) in zip(parts, subs, states):
        carry = _gather_lanes(carry_all, ranges)
        laters = [None] * len(sub)
        for n in reversed(range(len(sub))):
            laters[n] = sub[n] + carry
            carry = carry + sub[n][0:1] + log_keep[n * sub_keys:n * sub_keys + 1]
        w = jnp.exp2(log_beta + jnp.concatenate(laters, axis=0))
        if strict is not None:
            w = jnp.where(strict, w, 0.0)
        ws.append(w)
        carries.append(_scatter_lanes(carry_all, carry, ranges))
    out = []
    for w, vt, carry, (_, acc) in zip(ws, vts, carries, states):
        acc_new = _gather_lanes(acc, ranges) + jnp.dot(vt, w.astype(vt.dtype),
                                                       preferred_element_type=jnp.float32)
        out.append((carry, _scatter_lanes(acc, acc_new, ranges)))
    return tuple(out)


def _sb_kernel(q_ref, k_ref, vt_ref, g_ref, ut_ref, o_ref, *, tq, tk):
    heads, seq = q_ref.shape[0], q_ref.shape[1]
    ratio = tq // tk
    ut = ut_ref[...]

    for qi in range(seq // tq):
        r0 = qi * tq
        qs = [q_ref[g, r0:r0 + tq, :] for g in range(heads)]

        def block(c0, lane0, state, ranges, mask):
            return _sb_blocks([q[lane0:] for q in qs],
                              [k_ref[g, pl.ds(c0, tk), :] for g in range(heads)],
                              [vt_ref[g, :, pl.ds(c0, tk)] for g in range(heads)], ut, state, ranges, mask)

        init = (jnp.zeros((1, tq), jnp.float32), jnp.zeros((HEAD_DIM, tq), jnp.float32))
        state = (init,) * heads
        for d in reversed(range(ratio)):
            state = block(r0 + d * tk, d * tk, state, [(d * tk, tq)] if d else None,
                          _causal(tk, tq - d * tk, strict=True))
        n_full = qi * ratio
        state = lax.fori_loop(
            0, n_full,
            lambda i, st: block(pl.multiple_of((n_full - 1 - i) * tk, tk), 0, st, None, None), state)
        for g in range(heads):
            out = state[g][1].T * g_ref[g, r0:r0 + tq, :]
            o_ref[r0:r0 + tq, g * LANES:(g + 1) * LANES] = out.astype(o_ref.dtype)


def _sb_attention(qk, vt, gates, batch, seq, tq, tk, heads):
    q_s, k_s, v_s, g_s = _attn_specs(SB_Q0, SB_K0, SB_V0, GATE_SB0, seq, heads)
    sub = min(SB_SUB, tk)
    ut = (lax.broadcasted_iota(jnp.int32, (sub, sub), 1)
          > lax.broadcasted_iota(jnp.int32, (sub, sub), 0)).astype(jnp.bfloat16)
    return pl.pallas_call(
        functools.partial(_sb_kernel, tq=tq, tk=tk),
        grid=(batch, SB_HEADS // heads),
        in_specs=[q_s, k_s, v_s, g_s, pl.BlockSpec((sub, sub), lambda b, hg: (0, 0))],
        out_specs=pl.BlockSpec((seq, heads * LANES), lambda b, hg: (b, hg)),
        out_shape=jax.ShapeDtypeStruct((batch * seq, SB_W), jnp.bfloat16),
        compiler_params=_params("parallel", "parallel"),
        name="sb_attention",
    )(qk, qk, vt, gates, ut)


def _diff_kernel(lam_ref, gsub_ref, q_ref, k_ref, vt_ref, g_ref, o_ref, vta_ref, *, tq, tk, lam_init):
    heads, seq = q_ref.shape[0], q_ref.shape[1]
    ratio = tq // tk
    lp = lam_ref[...]
    lam = (jnp.exp(jnp.sum(lp[0:1] * lp[1:2], axis=1, keepdims=True))
           - jnp.exp(jnp.sum(lp[2:3] * lp[3:4], axis=1, keepdims=True)) + lam_init)
    key = lax.broadcasted_iota(jnp.int32, (tk, LANES), 0).astype(jnp.float32)
    ramps = []
    for g in range(heads):
        h = pl.program_id(1) * heads + g
        expo = jnp.full((tk, LANES), 127 - (8 // DIFF_HEADS) * (h + 1), jnp.int32)
        slope = lax.bitcast_convert_type(expo << 23, jnp.float32) * LOG2E
        ramps.append((slope, slope * key))
        vta_ref[g] = _with_ones_rows(vt_ref[g])
    lane = lax.broadcasted_iota(jnp.int32, (tq, HEAD_DIM), 1)

    for qi in range(seq // tq):
        r0 = qi * tq
        q1s, q2s = [], []
        for g in range(heads):
            q = q_ref[g, r0:r0 + tq, :]
            zero = jnp.zeros_like(q)
            q1s.append(jnp.where(lane < DIFF_QK_DIM, q, zero))
            q2s.append(jnp.where(lane >= DIFF_QK_DIM, q, zero))

        def scores(c0, lane0):
            out = []
            for g in range(heads):
                slope, ramp = ramps[g]
                bias = ramp + slope * jnp.asarray(c0 - r0, jnp.float32)
                qq = jnp.concatenate([q1s[g][lane0:], q2s[g][lane0:]], axis=0)
                out.append(_lane_tiles(_kq(k_ref[g, pl.ds(c0, tk), :], qq), lambda a, b=bias: a + b))
            return out

        def values(c0):
            return [vta_ref[g, :, pl.ds(c0, tk)] for g in range(heads)]

        def body(j, carry):
            c0 = pl.multiple_of(j * tk, tk)
            return _softmax_steps(scores(c0, 0), values(c0), carry)

        init = (jnp.full((1, 2 * tq), NEG_BIG, jnp.float32),
                jnp.zeros((HEAD_DIM + ONES_ROWS, 2 * tq), jnp.float32))
        carry = lax.fori_loop(0, qi * ratio, body, (init,) * heads)
        for d in range(ratio):
            ranges = [(d * tk, tq), (tq + d * tk, 2 * tq)] if d else None
            half = _causal(tk, tq - d * tk)
            carry = _softmax_steps(scores(r0 + d * tk, d * tk), values(r0 + d * tk), carry, ranges,
                                   jnp.concatenate([half, half], axis=1))
        for g in range(heads):
            o = _normalised(carry[g][1])
            o = o[:, 0:tq] - lam * o[:, tq:2 * tq]
            o = o * lax.rsqrt(jnp.mean(o * o, axis=0, keepdims=True) + SUBLN_EPS)
            o = o.T * gsub_ref[...] * (1.0 - lam_init)
            out = o * g_ref[g, r0:r0 + tq, :]
            o_ref[r0:r0 + tq, g * LANES:(g + 1) * LANES] = out.astype(o_ref.dtype)


def _diff_attention(qk, vt, gates, lam_p, subln_row, batch, seq, tq, tk, lam_init, heads):
    q_s, k_s, v_s, g_s = _attn_specs(DIFF_Q0, DIFF_K0, DIFF_V0, GATE_DIFF0, seq, heads)
    return pl.pallas_call(
        functools.partial(_diff_kernel, tq=tq, tk=tk, lam_init=lam_init),
        grid=(batch, DIFF_HEADS // heads),
        in_specs=[pl.BlockSpec(lam_p.shape, lambda b, hg: (0, 0)),
                  pl.BlockSpec((1, HEAD_DIM), lambda b, hg: (0, 0)),
                  q_s, k_s, v_s, g_s],
        out_specs=pl.BlockSpec((seq, heads * LANES), lambda b, hg: (b, hg)),
        out_shape=jax.ShapeDtypeStruct((batch * seq, DIFF_W), jnp.bfloat16),
        scratch_shapes=[pltpu.VMEM((heads, HEAD_DIM + ONES_ROWS, seq), jnp.bfloat16)],
        compiler_params=_params("parallel", "parallel"),
        name="diff_attention",
    )(lam_p, subln_row, qk, qk, vt, gates)


def _merge_kernel(of_ref, os_ref, od_ref, wf_ref, ws_ref, wd_ref, x_ref, g_ref, b_ref, y_ref, yb_ref):
    y = (jnp.dot(of_ref[...], wf_ref[...], preferred_element_type=jnp.float32)
         + jnp.dot(os_ref[...], ws_ref[...], preferred_element_type=jnp.float32)
         + jnp.dot(od_ref[...], wd_ref[...], preferred_element_type=jnp.float32))
    z = DEEPNORM_ALPHA * x_ref[...] + y
    mu = jnp.mean(z, axis=1, keepdims=True)
    zc = z - mu
    var = jnp.mean(zc * zc, axis=1, keepdims=True)
    out = zc * lax.rsqrt(var + LN_EPS) * g_ref[...] + b_ref[...]
    y_ref[...] = out
    yb_ref[...] = out.astype(yb_ref.dtype)


def _merge(o_fox, o_sb, o_diff, w_out, layer, x, ln_g, ln_b, tm):
    t, d = x.shape
    row = lambda w: pl.BlockSpec((tm, w), lambda i: (i, 0))
    w_rows = lambda rows, start: pl.BlockSpec((None, rows, d), lambda i: (layer, start // rows, 0))
    vec = lambda: pl.BlockSpec((None, 1, d), lambda i: (layer, 0, 0))
    return pl.pallas_call(
        _merge_kernel,
        grid=(t // tm,),
        in_specs=[row(FOX_W), row(SB_W), row(DIFF_W),
                  w_rows(FOX_W, 0), w_rows(SB_W, FOX_W), w_rows(DIFF_W, FOX_W + SB_W),
                  row(d), vec(), vec()],
        out_specs=[row(d), row(d)],
        out_shape=[jax.ShapeDtypeStruct((t, d), jnp.float32),
                   jax.ShapeDtypeStruct((t, d), jnp.bfloat16)],
        compiler_params=_params("parallel"),
        name="merge_layernorm",
    )(o_fox, o_sb, o_diff, w_out, w_out, w_out, x, ln_g, ln_b)


PREP_UNIT = 256
QK_COLS = 2 * (FOX_W + SB_W + DIFF_W)
V_COLS = FOX_W + SB_W + DIFF_W
GATE_COLS = FOX_W + SB_W + DIFF_W


def _cast_kernel(w_ref, o_ref):
    o_ref[...] = w_ref[...].astype(o_ref.dtype)


def _permuted_in_weights(w_in):
    depth, k, _ = w_in.shape
    sizes = (FOX_W,) * 4 + (SB_W,) * 4 + (DIFF_W,) * 4
    starts = [sum(sizes[:i]) for i in range(len(sizes))]
    fq, fk, fv, fg, sq, sk, sv, sg, dq, dk, dv, dg = range(12)
    order = [fq, fk, sq, sk, dq, dk, fv, sv, dv, fg, sg, dg]
    steps, dst, shift = [], 0, 0
    for grp in order:
        new_shift = (starts[grp] - dst) // PREP_UNIT
        steps.append((dst // PREP_UNIT, new_shift - shift))
        shift = new_shift
        dst += sizes[grp]
    n_units = dst // PREP_UNIT

    def src_unit(j):
        return j + sum(jnp.where(j >= first, delta, 0) for first, delta in steps)

    return pl.pallas_call(
        _cast_kernel,
        grid=(depth, n_units),
        in_specs=[pl.BlockSpec((None, k, PREP_UNIT), lambda l, j: (l, 0, src_unit(j)))],
        out_specs=pl.BlockSpec((None, k, PREP_UNIT), lambda l, j: (l, 0, j)),
        out_shape=jax.ShapeDtypeStruct((depth, k, n_units * PREP_UNIT), jnp.bfloat16),
        compiler_params=_params("parallel", "parallel"),
        name="permute_in_weights",
    )(w_in)


def _forget_in_weights(w_in):
    ff = w_in[:, :, QK_COLS + V_COLS + GATE_COLS:]
    return jnp.pad(ff, ((0, 0), (0, 0), (0, LANES - FOX_HEADS))).astype(jnp.bfloat16)


def _qk_col_scale():
    one = lambda n: jnp.ones((n,), jnp.float32)
    full = lambda n, v: jnp.full((n,), v, jnp.float32)
    return jnp.concatenate([
        full(FOX_W, HEAD_DIM ** -0.5 * LOG2E), one(FOX_W),
        full(SB_W, HEAD_DIM ** -0.5 * LOG2E), one(SB_W),
        full(DIFF_W, DIFF_QK_DIM ** -0.5 * LOG2E), one(DIFF_W)])[None, :]


def kernel(x, w_in, b_f, diff_lambda, diff_subln_g, w_out, ln_g, ln_b):
    batch, seq, d_model = x.shape
    depth = w_in.shape[0]
    t = batch * seq
    tk = min(seq, KEY_BLOCK)
    tq = min(seq, ATTN_TQ)

    w_in_b = w_in.astype(jnp.bfloat16)
    w_perm = _permuted_in_weights(w_in_b)
    w_ff = _forget_in_weights(w_in_b)
    w_out_b = w_out.astype(jnp.bfloat16)
    qk_scale = _qk_col_scale()
    b_f_rows = jnp.pad(b_f, ((0, 0), (0, LANES - FOX_HEADS)))
    ln_g3 = ln_g.reshape(depth, 1, d_model)
    ln_b3 = ln_b.reshape(depth, 1, d_model)

    xf = x.reshape(t, d_model)
    xb = xf
    for l in range(depth):
        lam_init = 0.8 - 0.6 * math.exp(-0.3 * l)
        tm = min(t, PROJ_ROWS_F32 if xb.dtype == jnp.float32 else PROJ_ROWS_BF16)
        qk = _project(xb, w_perm, l, 0, QK_COLS, qk_scale, jnp.bfloat16, tm, 1024, "qk_projection")
        vt = _project(xb, w_perm, l, QK_COLS, V_COLS, None, jnp.bfloat16, tm, 1024, "v_projection",
                      transpose_out=True)
        gates = _project_gates(xb, w_perm, w_ff, l, QK_COLS + V_COLS, min(t, 512))
        c = _forget_prefix(gates, b_f_rows[l:l + 1], batch, seq, tk)
        o_fox = _fox_attention(qk, vt, gates, c, batch, seq, tq, tk, FOX_GROUP)
        o_sb = _sb_attention(qk, vt, gates, batch, seq, tq, tk, SB_GROUP)
        o_diff = _diff_attention(qk, vt, gates, diff_lambda[l], diff_subln_g[l:l + 1], batch, seq, tq, tk,
                                 lam_init, DIFF_GROUP)
        xf, xb = _merge(o_fox, o_sb, o_diff, w_out_b, l, xf, ln_g3, ln_b3, min(t, 512))
    return xf.reshape(batch, seq, d_model)
```

```python
import functools
import math

import jax
import jax.numpy as jnp
from jax import lax
from jax.experimental import pallas as pl
from jax.experimental.pallas import tpu as pltpu

HEAD_DIM = 128
FOX_HEADS = 6
SB_HEADS = 6
DIFF_HEADS = 4
DIFF_QK_DIM = HEAD_DIM // 2
FOX_W = FOX_HEADS * HEAD_DIM
SB_W = SB_HEADS * HEAD_DIM
DIFF_W = DIFF_HEADS * HEAD_DIM
DEPTH_FOR_DEEPNORM = 4
DEEPNORM_ALPHA = (2 * DEPTH_FOR_DEEPNORM) ** 0.25
LN_EPS = 1e-5
SUBLN_EPS = 1e-5
NEG_BIG = -1e30
LOG2E = math.log2(math.e)

LANES = 128
VMEM_LIMIT_BYTES = 56 * 1024 * 1024

FOX_Q0, FOX_K0 = 0, FOX_HEADS
SB_Q0, SB_K0 = 2 * FOX_HEADS, 2 * FOX_HEADS + SB_HEADS
DIFF_Q0, DIFF_K0 = 2 * (FOX_HEADS + SB_HEADS), 2 * (FOX_HEADS + SB_HEADS) + DIFF_HEADS
FOX_V0, SB_V0, DIFF_V0 = 0, FOX_HEADS, FOX_HEADS + SB_HEADS
GATE_FOX0, GATE_SB0, GATE_DIFF0 = 0, FOX_HEADS, FOX_HEADS + SB_HEADS
GATE_FF = FOX_HEADS + SB_HEADS + DIFF_HEADS
GATE_BLOCKS = GATE_FF + 1
FOX_GROUP, SB_GROUP, DIFF_GROUP = 3, 2, 2
ATTN_TQ = 2048
KEY_BLOCK = 256
SB_SUB = 256
PROJ_ROWS_F32, PROJ_ROWS_BF16 = 1024, 2048


def _params(*semantics):
    return pltpu.CompilerParams(dimension_semantics=semantics, vmem_limit_bytes=VMEM_LIMIT_BYTES)


def _proj_kernel(x_ref, w_ref, *rest, transpose_out):
    o_ref = rest[-1]
    acc = jnp.dot(x_ref[...].astype(w_ref.dtype), w_ref[...], preferred_element_type=jnp.float32)
    if len(rest) == 2:
        acc = acc * rest[0][...]
    for c in range(o_ref.shape[0]):
        blk = acc[:, c * LANES:(c + 1) * LANES]
        o_ref[c] = (blk.T if transpose_out else blk).astype(o_ref.dtype)


def _gate_proj_kernel(x_ref, w_ref, wf_ref, o_ref):
    x = x_ref[...].astype(w_ref.dtype)
    acc = jnp.dot(x, w_ref[...], preferred_element_type=jnp.float32)
    for c in range(GATE_FF):
        o_ref[c] = acc[:, c * LANES:(c + 1) * LANES]
    o_ref[GATE_FF] = jnp.dot(x, wf_ref[...], preferred_element_type=jnp.float32)


def _project_gates(x, w, w_ff, layer, col0, tm):
    t, k = x.shape
    return pl.pallas_call(
        _gate_proj_kernel,
        grid=(t // tm,),
        in_specs=[pl.BlockSpec((tm, k), lambda i: (i, 0)),
                  pl.BlockSpec((None, k, GATE_COLS), lambda i: (layer, 0, col0 // GATE_COLS)),
                  pl.BlockSpec((None, k, LANES), lambda i: (layer, 0, 0))],
        out_specs=pl.BlockSpec((GATE_BLOCKS, tm, LANES), lambda i: (0, i, 0)),
        out_shape=jax.ShapeDtypeStruct((GATE_BLOCKS, t, LANES), jnp.float32),
        compiler_params=_params("parallel"),
        name="gate_projection",
    )(x, w, w_ff)


def _project(x, w, layer, col0, n, col_scale, out_dtype, tm, tn, name, transpose_out=False):
    t, k = x.shape
    if transpose_out:
        out_spec = pl.BlockSpec((tn // LANES, LANES, tm), lambda i, j: (j, 0, i))
        out_shape = (n // LANES, LANES, t)
    else:
        out_spec = pl.BlockSpec((tn // LANES, tm, LANES), lambda i, j: (j, i, 0))
        out_shape = (n // LANES, t, LANES)
    in_specs = [pl.BlockSpec((tm, k), lambda i, j: (i, 0)),
                pl.BlockSpec((None, k, tn), lambda i, j: (layer, 0, col0 // tn + j))]
    operands = [x, w]
    if col_scale is not None:
        in_specs.append(pl.BlockSpec((1, tn), lambda i, j: (0, j)))
        operands.append(col_scale)
    return pl.pallas_call(
        functools.partial(_proj_kernel, transpose_out=transpose_out),
        grid=(t // tm, n // tn),
        in_specs=in_specs,
        out_specs=out_spec,
        out_shape=jax.ShapeDtypeStruct(out_shape, out_dtype),
        compiler_params=_params("parallel", "parallel"),
        name=name,
    )(*operands)


def _log_sigmoid(z):
    return jnp.minimum(z, 0.0) - jnp.log1p(jnp.exp(-jnp.abs(z)))


def _split3(x):
    x1 = x.astype(jnp.bfloat16)
    r = x - x1.astype(jnp.float32)
    x2 = r.astype(jnp.bfloat16)
    x3 = (r - x2.astype(jnp.float32)).astype(jnp.bfloat16)
    return x1, x2, x3


def _decay_kernel(ff_ref, bf_ref, tri_ref, c_ref, *, chunk):
    s = ff_ref.shape[1]
    tri = tri_ref[...]
    carry = jnp.zeros((1, LANES), jnp.float32)
    for c in range(s // chunk):
        lf = _log_sigmoid(ff_ref[0, c * chunk:(c + 1) * chunk, :] + bf_ref[...])
        x1, x2, x3 = _split3(lf)
        cs = (jnp.dot(tri, x1, preferred_element_type=jnp.float32)
              + jnp.dot(tri, x2, preferred_element_type=jnp.float32)
              + jnp.dot(tri, x3, preferred_element_type=jnp.float32)) + carry
        c_ref[0, c * chunk:(c + 1) * chunk, :] = cs
        carry = cs[chunk - 1:chunk, :]


def _forget_prefix(gates, b_f_row, batch, seq, chunk):
    tri = (lax.broadcasted_iota(jnp.int32, (chunk, chunk), 0)
           >= lax.broadcasted_iota(jnp.int32, (chunk, chunk), 1)).astype(jnp.bfloat16)
    return pl.pallas_call(
        functools.partial(_decay_kernel, chunk=chunk),
        grid=(batch,),
        in_specs=[
            pl.BlockSpec((1, seq, LANES), lambda b: (GATE_FF, b, 0)),
            pl.BlockSpec((1, LANES), lambda b: (0, 0)),
            pl.BlockSpec((chunk, chunk), lambda b: (0, 0)),
        ],
        out_specs=pl.BlockSpec((1, seq, LANES), lambda b: (b, 0, 0)),
        out_shape=jax.ShapeDtypeStruct((batch, seq, LANES), jnp.float32),
        compiler_params=_params("parallel"),
        name="forget_prefix",
    )(gates, b_f_row, tri)


def _kq(k, q):
    return lax.dot_general(k, q, (((1,), (1,)), ((), ())), preferred_element_type=jnp.float32)


def _silu(g):
    return g * (1.0 / (1.0 + jnp.exp(-g)))


def _lane_tiles(a, tile):
    n = a.shape[1] // LANES
    return jnp.concatenate([tile(a[:, i * LANES:(i + 1) * LANES]) for i in range(n)], axis=1)


def _gather_lanes(x, ranges):
    if ranges is None:
        return x
    return jnp.concatenate([x[:, a:b] for a, b in ranges], axis=1)


def _scatter_lanes(x, new, ranges):
    if ranges is None:
        return new
    pieces, pos, off = [], 0, 0
    for a, b in ranges:
        if a > pos:
            pieces.append(x[:, pos:a])
        pieces.append(new[:, off:off + b - a])
        off += b - a
        pos = b
    if pos < x.shape[1]:
        pieces.append(x[:, pos:])
    return jnp.concatenate(pieces, axis=1)


ONES_ROWS = 16


def _with_ones_rows(vt):
    ones = (lax.broadcasted_iota(jnp.int32, (ONES_ROWS, vt.shape[1]), 0) == 0).astype(vt.dtype)
    return jnp.concatenate([vt, ones], axis=0)


def _softmax_steps(sts, vts, carry, ranges=None, mask=None):
    stats = []
    for st, (m, _) in zip(sts, carry):
        if mask is not None:
            st = jnp.where(mask, st, NEG_BIG)
        m_old = _gather_lanes(m, ranges)
        m_new = jnp.maximum(m_old, jnp.max(st, axis=0, keepdims=True))
        stats.append((m_new, jnp.exp2(m_old - m_new), jnp.exp2(st - m_new)))
    out = []
    for vt, (m, acc), (m_new, alpha, p) in zip(vts, carry, stats):
        acc_new = alpha * _gather_lanes(acc, ranges) + jnp.dot(
            vt, p.astype(vt.dtype), preferred_element_type=jnp.float32)
        out.append((_scatter_lanes(m, m_new, ranges), _scatter_lanes(acc, acc_new, ranges)))
    return tuple(out)


def _normalised(acc):
    return acc[0:HEAD_DIM] * (1.0 / acc[HEAD_DIM:HEAD_DIM + 1])


def _attn_specs(q0, k0, v0, g0, seq, heads):
    blk = lambda base: pl.BlockSpec((heads, seq, LANES), lambda b, hg: (base // heads + hg, b, 0))
    vt = pl.BlockSpec((heads, LANES, seq), lambda b, hg: (v0 // heads + hg, 0, b))
    return blk(q0), blk(k0), vt, blk(g0)


def _causal(tk, width, strict=False):
    rows = lax.broadcasted_iota(jnp.int32, (tk, width), 0)
    cols = lax.broadcasted_iota(jnp.int32, (tk, width), 1)
    return cols > rows if strict else cols >= rows


def _fox_kernel(q_ref, k_ref, vt_ref, g_ref, c_ref, o_ref, bias_ref, vta_ref, *, tq, tk):
    heads, seq = q_ref.shape[0], q_ref.shape[1]
    ratio = tq // tk
    lane = lax.broadcasted_iota(jnp.int32, (seq, LANES), 1)
    for g in range(heads):
        h = pl.program_id(1) * heads + g
        col = jnp.sum(jnp.where(lane == h, c_ref[0], 0.0), axis=1, keepdims=True)
        bias_ref[g] = jnp.broadcast_to(col * (-LOG2E), (seq, LANES))
        vta_ref[g] = _with_ones_rows(vt_ref[g])

    for qi in range(seq // tq):
        r0 = qi * tq
        qs = [q_ref[g, r0:r0 + tq, :] for g in range(heads)]

        def scores(c0, lane0):
            return [_lane_tiles(_kq(k_ref[g, pl.ds(c0, tk), :], qs[g][lane0:]),
                                lambda a, g=g: a + bias_ref[g, pl.ds(c0, tk), :])
                    for g in range(heads)]

        def values(c0):
            return [vta_ref[g, :, pl.ds(c0, tk)] for g in range(heads)]

        def body(j, carry):
            c0 = pl.multiple_of(j * tk, tk)
            return _softmax_steps(scores(c0, 0), values(c0), carry)

        init = (jnp.full((1, tq), NEG_BIG, jnp.float32),
                jnp.zeros((HEAD_DIM + ONES_ROWS, tq), jnp.float32))
        carry = lax.fori_loop(0, qi * ratio, body, (init,) * heads)
        for d in range(ratio):
            ranges = [(d * tk, tq)] if d else None
            carry = _softmax_steps(scores(r0 + d * tk, d * tk), values(r0 + d * tk), carry, ranges,
                                   _causal(tk, tq - d * tk))
        for g in range(heads):
            out = _normalised(carry[g][1]).T * _silu(g_ref[g, r0:r0 + tq, :])
            o_ref[r0:r0 + tq, g * LANES:(g + 1) * LANES] = out.astype(o_ref.dtype)


def _fox_attention(qk, vt, gates, c, batch, seq, tq, tk, heads):
    q_s, k_s, v_s, g_s = _attn_specs(FOX_Q0, FOX_K0, FOX_V0, GATE_FOX0, seq, heads)
    return pl.pallas_call(
        functools.partial(_fox_kernel, tq=tq, tk=tk),
        grid=(batch, FOX_HEADS // heads),
        in_specs=[q_s, k_s, v_s, g_s, pl.BlockSpec((1, seq, LANES), lambda b, hg: (b, 0, 0))],
        out_specs=pl.BlockSpec((seq, heads * LANES), lambda b, hg: (b, hg)),
        out_shape=jax.ShapeDtypeStruct((batch * seq, FOX_W), jnp.bfloat16),
        scratch_shapes=[pltpu.VMEM((heads, seq, LANES), jnp.float32),
                        pltpu.VMEM((heads, HEAD_DIM + ONES_ROWS, seq), jnp.bfloat16)],
        compiler_params=_params("parallel", "parallel"),
        name="fox_attention",
    )(qk, qk, vt, gates, c)


def _sb_blocks(qs, ks, vts, ut, states, ranges, strict):
    tk = ks[0].shape[0]
    sub_keys = ut.shape[0]
    zs = [_kq(k, q) for k, q in zip(ks, qs)]
    parts = []
    for z in zs:
        neg_abs = lax.bitcast_convert_type(
            lax.bitcast_convert_type(z, jnp.uint32) | jnp.uint32(0x80000000), jnp.float32)
        sp = jnp.log2(1.0 + jnp.exp2(neg_abs))
        log_beta = jnp.minimum(z, 0.0) - sp
        log_keep = log_beta - z
        if strict is not None:
            log_keep = jnp.where(strict, log_keep, 0.0)
        parts.append((log_beta, log_keep, log_keep.astype(jnp.bfloat16)))
    subs = []
    for _, _, terms in parts:
        subs.append([jnp.dot(ut, terms[i:i + sub_keys], preferred_element_type=jnp.float32)
                     for i in range(0, tk, sub_keys)])
    ws, carries = [], []
    for (log_beta, log_keep, _), sub, (carry_all, _) in zip(parts, subs, states):
        carry = _gather_lanes(carry_all, ranges)
        laters = [None] * len(sub)
        for n in reversed(range(len(sub))):
            laters[n] = sub[n] + carry
            carry = carry + sub[n][0:1] + log_keep[n * sub_keys:n * sub_keys + 1]
        w = jnp.exp2(log_beta + jnp.concatenate(laters, axis=0))
        if strict is not None:
            w = jnp.where(strict, w, 0.0)
        ws.append(w)
        carries.append(_scatter_lanes(carry_all, carry, ranges))
    out = []
    for w, vt, carry, (_, acc) in zip(ws, vts, carries, states):
        acc_new = _gather_lanes(acc, ranges) + jnp.dot(vt, w.astype(vt.dtype),
                                                       preferred_element_type=jnp.float32)
        out.append((carry, _scatter_lanes(acc, acc_new, ranges)))
    return tuple(out)


def _sb_kernel(q_ref, k_ref, vt_ref, g_ref, ut_ref, o_ref, *, tq, tk):
    heads, seq = q_ref.shape[0], q_ref.shape[1]
    ratio = tq // tk
    ut = ut_ref[...]

    for qi in range(seq // tq):
        r0 = qi * tq
        qs = [q_ref[g, r0:r0 + tq, :] for g in range(heads)]

        def block(c0, lane0, state, ranges, mask):
            return _sb_blocks([q[lane0:] for q in qs],
                              [k_ref[g, pl.ds(c0, tk), :] for g in range(heads)],
                              [vt_ref[g, :, pl.ds(c0, tk)] for g in range(heads)], ut, state, ranges, mask)

        init = (jnp.zeros((1, tq), jnp.float32), jnp.zeros((HEAD_DIM, tq), jnp.float32))
        state = (init,) * heads
        for d in reversed(range(ratio)):
            state = block(r0 + d * tk, d * tk, state, [(d * tk, tq)] if d else None,
                          _causal(tk, tq - d * tk, strict=True))
        n_full = qi * ratio
        state = lax.fori_loop(
            0, n_full,
            lambda i, st: block(pl.multiple_of((n_full - 1 - i) * tk, tk), 0, st, None, None), state)
        for g in range(heads):
            out = state[g][1].T * _silu(g_ref[g, r0:r0 + tq, :])
            o_ref[r0:r0 + tq, g * LANES:(g + 1) * LANES] = out.astype(o_ref.dtype)


def _sb_attention(qk, vt, gates, batch, seq, tq, tk, heads):
    q_s, k_s, v_s, g_s = _attn_specs(SB_Q0, SB_K0, SB_V0, GATE_SB0, seq, heads)
    sub = min(SB_SUB, tk)
    ut = (lax.broadcasted_iota(jnp.int32, (sub, sub), 1)
          > lax.broadcasted_iota(jnp.int32, (sub, sub), 0)).astype(jnp.bfloat16)
    return pl.pallas_call(
        functools.partial(_sb_kernel, tq=tq, tk=tk),
        grid=(batch, SB_HEADS // heads),
        in_specs=[q_s, k_s, v_s, g_s, pl.BlockSpec((sub, sub), lambda b, hg: (0, 0))],
        out_specs=pl.BlockSpec((seq, heads * LANES), lambda b, hg: (b, hg)),
        out_shape=jax.ShapeDtypeStruct((batch * seq, SB_W), jnp.bfloat16),
        compiler_params=_params("parallel", "parallel"),
        name="sb_attention",
    )(qk, qk, vt, gates, ut)


def _diff_kernel(lam_ref, gsub_ref, q_ref, k_ref, vt_ref, g_ref, o_ref, vta_ref, *, tq, tk, lam_init):
    heads, seq = q_ref.shape[0], q_ref.shape[1]
    ratio = tq // tk
    lp = lam_ref[...]
    lam = (jnp.exp(jnp.sum(lp[0:1] * lp[1:2], axis=1, keepdims=True))
           - jnp.exp(jnp.sum(lp[2:3] * lp[3:4], axis=1, keepdims=True)) + lam_init)
    key = lax.broadcasted_iota(jnp.int32, (tk, LANES), 0).astype(jnp.float32)
    ramps = []
    for g in range(heads):
        h = pl.program_id(1) * heads + g
        expo = jnp.full((tk, LANES), 127 - (8 // DIFF_HEADS) * (h + 1), jnp.int32)
        slope = lax.bitcast_convert_type(expo << 23, jnp.float32) * LOG2E
        ramps.append((slope, slope * key))
        vta_ref[g] = _with_ones_rows(vt_ref[g])
    lane = lax.broadcasted_iota(jnp.int32, (tq, HEAD_DIM), 1)

    for qi in range(seq // tq):
        r0 = qi * tq
        q1s, q2s = [], []
        for g in range(heads):
            q = q_ref[g, r0:r0 + tq, :]
            zero = jnp.zeros_like(q)
            q1s.append(jnp.where(lane < DIFF_QK_DIM, q, zero))
            q2s.append(jnp.where(lane >= DIFF_QK_DIM, q, zero))

        def scores(c0, lane0):
            out = []
            for g in range(heads):
                slope, ramp = ramps[g]
                bias = ramp + slope * jnp.asarray(c0 - r0, jnp.float32)
                qq = jnp.concatenate([q1s[g][lane0:], q2s[g][lane0:]], axis=0)
                out.append(_lane_tiles(_kq(k_ref[g, pl.ds(c0, tk), :], qq), lambda a, b=bias: a + b))
            return out

        def values(c0):
            return [vta_ref[g, :, pl.ds(c0, tk)] for g in range(heads)]

        def body(j, carry):
            c0 = pl.multiple_of(j * tk, tk)
            return _softmax_steps(scores(c0, 0), values(c0), carry)

        init = (jnp.full((1, 2 * tq), NEG_BIG, jnp.float32),
                jnp.zeros((HEAD_DIM + ONES_ROWS, 2 * tq), jnp.float32))
        carry = lax.fori_loop(0, qi * ratio, body, (init,) * heads)
        for d in range(ratio):
            ranges = [(d * tk, tq), (tq + d * tk, 2 * tq)] if d else None
            half = _causal(tk, tq - d * tk)
            carry = _softmax_steps(scores(r0 + d * tk, d * tk), values(r0 + d * tk), carry, ranges,
                                   jnp.concatenate([half, half], axis=1))
        for g in range(heads):
            o = _normalised(carry[g][1])
            o = o[:, 0:tq] - lam * o[:, tq:2 * tq]
            o = o * lax.rsqrt(jnp.mean(o * o, axis=0, keepdims=True) + SUBLN_EPS)
            o = o.T * gsub_ref[...] * (1.0 - lam_init)
            out = o * _silu(g_ref[g, r0:r0 + tq, :])
            o_ref[r0:r0 + tq, g * LANES:(g + 1) * LANES] = out.astype(o_ref.dtype)


def _diff_attention(qk, vt, gates, lam_p, subln_row, batch, seq, tq, tk, lam_init, heads):
    q_s, k_s, v_s, g_s = _attn_specs(DIFF_Q0, DIFF_K0, DIFF_V0, GATE_DIFF0, seq, heads)
    return pl.pallas_call(
        functools.partial(_diff_kernel, tq=tq, tk=tk, lam_init=lam_init),
        grid=(batch, DIFF_HEADS // heads),
        in_specs=[pl.BlockSpec(lam_p.shape, lambda b, hg: (0, 0)),
                  pl.BlockSpec((1, HEAD_DIM), lambda b, hg: (0, 0)),
                  q_s, k_s, v_s, g_s],
        out_specs=pl.BlockSpec((seq, heads * LANES), lambda b, hg: (b, hg)),
        out_shape=jax.ShapeDtypeStruct((batch * seq, DIFF_W), jnp.bfloat16),
        scratch_shapes=[pltpu.VMEM((heads, HEAD_DIM + ONES_ROWS, seq), jnp.bfloat16)],
        compiler_params=_params("parallel", "parallel"),
        name="diff_attention",
    )(lam_p, subln_row, qk, qk, vt, gates)


def _merge_kernel(of_ref, os_ref, od_ref, wf_ref, ws_ref, wd_ref, x_ref, g_ref, b_ref, y_ref, yb_ref):
    y = (jnp.dot(of_ref[...], wf_ref[...], preferred_element_type=jnp.float32)
         + jnp.dot(os_ref[...], ws_ref[...], preferred_element_type=jnp.float32)
         + jnp.dot(od_ref[...], wd_ref[...], preferred_element_type=jnp.float32))
    z = DEEPNORM_ALPHA * x_ref[...] + y
    mu = jnp.mean(z, axis=1, keepdims=True)
    zc = z - mu
    var = jnp.mean(zc * zc, axis=1, keepdims=True)
    out = zc * lax.rsqrt(var + LN_EPS) * g_ref[...] + b_ref[...]
    y_ref[...] = out
    yb_ref[...] = out.astype(yb_ref.dtype)


def _merge(o_fox, o_sb, o_diff, w_out, layer, x, ln_g, ln_b, tm):
    t, d = x.shape
    row = lambda w: pl.BlockSpec((tm, w), lambda i: (i, 0))
    w_rows = lambda rows, start: pl.BlockSpec((None, rows, d), lambda i: (layer, start // rows, 0))
    vec = lambda: pl.BlockSpec((None, 1, d), lambda i: (layer, 0, 0))
    return pl.pallas_call(
        _merge_kernel,
        grid=(t // tm,),
        in_specs=[row(FOX_W), row(SB_W), row(DIFF_W),
                  w_rows(FOX_W, 0), w_rows(SB_W, FOX_W), w_rows(DIFF_W, FOX_W + SB_W),
                  row(d), vec(), vec()],
        out_specs=[row(d), row(d)],
        out_shape=[jax.ShapeDtypeStruct((t, d), jnp.float32),
                   jax.ShapeDtypeStruct((t, d), jnp.bfloat16)],
        compiler_params=_params("parallel"),
        name="merge_layernorm",
    )(o_fox, o_sb, o_diff, w_out, w_out, w_out, x, ln_g, ln_b)


PREP_UNIT = 256
QK_COLS = 2 * (FOX_W + SB_W + DIFF_W)
V_COLS = FOX_W + SB_W + DIFF_W
GATE_COLS = FOX_W + SB_W + DIFF_W


def _cast_kernel(w_ref, o_ref):
    o_ref[...] = w_ref[...].astype(o_ref.dtype)


def _permuted_in_weights(w_in):
    depth, k, _ = w_in.shape
    sizes = (FOX_W,) * 4 + (SB_W,) * 4 + (DIFF_W,) * 4
    starts = [sum(sizes[:i]) for i in range(len(sizes))]
    fq, fk, fv, fg, sq, sk, sv, sg, dq, dk, dv, dg = range(12)
    order = [fq, fk, sq, sk, dq, dk, fv, sv, dv, fg, sg, dg]
    steps, dst, shift = [], 0, 0
    for grp in order:
        new_shift = (starts[grp] - dst) // PREP_UNIT
        steps.append((dst // PREP_UNIT, new_shift - shift))
        shift = new_shift
        dst += sizes[grp]
    n_units = dst // PREP_UNIT

    def src_unit(j):
        return j + sum(jnp.where(j >= first, delta, 0) for first, delta in steps)

    return pl.pallas_call(
        _cast_kernel,
        grid=(depth, n_units),
        in_specs=[pl.BlockSpec((None, k, PREP_UNIT), lambda l, j: (l, 0, src_unit(j)))],
        out_specs=pl.BlockSpec((None, k, PREP_UNIT), lambda l, j: (l, 0, j)),
        out_shape=jax.ShapeDtypeStruct((depth, k, n_units * PREP_UNIT), jnp.bfloat16),
        compiler_params=_params("parallel", "parallel"),
        name="permute_in_weights",
    )(w_in)


def _forget_in_weights(w_in):
    ff = w_in[:, :, QK_COLS + V_COLS + GATE_COLS:]
    return jnp.pad(ff, ((0, 0), (0, 0), (0, LANES - FOX_HEADS))).astype(jnp.bfloat16)


def _qk_col_scale():
    one = lambda n: jnp.ones((n,), jnp.float32)
    full = lambda n, v: jnp.full((n,), v, jnp.float32)
    return jnp.concatenate([
        full(FOX_W, HEAD_DIM ** -0.5 * LOG2E), one(FOX_W),
        full(SB_W, HEAD_DIM ** -0.5 * LOG2E), one(SB_W),
        full(DIFF_W, DIFF_QK_DIM ** -0.5 * LOG2E), one(DIFF_W)])[None, :]


def kernel(x, w_in, b_f, diff_lambda, diff_subln_g, w_out, ln_g, ln_b):
    batch, seq, d_model = x.shape
    depth = w_in.shape[0]
    t = batch * seq
    tk = min(seq, KEY_BLOCK)
    tq = min(seq, ATTN_TQ)

    w_in_b = w_in.astype(jnp.bfloat16)
    w_perm = _permuted_in_weights(w_in_b)
    w_ff = _forget_in_weights(w_in_b)
    w_out_b = w_out.astype(jnp.bfloat16)
    qk_scale = _qk_col_scale()
    b_f_rows = jnp.pad(b_f, ((0, 0), (0, LANES - FOX_HEADS)))
    ln_g3 = ln_g.reshape(depth, 1, d_model)
    ln_b3 = ln_b.reshape(depth, 1, d_model)

    xf = x.reshape(t, d_model)
    xb = xf
    for l in range(depth):
        lam_init = 0.8 - 0.6 * math.exp(-0.3 * l)
        tm = min(t, PROJ_ROWS_F32 if xb.dtype == jnp.float32 else PROJ_ROWS_BF16)
        qk = _project(xb, w_perm, l, 0, QK_COLS, qk_scale, jnp.bfloat16, tm, 1024, "qk_projection")
        vt = _project(xb, w_perm, l, QK_COLS, V_COLS, None, jnp.bfloat16, tm, 1024, "v_projection",
                      transpose_out=True)
        gates = _project_gates(xb, w_perm, w_ff, l, QK_COLS + V_COLS, min(t, 512))
        c = _forget_prefix(gates, b_f_rows[l:l + 1], batch, seq, tk)
        o_fox = _fox_attention(qk, vt, gates, c, batch, seq, tq, tk, FOX_GROUP)
        o_sb = _sb_attention(qk, vt, gates, batch, seq, tq, tk, SB_GROUP)
        o_diff = _diff_attention(qk, vt, gates, diff_lambda[l], diff_subln_g[l:l + 1], batch, seq, tq, tk,
                                 lam_init, DIFF_GROUP)
        xf, xb = _merge(o_fox, o_sb, o_diff, w_out_b, l, xf, ln_g3, ln_b3, min(t, 512))
    return xf.reshape(batch, seq, d_model)
```

```python
import functools
import math

import jax
import jax.numpy as jnp
from jax import lax
from jax.experimental import pallas as pl
from jax.experimental.pallas import tpu as pltpu

HEAD_DIM = 128
FOX_HEADS = 6
SB_HEADS = 6
DIFF_HEADS = 4
DIFF_QK_DIM = HEAD_DIM // 2
FOX_W = FOX_HEADS * HEAD_DIM
SB_W = SB_HEADS * HEAD_DIM
DIFF_W = DIFF_HEADS * HEAD_DIM
DEPTH_FOR_DEEPNORM = 4
DEEPNORM_ALPHA = (2 * DEPTH_FOR_DEEPNORM) ** 0.25
LN_EPS = 1e-5
SUBLN_EPS = 1e-5
NEG_BIG = -1e30
LOG2E = math.log2(math.e)

LANES = 128
VMEM_LIMIT_BYTES = 56 * 1024 * 1024

FOX_Q0, FOX_K0 = 0, FOX_HEADS
SB_Q0, SB_K0 = 2 * FOX_HEADS, 2 * FOX_HEADS + SB_HEADS
DIFF_Q0, DIFF_K0 = 2 * (FOX_HEADS + SB_HEADS), 2 * (FOX_HEADS + SB_HEADS) + DIFF_HEADS
FOX_V0, SB_V0, DIFF_V0 = 0, FOX_HEADS, FOX_HEADS + SB_HEADS
GATE_FOX0, GATE_SB0, GATE_DIFF0 = 0, FOX_HEADS, FOX_HEADS + SB_HEADS
GATE_FF = FOX_HEADS + SB_HEADS + DIFF_HEADS
GATE_BLOCKS = GATE_FF + 1
FOX_GROUP, SB_GROUP, DIFF_GROUP = 3, 2, 2
ATTN_TQ = 2048
KEY_BLOCK = 256
SB_SUB = 256
PROJ_ROWS_F32, PROJ_ROWS_BF16 = 1024, 2048
PROJ_COLS = 1024
WIDE_ROWS = 512


def _params(*semantics):
    return pltpu.CompilerParams(dimension_semantics=semantics, vmem_limit_bytes=VMEM_LIMIT_BYTES)


def _proj_kernel(x_ref, w_ref, *rest, transpose_out):
    o_ref = rest[-1]
    acc = jnp.dot(x_ref[...].astype(w_ref.dtype), w_ref[...], preferred_element_type=jnp.float32)
    if len(rest) == 2:
        acc = acc * rest[0][...]
    for c in range(o_ref.shape[0]):
        blk = acc[:, c * LANES:(c + 1) * LANES]
        o_ref[c] = (blk.T if transpose_out else blk).astype(o_ref.dtype)


def _gate_proj_kernel(x_ref, w_ref, wf_ref, o_ref):
    x = x_ref[...].astype(w_ref.dtype)
    acc = jnp.dot(x, w_ref[...], preferred_element_type=jnp.float32)
    for c in range(GATE_FF):
        o_ref[c] = acc[:, c * LANES:(c + 1) * LANES]
    o_ref[GATE_FF] = jnp.dot(x, wf_ref[...], preferred_element_type=jnp.float32)


def _project_gates(x, w, w_ff, layer, col0, tm):
    t, k = x.shape
    assert t % tm == 0 and col0 % GATE_COLS == 0, (t, tm, col0)
    return pl.pallas_call(
        _gate_proj_kernel,
        grid=(t // tm,),
        in_specs=[pl.BlockSpec((tm, k), lambda i: (i, 0)),
                  pl.BlockSpec((None, k, GATE_COLS), lambda i: (layer, 0, col0 // GATE_COLS)),
                  pl.BlockSpec((None, k, LANES), lambda i: (layer, 0, 0))],
        out_specs=pl.BlockSpec((GATE_BLOCKS, tm, LANES), lambda i: (0, i, 0)),
        out_shape=jax.ShapeDtypeStruct((GATE_BLOCKS, t, LANES), jnp.float32),
        compiler_params=_params("parallel"),
        name="gate_projection",
    )(x, w, w_ff)


def _project(x, w, layer, col0, n, col_scale, out_dtype, tm, tn, name, transpose_out=False):
    t, k = x.shape
    assert t % tm == 0 and n % tn == 0 and col0 % tn == 0, (t, tm, n, tn, col0)
    if transpose_out:
        out_spec = pl.BlockSpec((tn // LANES, LANES, tm), lambda i, j: (j, 0, i))
        out_shape = (n // LANES, LANES, t)
    else:
        out_spec = pl.BlockSpec((tn // LANES, tm, LANES), lambda i, j: (j, i, 0))
        out_shape = (n // LANES, t, LANES)
    in_specs = [pl.BlockSpec((tm, k), lambda i, j: (i, 0)),
                pl.BlockSpec((None, k, tn), lambda i, j: (layer, 0, col0 // tn + j))]
    operands = [x, w]
    if col_scale is not None:
        in_specs.append(pl.BlockSpec((1, tn), lambda i, j: (0, j)))
        operands.append(col_scale)
    return pl.pallas_call(
        functools.partial(_proj_kernel, transpose_out=transpose_out),
        grid=(t // tm, n // tn),
        in_specs=in_specs,
        out_specs=out_spec,
        out_shape=jax.ShapeDtypeStruct(out_shape, out_dtype),
        compiler_params=_params("parallel", "parallel"),
        name=name,
    )(*operands)


def _log_sigmoid(z):
    return jnp.minimum(z, 0.0) - jnp.log1p(jnp.exp(-jnp.abs(z)))


def _split3(x):
    x1 = x.astype(jnp.bfloat16)
    r = x - x1.astype(jnp.float32)
    x2 = r.astype(jnp.bfloat16)
    x3 = (r - x2.astype(jnp.float32)).astype(jnp.bfloat16)
    return x1, x2, x3


def _decay_kernel(ff_ref, bf_ref, tri_ref, c_ref, *, chunk):
    s = ff_ref.shape[1]
    tri = tri_ref[...]
    carry = jnp.zeros((1, LANES), jnp.float32)
    for c in range(s // chunk):
        lf = _log_sigmoid(ff_ref[0, c * chunk:(c + 1) * chunk, :] + bf_ref[...])
        x1, x2, x3 = _split3(lf)
        cs = (jnp.dot(tri, x1, preferred_element_type=jnp.float32)
              + jnp.dot(tri, x2, preferred_element_type=jnp.float32)
              + jnp.dot(tri, x3, preferred_element_type=jnp.float32)) + carry
        c_ref[0, c * chunk:(c + 1) * chunk, :] = cs
        carry = cs[chunk - 1:chunk, :]


def _forget_prefix(gates, b_f_row, batch, seq, chunk):
    tri = (lax.broadcasted_iota(jnp.int32, (chunk, chunk), 0)
           >= lax.broadcasted_iota(jnp.int32, (chunk, chunk), 1)).astype(jnp.bfloat16)
    return pl.pallas_call(
        functools.partial(_decay_kernel, chunk=chunk),
        grid=(batch,),
        in_specs=[
            pl.BlockSpec((1, seq, LANES), lambda b: (GATE_FF, b, 0)),
            pl.BlockSpec((1, LANES), lambda b: (0, 0)),
            pl.BlockSpec((chunk, chunk), lambda b: (0, 0)),
        ],
        out_specs=pl.BlockSpec((1, seq, LANES), lambda b: (b, 0, 0)),
        out_shape=jax.ShapeDtypeStruct((batch, seq, LANES), jnp.float32),
        compiler_params=_params("parallel"),
        name="forget_prefix",
    )(gates, b_f_row, tri)


def _kq(k, q):
    return lax.dot_general(k, q, (((1,), (1,)), ((), ())), preferred_element_type=jnp.float32)


def _silu(g):
    return g * (1.0 / (1.0 + jnp.exp(-g)))


def _lane_tiles(a, tile):
    n = a.shape[1] // LANES
    return jnp.concatenate([tile(a[:, i * LANES:(i + 1) * LANES]) for i in range(n)], axis=1)


def _gather_lanes(x, ranges):
    if ranges is None:
        return x
    return jnp.concatenate([x[:, a:b] for a, b in ranges], axis=1)


def _scatter_lanes(x, new, ranges):
    if ranges is None:
        return new
    pieces, pos, off = [], 0, 0
    for a, b in ranges:
        if a > pos:
            pieces.append(x[:, pos:a])
        pieces.append(new[:, off:off + b - a])
        off += b - a
        pos = b
    if pos < x.shape[1]:
        pieces.append(x[:, pos:])
    return jnp.concatenate(pieces, axis=1)


ONES_ROWS = 16


def _with_ones_rows(vt):
    ones = (lax.broadcasted_iota(jnp.int32, (ONES_ROWS, vt.shape[1]), 0) == 0).astype(vt.dtype)
    return jnp.concatenate([vt, ones], axis=0)


def _softmax_steps(sts, vts, carry, ranges=None, mask=None):
    stats = []
    for st, (m, _) in zip(sts, carry):
        if mask is not None:
            st = jnp.where(mask, st, NEG_BIG)
        m_old = _gather_lanes(m, ranges)
        m_new = jnp.maximum(m_old, jnp.max(st, axis=0, keepdims=True))
        stats.append((m_new, jnp.exp2(m_old - m_new), jnp.exp2(st - m_new)))
    out = []
    for vt, (m, acc), (m_new, alpha, p) in zip(vts, carry, stats):
        acc_new = alpha * _gather_lanes(acc, ranges) + jnp.dot(
            vt, p.astype(vt.dtype), preferred_element_type=jnp.float32)
        out.append((_scatter_lanes(m, m_new, ranges), _scatter_lanes(acc, acc_new, ranges)))
    return tuple(out)


def _normalised(acc):
    return acc[0:HEAD_DIM] * (1.0 / acc[HEAD_DIM:HEAD_DIM + 1])


def _attn_specs(q0, k0, v0, g0, seq, heads):
    blk = lambda base: pl.BlockSpec((heads, seq, LANES), lambda b, hg: (base // heads + hg, b, 0))
    vt = pl.BlockSpec((heads, LANES, seq), lambda b, hg: (v0 // heads + hg, 0, b))
    return blk(q0), blk(k0), vt, blk(g0)


def _causal(tk, width, strict=False):
    rows = lax.broadcasted_iota(jnp.int32, (tk, width), 0)
    cols = lax.broadcasted_iota(jnp.int32, (tk, width), 1)
    return cols > rows if strict else cols >= rows


def _fox_kernel(q_ref, k_ref, vt_ref, g_ref, c_ref, o_ref, bias_ref, vta_ref, *, tq, tk):
    heads, seq = q_ref.shape[0], q_ref.shape[1]
    ratio = tq // tk
    lane = lax.broadcasted_iota(jnp.int32, (seq, LANES), 1)
    for g in range(heads):
        h = pl.program_id(1) * heads + g
        col = jnp.sum(jnp.where(lane == h, c_ref[0], 0.0), axis=1, keepdims=True)
        bias_ref[g] = jnp.broadcast_to(col * (-LOG2E), (seq, LANES))
        vta_ref[g] = _with_ones_rows(vt_ref[g])

    for qi in range(seq // tq):
        r0 = qi * tq
        qs = [q_ref[g, r0:r0 + tq, :] for g in range(heads)]

        def scores(c0, lane0):
            return [_lane_tiles(_kq(k_ref[g, pl.ds(c0, tk), :], qs[g][lane0:]),
                                lambda a, g=g: a + bias_ref[g, pl.ds(c0, tk), :])
                    for g in range(heads)]

        def values(c0):
            return [vta_ref[g, :, pl.ds(c0, tk)] for g in range(heads)]

        def body(j, carry):
            c0 = pl.multiple_of(j * tk, tk)
            return _softmax_steps(scores(c0, 0), values(c0), carry)

        init = (jnp.full((1, tq), NEG_BIG, jnp.float32),
                jnp.zeros((HEAD_DIM + ONES_ROWS, tq), jnp.float32))
        carry = lax.fori_loop(0, qi * ratio, body, (init,) * heads)
        for d in range(ratio):
            ranges = [(d * tk, tq)] if d else None
            carry = _softmax_steps(scores(r0 + d * tk, d * tk), values(r0 + d * tk), carry, ranges,
                                   _causal(tk, tq - d * tk))
        for g in range(heads):
            out = _normalised(carry[g][1]).T * _silu(g_ref[g, r0:r0 + tq, :])
            o_ref[r0:r0 + tq, g * LANES:(g + 1) * LANES] = out.astype(o_ref.dtype)


def _fox_attention(qk, vt, gates, c, batch, seq, tq, tk, heads):
    q_s, k_s, v_s, g_s = _attn_specs(FOX_Q0, FOX_K0, FOX_V0, GATE_FOX0, seq, heads)
    return pl.pallas_call(
        functools.partial(_fox_kernel, tq=tq, tk=tk),
        grid=(batch, FOX_HEADS // heads),
        in_specs=[q_s, k_s, v_s, g_s, pl.BlockSpec((1, seq, LANES), lambda b, hg: (b, 0, 0))],
        out_specs=pl.BlockSpec((seq, heads * LANES), lambda b, hg: (b, hg)),
        out_shape=jax.ShapeDtypeStruct((batch * seq, FOX_W), jnp.bfloat16),
        scratch_shapes=[pltpu.VMEM((heads, seq, LANES), jnp.float32),
                        pltpu.VMEM((heads, HEAD_DIM + ONES_ROWS, seq), jnp.bfloat16)],
        compiler_params=_params("parallel", "parallel"),
        name="fox_attention",
    )(qk, qk, vt, gates, c)


def _sb_blocks(qs, ks, vts, ut, states, ranges, strict):
    tk = ks[0].shape[0]
    sub_keys = ut.shape[0]
    zs = [_kq(k, q) for k, q in zip(ks, qs)]
    parts = []
    for z in zs:
        neg_abs = lax.bitcast_convert_type(
            lax.bitcast_convert_type(z, jnp.uint32) | jnp.uint32(0x80000000), jnp.float32)
        sp = jnp.log2(1.0 + jnp.exp2(neg_abs))
        log_beta = jnp.minimum(z, 0.0) - sp
        log_keep = log_beta - z
        if strict is not None:
            log_keep = jnp.where(strict, log_keep, 0.0)
        parts.append((log_beta, log_keep, log_keep.astype(jnp.bfloat16)))
    subs = []
    for _, _, terms in parts:
        subs.append([jnp.dot(ut, terms[i:i + sub_keys], preferred_element_type=jnp.float32)
                     for i in range(0, tk, sub_keys)])
    ws, carries = [], []
    for (log_beta, log_keep, _), sub, (carry_all, _) in zip(parts, subs, states):
        carry = _gather_lanes(carry_all, ranges)
        laters = [None] * len(sub)
        for n in reversed(range(len(sub))):
            laters[n] = sub[n] + carry
            carry = carry + sub[n][0:1] + log_keep[n * sub_keys:n * sub_keys + 1]
        w = jnp.exp2(log_beta + jnp.concatenate(laters, axis=0))
        if strict is not None:
            w = jnp.where(strict, w, 0.0)
        ws.append(w)
        carries.append(_scatter_lanes(carry_all, carry, ranges))
    out = []
    for w, vt, carry, (_, acc) in zip(ws, vts, carries, states):
        acc_new = _gather_lanes(acc, ranges) + jnp.dot(vt, w.astype(vt.dtype),
                                                       preferred_element_type=jnp.float32)
        out.append((carry, _scatter_lanes(acc, acc_new, ranges)))
    return tuple(out)


def _sb_kernel(q_ref, k_ref, vt_ref, g_ref, ut_ref, o_ref, *, tq, tk):
    heads, seq = q_ref.shape[0], q_ref.shape[1]
    ratio = tq // tk
    ut = ut_ref[...]

    for qi in range(seq // tq):
        r0 = qi * tq
        qs = [q_ref[g, r0:r0 + tq, :] for g in range(heads)]

        def block(c0, lane0, state, ranges, mask):
            return _sb_blocks([q[lane0:] for q in qs],
                              [k_ref[g, pl.ds(c0, tk), :] for g in range(heads)],
                              [vt_ref[g, :, pl.ds(c0, tk)] for g in range(heads)], ut, state, ranges, mask)

        init = (jnp.zeros((1, tq), jnp.float32), jnp.zeros((HEAD_DIM, tq), jnp.float32))
        state = (init,) * heads
        for d in reversed(range(ratio)):
            state = block(r0 + d * tk, d * tk, state, [(d * tk, tq)] if d else None,
                          _causal(tk, tq - d * tk, strict=True))
        n_full = qi * ratio
        state = lax.fori_loop(
            0, n_full,
            lambda i, st: block(pl.multiple_of((n_full - 1 - i) * tk, tk), 0, st, None, None), state)
        for g in range(heads):
            out = state[g][1].T * _silu(g_ref[g, r0:r0 + tq, :])
            o_ref[r0:r0 + tq, g * LANES:(g + 1) * LANES] = out.astype(o_ref.dtype)


def _sb_attention(qk, vt, gates, batch, seq, tq, tk, heads):
    q_s, k_s, v_s, g_s = _attn_specs(SB_Q0, SB_K0, SB_V0, GATE_SB0, seq, heads)
    sub = min(SB_SUB, tk)
    ut = (lax.broadcasted_iota(jnp.int32, (sub, sub), 1)
          > lax.broadcasted_iota(jnp.int32, (sub, sub), 0)).astype(jnp.bfloat16)
    return pl.pallas_call(
        functools.partial(_sb_kernel, tq=tq, tk=tk),
        grid=(batch, SB_HEADS // heads),
        in_specs=[q_s, k_s, v_s, g_s, pl.BlockSpec((sub, sub), lambda b, hg: (0, 0))],
        out_specs=pl.BlockSpec((seq, heads * LANES), lambda b, hg: (b, hg)),
        out_shape=jax.ShapeDtypeStruct((batch * seq, SB_W), jnp.bfloat16),
        compiler_params=_params("parallel", "parallel"),
        name="sb_attention",
    )(qk, qk, vt, gates, ut)


def _diff_kernel(lam_ref, gsub_ref, q_ref, k_ref, vt_ref, g_ref, o_ref, vta_ref, *, tq, tk, lam_init):
    heads, seq = q_ref.shape[0], q_ref.shape[1]
    ratio = tq // tk
    lp = lam_ref[...]
    lam = (jnp.exp(jnp.sum(lp[0:1] * lp[1:2], axis=1, keepdims=True))
           - jnp.exp(jnp.sum(lp[2:3] * lp[3:4], axis=1, keepdims=True)) + lam_init)
    key = lax.broadcasted_iota(jnp.int32, (tk, LANES), 0).astype(jnp.float32)
    ramps = []
    for g in range(heads):
        h = pl.program_id(1) * heads + g
        expo = jnp.full((tk, LANES), 127 - (8 // DIFF_HEADS) * (h + 1), jnp.int32)
        slope = lax.bitcast_convert_type(expo << 23, jnp.float32) * LOG2E
        ramps.append((slope, slope * key))
        vta_ref[g] = _with_ones_rows(vt_ref[g])
    lane = lax.broadcasted_iota(jnp.int32, (tq, HEAD_DIM), 1)

    for qi in range(seq // tq):
        r0 = qi * tq
        q1s, q2s = [], []
        for g in range(heads):
            q = q_ref[g, r0:r0 + tq, :]
            zero = jnp.zeros_like(q)
            q1s.append(jnp.where(lane < DIFF_QK_DIM, q, zero))
            q2s.append(jnp.where(lane >= DIFF_QK_DIM, q, zero))

        def scores(c0, lane0):
            out = []
            for g in range(heads):
                slope, ramp = ramps[g]
                bias = ramp + slope * jnp.asarray(c0 - r0, jnp.float32)
                qq = jnp.concatenate([q1s[g][lane0:], q2s[g][lane0:]], axis=0)
                out.append(_lane_tiles(_kq(k_ref[g, pl.ds(c0, tk), :], qq), lambda a, b=bias: a + b))
            return out

        def values(c0):
            return [vta_ref[g, :, pl.ds(c0, tk)] for g in range(heads)]

        def body(j, carry):
            c0 = pl.multiple_of(j * tk, tk)
            return _softmax_steps(scores(c0, 0), values(c0), carry)

        init = (jnp.full((1, 2 * tq), NEG_BIG, jnp.float32),
                jnp.zeros((HEAD_DIM + ONES_ROWS, 2 * tq), jnp.float32))
        carry = lax.fori_loop(0, qi * ratio, body, (init,) * heads)
        for d in range(ratio):
            ranges = [(d * tk, tq), (tq + d * tk, 2 * tq)] if d else None
            half = _causal(tk, tq - d * tk)
            carry = _softmax_steps(scores(r0 + d * tk, d * tk), values(r0 + d * tk), carry, ranges,
                                   jnp.concatenate([half, half], axis=1))
        for g in range(heads):
            o = _normalised(carry[g][1])
            o = o[:, 0:tq] - lam * o[:, tq:2 * tq]
            o = o * lax.rsqrt(jnp.mean(o * o, axis=0, keepdims=True) + SUBLN_EPS)
            o = o.T * gsub_ref[...] * (1.0 - lam_init)
            out = o * _silu(g_ref[g, r0:r0 + tq, :])
            o_ref[r0:r0 + tq, g * LANES:(g + 1) * LANES] = out.astype(o_ref.dtype)


def _diff_attention(qk, vt, gates, lam_p, subln_row, batch, seq, tq, tk, lam_init, heads):
    q_s, k_s, v_s, g_s = _attn_specs(DIFF_Q0, DIFF_K0, DIFF_V0, GATE_DIFF0, seq, heads)
    return pl.pallas_call(
        functools.partial(_diff_kernel, tq=tq, tk=tk, lam_init=lam_init),
        grid=(batch, DIFF_HEADS // heads),
        in_specs=[pl.BlockSpec(lam_p.shape, lambda b, hg: (0, 0)),
                  pl.BlockSpec((1, HEAD_DIM), lambda b, hg: (0, 0)),
                  q_s, k_s, v_s, g_s],
        out_specs=pl.BlockSpec((seq, heads * LANES), lambda b, hg: (b, hg)),
        out_shape=jax.ShapeDtypeStruct((batch * seq, DIFF_W), jnp.bfloat16),
        scratch_shapes=[pltpu.VMEM((heads, HEAD_DIM + ONES_ROWS, seq), jnp.bfloat16)],
        compiler_params=_params("parallel", "parallel"),
        name="diff_attention",
    )(lam_p, subln_row, qk, qk, vt, gates)


def _merge_kernel(of_ref, os_ref, od_ref, wf_ref, ws_ref, wd_ref, x_ref, g_ref, b_ref, y_ref, yb_ref):
    y = (jnp.dot(of_ref[...], wf_ref[...], preferred_element_type=jnp.float32)
         + jnp.dot(os_ref[...], ws_ref[...], preferred_element_type=jnp.float32)
         + jnp.dot(od_ref[...], wd_ref[...], preferred_element_type=jnp.float32))
    z = DEEPNORM_ALPHA * x_ref[...] + y
    mu = jnp.mean(z, axis=1, keepdims=True)
    zc = z - mu
    var = jnp.mean(zc * zc, axis=1, keepdims=True)
    out = zc * lax.rsqrt(var + LN_EPS) * g_ref[...] + b_ref[...]
    y_ref[...] = out
    yb_ref[...] = out.astype(yb_ref.dtype)


def _merge(o_fox, o_sb, o_diff, w_out, layer, x, ln_g, ln_b, tm):
    t, d = x.shape
    row = lambda w: pl.BlockSpec((tm, w), lambda i: (i, 0))
    w_rows = lambda rows, start: pl.BlockSpec((None, rows, d), lambda i: (layer, start // rows, 0))
    vec = lambda: pl.BlockSpec((None, 1, d), lambda i: (layer, 0, 0))
    return pl.pallas_call(
        _merge_kernel,
        grid=(t // tm,),
        in_specs=[row(FOX_W), row(SB_W), row(DIFF_W),
                  w_rows(FOX_W, 0), w_rows(SB_W, FOX_W), w_rows(DIFF_W, FOX_W + SB_W),
                  row(d), vec(), vec()],
        out_specs=[row(d), row(d)],
        out_shape=[jax.ShapeDtypeStruct((t, d), jnp.float32),
                   jax.ShapeDtypeStruct((t, d), jnp.bfloat16)],
        compiler_params=_params("parallel"),
        name="merge_layernorm",
    )(o_fox, o_sb, o_diff, w_out, w_out, w_out, x, ln_g, ln_b)


PREP_UNIT = 256
QK_COLS = 2 * (FOX_W + SB_W + DIFF_W)
V_COLS = FOX_W + SB_W + DIFF_W
GATE_COLS = FOX_W + SB_W + DIFF_W


def _cast_kernel(w_ref, o_ref):
    o_ref[...] = w_ref[...].astype(o_ref.dtype)


def _permuted_in_weights(w_in):
    depth, k, _ = w_in.shape
    sizes = (FOX_W,) * 4 + (SB_W,) * 4 + (DIFF_W,) * 4
    starts = [sum(sizes[:i]) for i in range(len(sizes))]
    fq, fk, fv, fg, sq, sk, sv, sg, dq, dk, dv, dg = range(12)
    order = [fq, fk, sq, sk, dq, dk, fv, sv, dv, fg, sg, dg]
    steps, dst, shift = [], 0, 0
    for grp in order:
        new_shift = (starts[grp] - dst) // PREP_UNIT
        steps.append((dst // PREP_UNIT, new_shift - shift))
        shift = new_shift
        dst += sizes[grp]
    n_units = dst // PREP_UNIT

    def src_unit(j):
        return j + sum(jnp.where(j >= first, delta, 0) for first, delta in steps)

    return pl.pallas_call(
        _cast_kernel,
        grid=(depth, n_units),
        in_specs=[pl.BlockSpec((None, k, PREP_UNIT), lambda l, j: (l, 0, src_unit(j)))],
        out_specs=pl.BlockSpec((None, k, PREP_UNIT), lambda l, j: (l, 0, j)),
        out_shape=jax.ShapeDtypeStruct((depth, k, n_units * PREP_UNIT), jnp.bfloat16),
        compiler_params=_params("parallel", "parallel"),
        name="permute_in_weights",
    )(w_in)


def _forget_in_weights(w_in):
    ff = w_in[:, :, QK_COLS + V_COLS + GATE_COLS:]
    return jnp.pad(ff, ((0, 0), (0, 0), (0, LANES - FOX_HEADS))).astype(jnp.bfloat16)


def _qk_col_scale():
    one = lambda n: jnp.ones((n,), jnp.float32)
    full = lambda n, v: jnp.full((n,), v, jnp.float32)
    return jnp.concatenate([
        full(FOX_W, HEAD_DIM ** -0.5 * LOG2E), one(FOX_W),
        full(SB_W, HEAD_DIM ** -0.5 * LOG2E), one(SB_W),
        full(DIFF_W, DIFF_QK_DIM ** -0.5 * LOG2E), one(DIFF_W)])[None, :]


def kernel(x, w_in, b_f, diff_lambda, diff_subln_g, w_out, ln_g, ln_b):
    batch, seq, d_model = x.shape
    depth = w_in.shape[0]
    t = batch * seq
    tk = min(seq, KEY_BLOCK)
    tq = min(seq, ATTN_TQ)

    w_in_b = w_in.astype(jnp.bfloat16)
    w_perm = _permuted_in_weights(w_in_b)
    w_ff = _forget_in_weights(w_in_b)
    w_out_b = w_out.astype(jnp.bfloat16)
    qk_scale = _qk_col_scale()
    b_f_rows = jnp.pad(b_f, ((0, 0), (0, LANES - FOX_HEADS)))
    ln_g3 = ln_g.reshape(depth, 1, d_model)
    ln_b3 = ln_b.reshape(depth, 1, d_model)

    xf = x.reshape(t, d_model)
    xb = xf
    for l in range(depth):
        lam_init = 0.8 - 0.6 * math.exp(-0.3 * l)
        tm = min(t, PROJ_ROWS_F32 if xb.dtype == jnp.float32 else PROJ_ROWS_BF16)
        qk = _project(xb, w_perm, l, 0, QK_COLS, qk_scale, jnp.bfloat16, tm, PROJ_COLS, "qk_projection")
        vt = _project(xb, w_perm, l, QK_COLS, V_COLS, None, jnp.bfloat16, tm, PROJ_COLS, "v_projection",
                      transpose_out=True)
        gates = _project_gates(xb, w_perm, w_ff, l, QK_COLS + V_COLS, min(t, WIDE_ROWS))
        c = _forget_prefix(gates, b_f_rows[l:l + 1], batch, seq, tk)
        o_fox = _fox_attention(qk, vt, gates, c, batch, seq, tq, tk, FOX_GROUP)
        o_sb = _sb_attention(qk, vt, gates, batch, seq, tq, tk, SB_GROUP)
        o_diff = _diff_attention(qk, vt, gates, diff_lambda[l], diff_subln_g[l:l + 1], batch, seq, tq, tk,
                                 lam_init, DIFF_GROUP)
        xf, xb = _merge(o_fox, o_sb, o_diff, w_out_b, l, xf, ln_g3, ln_b3, min(t, WIDE_ROWS))
    return xf.reshape(batch, seq, d_model)
```

```python
import functools
import math

import jax
import jax.numpy as jnp
from jax import lax
from jax.experimental import pallas as pl
from jax.experimental.pallas import tpu as pltpu

HEAD_DIM = 128
FOX_HEADS = 6
SB_HEADS = 6
DIFF_HEADS = 4
DIFF_QK_DIM = HEAD_DIM // 2
FOX_W = FOX_HEADS * HEAD_DIM
SB_W = SB_HEADS * HEAD_DIM
DIFF_W = DIFF_HEADS * HEAD_DIM
DEPTH_FOR_DEEPNORM = 4
DEEPNORM_ALPHA = (2 * DEPTH_FOR_DEEPNORM) ** 0.25
LN_EPS = 1e-5
SUBLN_EPS = 1e-5
NEG_BIG = -1e30
LOG2E = math.log2(math.e)

LANES = 128
VMEM_LIMIT_BYTES = 56 * 1024 * 1024

FOX_Q0, FOX_K0 = 0, FOX_HEADS
SB_Q0, SB_K0 = 2 * FOX_HEADS, 2 * FOX_HEADS + SB_HEADS
DIFF_Q0, DIFF_K0 = 2 * (FOX_HEADS + SB_HEADS), 2 * (FOX_HEADS + SB_HEADS) + DIFF_HEADS
FOX_V0, SB_V0, DIFF_V0 = 0, FOX_HEADS, FOX_HEADS + SB_HEADS
GATE_FOX0, GATE_SB0, GATE_DIFF0 = 0, FOX_HEADS, FOX_HEADS + SB_HEADS
GATE_FF = FOX_HEADS + SB_HEADS + DIFF_HEADS
GATE_BLOCKS = GATE_FF + 1
FOX_GROUP, SB_GROUP, DIFF_GROUP = 3, 2, 2
ATTN_TQ = 2048
KEY_BLOCK = 256
SB_SUB = 256
PROJ_ROWS = 2048
PROJ_COLS = 1024
WIDE_ROWS = 512


def _params(*semantics):
    return pltpu.CompilerParams(dimension_semantics=semantics, vmem_limit_bytes=VMEM_LIMIT_BYTES)


def _proj_kernel(x_ref, w_ref, *rest, transpose_out):
    o_ref = rest[-1]
    acc = jnp.dot(x_ref[...].astype(w_ref.dtype), w_ref[...], preferred_element_type=jnp.float32)
    if len(rest) == 2:
        acc = acc * rest[0][...]
    for c in range(o_ref.shape[0]):
        blk = acc[:, c * LANES:(c + 1) * LANES]
        o_ref[c] = (blk.T if transpose_out else blk).astype(o_ref.dtype)


def _gate_proj_kernel(x_ref, w_ref, wf_ref, o_ref, *xb_ref):
    x = x_ref[...].astype(w_ref.dtype)
    if xb_ref:
        xb_ref[0][...] = x
    acc = jnp.dot(x, w_ref[...], preferred_element_type=jnp.float32)
    for c in range(GATE_FF):
        o_ref[c] = acc[:, c * LANES:(c + 1) * LANES]
    o_ref[GATE_FF] = jnp.dot(x, wf_ref[...], preferred_element_type=jnp.float32)


def _project_gates(x, w, w_ff, layer, col0, tm):
    t, k = x.shape
    assert t % tm == 0 and col0 % GATE_COLS == 0, (t, tm, col0)
    out_specs = [pl.BlockSpec((GATE_BLOCKS, tm, LANES), lambda i: (0, i, 0))]
    out_shape = [jax.ShapeDtypeStruct((GATE_BLOCKS, t, LANES), jnp.float32)]
    if x.dtype != w.dtype:
        out_specs.append(pl.BlockSpec((tm, k), lambda i: (i, 0)))
        out_shape.append(jax.ShapeDtypeStruct((t, k), w.dtype))
    out = pl.pallas_call(
        _gate_proj_kernel,
        grid=(t // tm,),
        in_specs=[pl.BlockSpec((tm, k), lambda i: (i, 0)),
                  pl.BlockSpec((None, k, GATE_COLS), lambda i: (layer, 0, col0 // GATE_COLS)),
                  pl.BlockSpec((None, k, LANES), lambda i: (layer, 0, 0))],
        out_specs=out_specs,
        out_shape=out_shape,
        compiler_params=_params("parallel"),
        name="gate_projection",
    )(x, w, w_ff)
    return (out[0], out[1]) if len(out) == 2 else (out[0], x)


def _project(x, w, layer, col0, n, col_scale, out_dtype, tm, tn, name, transpose_out=False):
    t, k = x.shape
    assert t % tm == 0 and n % tn == 0 and col0 % tn == 0, (t, tm, n, tn, col0)
    if transpose_out:
        out_spec = pl.BlockSpec((tn // LANES, LANES, tm), lambda i, j: (j, 0, i))
        out_shape = (n // LANES, LANES, t)
    else:
        out_spec = pl.BlockSpec((tn // LANES, tm, LANES), lambda i, j: (j, i, 0))
        out_shape = (n // LANES, t, LANES)
    in_specs = [pl.BlockSpec((tm, k), lambda i, j: (i, 0)),
                pl.BlockSpec((None, k, tn), lambda i, j: (layer, 0, col0 // tn + j))]
    operands = [x, w]
    if col_scale is not None:
        in_specs.append(pl.BlockSpec((1, tn), lambda i, j: (0, j)))
        operands.append(col_scale)
    return pl.pallas_call(
        functools.partial(_proj_kernel, transpose_out=transpose_out),
        grid=(t // tm, n // tn),
        in_specs=in_specs,
        out_specs=out_spec,
        out_shape=jax.ShapeDtypeStruct(out_shape, out_dtype),
        compiler_params=_params("parallel", "parallel"),
        name=name,
    )(*operands)


def _log_sigmoid(z):
    return jnp.minimum(z, 0.0) - jnp.log1p(jnp.exp(-jnp.abs(z)))


def _split3(x):
    x1 = x.astype(jnp.bfloat16)
    r = x - x1.astype(jnp.float32)
    x2 = r.astype(jnp.bfloat16)
    x3 = (r - x2.astype(jnp.float32)).astype(jnp.bfloat16)
    return x1, x2, x3


def _decay_kernel(ff_ref, bf_ref, tri_ref, c_ref, *, chunk):
    s = ff_ref.shape[1]
    tri = tri_ref[...]
    carry = jnp.zeros((1, LANES), jnp.float32)
    for c in range(s // chunk):
        lf = _log_sigmoid(ff_ref[0, c * chunk:(c + 1) * chunk, :] + bf_ref[...])
        x1, x2, x3 = _split3(lf)
        cs = (jnp.dot(tri, x1, preferred_element_type=jnp.float32)
              + jnp.dot(tri, x2, preferred_element_type=jnp.float32)
              + jnp.dot(tri, x3, preferred_element_type=jnp.float32)) + carry
        c_ref[0, c * chunk:(c + 1) * chunk, :] = cs
        carry = cs[chunk - 1:chunk, :]


def _forget_prefix(gates, b_f_row, batch, seq, chunk):
    tri = (lax.broadcasted_iota(jnp.int32, (chunk, chunk), 0)
           >= lax.broadcasted_iota(jnp.int32, (chunk, chunk), 1)).astype(jnp.bfloat16)
    return pl.pallas_call(
        functools.partial(_decay_kernel, chunk=chunk),
        grid=(batch,),
        in_specs=[
            pl.BlockSpec((1, seq, LANES), lambda b: (GATE_FF, b, 0)),
            pl.BlockSpec((1, LANES), lambda b: (0, 0)),
            pl.BlockSpec((chunk, chunk), lambda b: (0, 0)),
        ],
        out_specs=pl.BlockSpec((1, seq, LANES), lambda b: (b, 0, 0)),
        out_shape=jax.ShapeDtypeStruct((batch, seq, LANES), jnp.float32),
        compiler_params=_params("parallel"),
        name="forget_prefix",
    )(gates, b_f_row, tri)


def _kq(k, q):
    return lax.dot_general(k, q, (((1,), (1,)), ((), ())), preferred_element_type=jnp.float32)


def _silu(g):
    return g * (1.0 / (1.0 + jnp.exp(-g)))


def _lane_tiles(a, tile):
    n = a.shape[1] // LANES
    return jnp.concatenate([tile(a[:, i * LANES:(i + 1) * LANES]) for i in range(n)], axis=1)


def _gather_lanes(x, ranges):
    if ranges is None:
        return x
    return jnp.concatenate([x[:, a:b] for a, b in ranges], axis=1)


def _scatter_lanes(x, new, ranges):
    if ranges is None:
        return new
    pieces, pos, off = [], 0, 0
    for a, b in ranges:
        if a > pos:
            pieces.append(x[:, pos:a])
        pieces.append(new[:, off:off + b - a])
        off += b - a
        pos = b
    if pos < x.shape[1]:
        pieces.append(x[:, pos:])
    return jnp.concatenate(pieces, axis=1)


ONES_ROWS = 16


def _with_ones_rows(vt):
    ones = (lax.broadcasted_iota(jnp.int32, (ONES_ROWS, vt.shape[1]), 0) == 0).astype(vt.dtype)
    return jnp.concatenate([vt, ones], axis=0)


def _softmax_steps(sts, vts, carry, ranges=None, mask=None):
    stats = []
    for st, (m, _) in zip(sts, carry):
        if mask is not None:
            st = jnp.where(mask, st, NEG_BIG)
        m_old = _gather_lanes(m, ranges)
        m_new = jnp.maximum(m_old, jnp.max(st, axis=0, keepdims=True))
        stats.append((m_new, jnp.exp2(m_old - m_new), jnp.exp2(st - m_new)))
    out = []
    for vt, (m, acc), (m_new, alpha, p) in zip(vts, carry, stats):
        acc_new = alpha * _gather_lanes(acc, ranges) + jnp.dot(
            vt, p.astype(vt.dtype), preferred_element_type=jnp.float32)
        out.append((_scatter_lanes(m, m_new, ranges), _scatter_lanes(acc, acc_new, ranges)))
    return tuple(out)


def _normalised(acc):
    return acc[0:HEAD_DIM] * (1.0 / acc[HEAD_DIM:HEAD_DIM + 1])


def _attn_specs(q0, k0, v0, g0, seq, heads):
    blk = lambda base: pl.BlockSpec((heads, seq, LANES), lambda b, hg: (base // heads + hg, b, 0))
    vt = pl.BlockSpec((heads, LANES, seq), lambda b, hg: (v0 // heads + hg, 0, b))
    return blk(q0), blk(k0), vt, blk(g0)


def _causal(tk, width, strict=False):
    rows = lax.broadcasted_iota(jnp.int32, (tk, width), 0)
    cols = lax.broadcasted_iota(jnp.int32, (tk, width), 1)
    return cols > rows if strict else cols >= rows


def _fox_kernel(q_ref, k_ref, vt_ref, g_ref, c_ref, o_ref, bias_ref, vta_ref, *, tq, tk):
    heads, seq = q_ref.shape[0], q_ref.shape[1]
    ratio = tq // tk
    lane = lax.broadcasted_iota(jnp.int32, (seq, LANES), 1)
    for g in range(heads):
        h = pl.program_id(1) * heads + g
        col = jnp.sum(jnp.where(lane == h, c_ref[0], 0.0), axis=1, keepdims=True)
        bias_ref[g] = jnp.broadcast_to(col * (-LOG2E), (seq, LANES))
        vta_ref[g] = _with_ones_rows(vt_ref[g])

    for qi in range(seq // tq):
        r0 = qi * tq
        qs = [q_ref[g, r0:r0 + tq, :] for g in range(heads)]

        def scores(c0, lane0):
            return [_lane_tiles(_kq(k_ref[g, pl.ds(c0, tk), :], qs[g][lane0:]),
                                lambda a, g=g: a + bias_ref[g, pl.ds(c0, tk), :])
                    for g in range(heads)]

        def values(c0):
            return [vta_ref[g, :, pl.ds(c0, tk)] for g in range(heads)]

        def body(j, carry):
            c0 = pl.multiple_of(j * tk, tk)
            return _softmax_steps(scores(c0, 0), values(c0), carry)

        init = (jnp.full((1, tq), NEG_BIG, jnp.float32),
                jnp.zeros((HEAD_DIM + ONES_ROWS, tq), jnp.float32))
        carry = lax.fori_loop(0, qi * ratio, body, (init,) * heads)
        for d in range(ratio):
            ranges = [(d * tk, tq)] if d else None
            carry = _softmax_steps(scores(r0 + d * tk, d * tk), values(r0 + d * tk), carry, ranges,
                                   _causal(tk, tq - d * tk))
        for g in range(heads):
            out = _normalised(carry[g][1]).T * _silu(g_ref[g, r0:r0 + tq, :])
            o_ref[r0:r0 + tq, g * LANES:(g + 1) * LANES] = out.astype(o_ref.dtype)


def _fox_attention(qk, vt, gates, c, batch, seq, tq, tk, heads):
    q_s, k_s, v_s, g_s = _attn_specs(FOX_Q0, FOX_K0, FOX_V0, GATE_FOX0, seq, heads)
    return pl.pallas_call(
        functools.partial(_fox_kernel, tq=tq, tk=tk),
        grid=(batch, FOX_HEADS // heads),
        in_specs=[q_s, k_s, v_s, g_s, pl.BlockSpec((1, seq, LANES), lambda b, hg: (b, 0, 0))],
        out_specs=pl.BlockSpec((seq, heads * LANES), lambda b, hg: (b, hg)),
        out_shape=jax.ShapeDtypeStruct((batch * seq, FOX_W), jnp.bfloat16),
        scratch_shapes=[pltpu.VMEM((heads, seq, LANES), jnp.float32),
                        pltpu.VMEM((heads, HEAD_DIM + ONES_ROWS, seq), jnp.bfloat16)],
        compiler_params=_params("parallel", "parallel"),
        name="fox_attention",
    )(qk, qk, vt, gates, c)


def _sb_blocks(qs, ks, vts, ut, states, ranges, strict):
    tk = ks[0].shape[0]
    sub_keys = ut.shape[0]
    zs = [_kq(k, q) for k, q in zip(ks, qs)]
    parts = []
    for z in zs:
        neg_abs = lax.bitcast_convert_type(
            lax.bitcast_convert_type(z, jnp.uint32) | jnp.uint32(0x80000000), jnp.float32)
        sp = jnp.log2(1.0 + jnp.exp2(neg_abs))
        log_beta = jnp.minimum(z, 0.0) - sp
        log_keep = log_beta - z
        if strict is not None:
            log_keep = jnp.where(strict, log_keep, 0.0)
        parts.append((log_beta, log_keep, log_keep.astype(jnp.bfloat16)))
    subs = []
    for _, _, terms in parts:
        subs.append([jnp.dot(ut, terms[i:i + sub_keys], preferred_element_type=jnp.float32)
                     for i in range(0, tk, sub_keys)])
    ws, carries = [], []
    for (log_beta, log_keep, _), sub, (carry_all, _) in zip(parts, subs, states):
        carry = _gather_lanes(carry_all, ranges)
        laters = [None] * len(sub)
        for n in reversed(range(len(sub))):
            laters[n] = sub[n] + carry
            carry = carry + sub[n][0:1] + log_keep[n * sub_keys:n * sub_keys + 1]
        w = jnp.exp2(log_beta + jnp.concatenate(laters, axis=0))
        if strict is not None:
            w = jnp.where(strict, w, 0.0)
        ws.append(w)
        carries.append(_scatter_lanes(carry_all, carry, ranges))
    out = []
    for w, vt, carry, (_, acc) in zip(ws, vts, carries, states):
        acc_new = _gather_lanes(acc, ranges) + jnp.dot(vt, w.astype(vt.dtype),
                                                       preferred_element_type=jnp.float32)
        out.append((carry, _scatter_lanes(acc, acc_new, ranges)))
    return tuple(out)


def _sb_kernel(q_ref, k_ref, vt_ref, g_ref, ut_ref, o_ref, *, tq, tk):
    heads, seq = q_ref.shape[0], q_ref.shape[1]
    ratio = tq // tk
    ut = ut_ref[...]

    for qi in range(seq // tq):
        r0 = qi * tq
        qs = [q_ref[g, r0:r0 + tq, :] for g in range(heads)]

        def block(c0, lane0, state, ranges, mask):
            return _sb_blocks([q[lane0:] for q in qs],
                              [k_ref[g, pl.ds(c0, tk), :] for g in range(heads)],
                              [vt_ref[g, :, pl.ds(c0, tk)] for g in range(heads)], ut, state, ranges, mask)

        init = (jnp.zeros((1, tq), jnp.float32), jnp.zeros((HEAD_DIM, tq), jnp.float32))
        state = (init,) * heads
        for d in reversed(range(ratio)):
            state = block(r0 + d * tk, d * tk, state, [(d * tk, tq)] if d else None,
                          _causal(tk, tq - d * tk, strict=True))
        n_full = qi * ratio
        state = lax.fori_loop(
            0, n_full,
            lambda i, st: block(pl.multiple_of((n_full - 1 - i) * tk, tk), 0, st, None, None), state)
        for g in range(heads):
            out = state[g][1].T * _silu(g_ref[g, r0:r0 + tq, :])
            o_ref[r0:r0 + tq, g * LANES:(g + 1) * LANES] = out.astype(o_ref.dtype)


def _sb_attention(qk, vt, gates, batch, seq, tq, tk, heads):
    q_s, k_s, v_s, g_s = _attn_specs(SB_Q0, SB_K0, SB_V0, GATE_SB0, seq, heads)
    sub = min(SB_SUB, tk)
    ut = (lax.broadcasted_iota(jnp.int32, (sub, sub), 1)
          > lax.broadcasted_iota(jnp.int32, (sub, sub), 0)).astype(jnp.bfloat16)
    return pl.pallas_call(
        functools.partial(_sb_kernel, tq=tq, tk=tk),
        grid=(batch, SB_HEADS // heads),
        in_specs=[q_s, k_s, v_s, g_s, pl.BlockSpec((sub, sub), lambda b, hg: (0, 0))],
        out_specs=pl.BlockSpec((seq, heads * LANES), lambda b, hg: (b, hg)),
        out_shape=jax.ShapeDtypeStruct((batch * seq, SB_W), jnp.bfloat16),
        compiler_params=_params("parallel", "parallel"),
        name="sb_attention",
    )(qk, qk, vt, gates, ut)


def _diff_kernel(lam_ref, gsub_ref, q_ref, k_ref, vt_ref, g_ref, o_ref, vta_ref, *, tq, tk, lam_init):
    heads, seq = q_ref.shape[0], q_ref.shape[1]
    ratio = tq // tk
    lp = lam_ref[...]
    lam = (jnp.exp(jnp.sum(lp[0:1] * lp[1:2], axis=1, keepdims=True))
           - jnp.exp(jnp.sum(lp[2:3] * lp[3:4], axis=1, keepdims=True)) + lam_init)
    key = lax.broadcasted_iota(jnp.int32, (tk, LANES), 0).astype(jnp.float32)
    ramps = []
    for g in range(heads):
        h = pl.program_id(1) * heads + g
        expo = jnp.full((tk, LANES), 127 - (8 // DIFF_HEADS) * (h + 1), jnp.int32)
        slope = lax.bitcast_convert_type(expo << 23, jnp.float32) * LOG2E
        ramps.append((slope, slope * key))
        vta_ref[g] = _with_ones_rows(vt_ref[g])
    lane = lax.broadcasted_iota(jnp.int32, (tq, HEAD_DIM), 1)

    for qi in range(seq // tq):
        r0 = qi * tq
        q1s, q2s = [], []
        for g in range(heads):
            q = q_ref[g, r0:r0 + tq, :]
            zero = jnp.zeros_like(q)
            q1s.append(jnp.where(lane < DIFF_QK_DIM, q, zero))
            q2s.append(jnp.where(lane >= DIFF_QK_DIM, q, zero))

        def scores(c0, lane0):
            out = []
            for g in range(heads):
                slope, ramp = ramps[g]
                bias = ramp + slope * jnp.asarray(c0 - r0, jnp.float32)
                qq = jnp.concatenate([q1s[g][lane0:], q2s[g][lane0:]], axis=0)
                out.append(_lane_tiles(_kq(k_ref[g, pl.ds(c0, tk), :], qq), lambda a, b=bias: a + b))
            return out

        def values(c0):
            return [vta_ref[g, :, pl.ds(c0, tk)] for g in range(heads)]

        def body(j, carry):
            c0 = pl.multiple_of(j * tk, tk)
            return _softmax_steps(scores(c0, 0), values(c0), carry)

        init = (jnp.full((1, 2 * tq), NEG_BIG, jnp.float32),
                jnp.zeros((HEAD_DIM + ONES_ROWS, 2 * tq), jnp.float32))
        carry = lax.fori_loop(0, qi * ratio, body, (init,) * heads)
        for d in range(ratio):
            ranges = [(d * tk, tq), (tq + d * tk, 2 * tq)] if d else None
            half = _causal(tk, tq - d * tk)
            carry = _softmax_steps(scores(r0 + d * tk, d * tk), values(r0 + d * tk), carry, ranges,
                                   jnp.concatenate([half, half], axis=1))
        for g in range(heads):
            o = _normalised(carry[g][1])
            o = o[:, 0:tq] - lam * o[:, tq:2 * tq]
            o = o * lax.rsqrt(jnp.mean(o * o, axis=0, keepdims=True) + SUBLN_EPS)
            o = o.T * gsub_ref[...] * (1.0 - lam_init)
            out = o * _silu(g_ref[g, r0:r0 + tq, :])
            o_ref[r0:r0 + tq, g * LANES:(g + 1) * LANES] = out.astype(o_ref.dtype)


def _diff_attention(qk, vt, gates, lam_p, subln_row, batch, seq, tq, tk, lam_init, heads):
    q_s, k_s, v_s, g_s = _attn_specs(DIFF_Q0, DIFF_K0, DIFF_V0, GATE_DIFF0, seq, heads)
    return pl.pallas_call(
        functools.partial(_diff_kernel, tq=tq, tk=tk, lam_init=lam_init),
        grid=(batch, DIFF_HEADS // heads),
        in_specs=[pl.BlockSpec(lam_p.shape, lambda b, hg: (0, 0)),
                  pl.BlockSpec((1, HEAD_DIM), lambda b, hg: (0, 0)),
                  q_s, k_s, v_s, g_s],
        out_specs=pl.BlockSpec((seq, heads * LANES), lambda b, hg: (b, hg)),
        out_shape=jax.ShapeDtypeStruct((batch * seq, DIFF_W), jnp.bfloat16),
        scratch_shapes=[pltpu.VMEM((heads, HEAD_DIM + ONES_ROWS, seq), jnp.bfloat16)],
        compiler_params=_params("parallel", "parallel"),
        name="diff_attention",
    )(lam_p, subln_row, qk, qk, vt, gates)


def _merge_kernel(of_ref, os_ref, od_ref, wf_ref, ws_ref, wd_ref, x_ref, g_ref, b_ref, y_ref, yb_ref):
    y = (jnp.dot(of_ref[...], wf_ref[...], preferred_element_type=jnp.float32)
         + jnp.dot(os_ref[...], ws_ref[...], preferred_element_type=jnp.float32)
         + jnp.dot(od_ref[...], wd_ref[...], preferred_element_type=jnp.float32))
    z = DEEPNORM_ALPHA * x_ref[...] + y
    mu = jnp.mean(z, axis=1, keepdims=True)
    zc = z - mu
    var = jnp.mean(zc * zc, axis=1, keepdims=True)
    out = zc * lax.rsqrt(var + LN_EPS) * g_ref[...] + b_ref[...]
    y_ref[...] = out
    yb_ref[...] = out.astype(yb_ref.dtype)


def _merge(o_fox, o_sb, o_diff, w_out, layer, x, ln_g, ln_b, tm):
    t, d = x.shape
    row = lambda w: pl.BlockSpec((tm, w), lambda i: (i, 0))
    w_rows = lambda rows, start: pl.BlockSpec((None, rows, d), lambda i: (layer, start // rows, 0))
    vec = lambda: pl.BlockSpec((None, 1, d), lambda i: (layer, 0, 0))
    return pl.pallas_call(
        _merge_kernel,
        grid=(t // tm,),
        in_specs=[row(FOX_W), row(SB_W), row(DIFF_W),
                  w_rows(FOX_W, 0), w_rows(SB_W, FOX_W), w_rows(DIFF_W, FOX_W + SB_W),
                  row(d), vec(), vec()],
        out_specs=[row(d), row(d)],
        out_shape=[jax.ShapeDtypeStruct((t, d), jnp.float32),
                   jax.ShapeDtypeStruct((t, d), jnp.bfloat16)],
        compiler_params=_params("parallel"),
        name="merge_layernorm",
    )(o_fox, o_sb, o_diff, w_out, w_out, w_out, x, ln_g, ln_b)


PREP_UNIT = 256
QK_COLS = 2 * (FOX_W + SB_W + DIFF_W)
V_COLS = FOX_W + SB_W + DIFF_W
GATE_COLS = FOX_W + SB_W + DIFF_W


def _cast_kernel(w_ref, o_ref):
    o_ref[...] = w_ref[...].astype(o_ref.dtype)


def _permuted_in_weights(w_in):
    depth, k, _ = w_in.shape
    sizes = (FOX_W,) * 4 + (SB_W,) * 4 + (DIFF_W,) * 4
    starts = [sum(sizes[:i]) for i in range(len(sizes))]
    fq, fk, fv, fg, sq, sk, sv, sg, dq, dk, dv, dg = range(12)
    order = [fq, fk, sq, sk, dq, dk, fv, sv, dv, fg, sg, dg]
    steps, dst, shift = [], 0, 0
    for grp in order:
        new_shift = (starts[grp] - dst) // PREP_UNIT
        steps.append((dst // PREP_UNIT, new_shift - shift))
        shift = new_shift
        dst += sizes[grp]
    n_units = dst // PREP_UNIT

    def src_unit(j):
        return j + sum(jnp.where(j >= first, delta, 0) for first, delta in steps)

    return pl.pallas_call(
        _cast_kernel,
        grid=(depth, n_units),
        in_specs=[pl.BlockSpec((None, k, PREP_UNIT), lambda l, j: (l, 0, src_unit(j)))],
        out_specs=pl.BlockSpec((None, k, PREP_UNIT), lambda l, j: (l, 0, j)),
        out_shape=jax.ShapeDtypeStruct((depth, k, n_units * PREP_UNIT), jnp.bfloat16),
        compiler_params=_params("parallel", "parallel"),
        name="permute_in_weights",
    )(w_in)


def _forget_in_weights(w_in):
    ff = w_in[:, :, QK_COLS + V_COLS + GATE_COLS:]
    return jnp.pad(ff, ((0, 0), (0, 0), (0, LANES - FOX_HEADS))).astype(jnp.bfloat16)


def _qk_col_scale():
    one = lambda n: jnp.ones((n,), jnp.float32)
    full = lambda n, v: jnp.full((n,), v, jnp.float32)
    return jnp.concatenate([
        full(FOX_W, HEAD_DIM ** -0.5 * LOG2E), one(FOX_W),
        full(SB_W, HEAD_DIM ** -0.5 * LOG2E), one(SB_W),
        full(DIFF_W, DIFF_QK_DIM ** -0.5 * LOG2E), one(DIFF_W)])[None, :]


def kernel(x, w_in, b_f, diff_lambda, diff_subln_g, w_out, ln_g, ln_b):
    batch, seq, d_model = x.shape
    depth = w_in.shape[0]
    t = batch * seq
    tk = min(seq, KEY_BLOCK)
    tq = min(seq, ATTN_TQ)

    w_in_b = w_in.astype(jnp.bfloat16)
    w_perm = _permuted_in_weights(w_in_b)
    w_ff = _forget_in_weights(w_in_b)
    w_out_b = w_out.astype(jnp.bfloat16)
    qk_scale = _qk_col_scale()
    b_f_rows = jnp.pad(b_f, ((0, 0), (0, LANES - FOX_HEADS)))
    ln_g3 = ln_g.reshape(depth, 1, d_model)
    ln_b3 = ln_b.reshape(depth, 1, d_model)

    xf = x.reshape(t, d_model)
    xb = xf
    for l in range(depth):
        lam_init = 0.8 - 0.6 * math.exp(-0.3 * l)
        gates, xb = _project_gates(xb, w_perm, w_ff, l, QK_COLS + V_COLS, min(t, WIDE_ROWS))
        tm = min(t, PROJ_ROWS)
        qk = _project(xb, w_perm, l, 0, QK_COLS, qk_scale, jnp.bfloat16, tm, PROJ_COLS, "qk_projection")
        vt = _project(xb, w_perm, l, QK_COLS, V_COLS, None, jnp.bfloat16, tm, PROJ_COLS, "v_projection",
                      transpose_out=True)
        c = _forget_prefix(gates, b_f_rows[l:l + 1], batch, seq, tk)
        o_fox = _fox_attention(qk, vt, gates, c, batch, seq, tq, tk, FOX_GROUP)
        o_sb = _sb_attention(qk, vt, gates, batch, seq, tq, tk, SB_GROUP)
        o_diff = _diff_attention(qk, vt, gates, diff_lambda[l], diff_subln_g[l:l + 1], batch, seq, tq, tk,
                                 lam_init, DIFF_GROUP)
        xf, xb = _merge(o_fox, o_sb, o_diff, w_out_b, l, xf, ln_g3, ln_b3, min(t, WIDE_ROWS))
    return xf.reshape(batch, seq, d_model)
```

```python
import functools
import math

import jax
import jax.numpy as jnp
from jax import lax
from jax.experimental import pallas as pl
from jax.experimental.pallas import tpu as pltpu

HEAD_DIM = 128
FOX_HEADS = 6
SB_HEADS = 6
DIFF_HEADS = 4
DIFF_QK_DIM = HEAD_DIM // 2
FOX_W = FOX_HEADS * HEAD_DIM
SB_W = SB_HEADS * HEAD_DIM
DIFF_W = DIFF_HEADS * HEAD_DIM
DEPTH_FOR_DEEPNORM = 4
DEEPNORM_ALPHA = (2 * DEPTH_FOR_DEEPNORM) ** 0.25
LN_EPS = 1e-5
SUBLN_EPS = 1e-5
NEG_BIG = -1e30
LOG2E = math.log2(math.e)

LANES = 128
VMEM_LIMIT_BYTES = 56 * 1024 * 1024

FOX_Q0, FOX_K0 = 0, FOX_HEADS
SB_Q0, SB_K0 = 2 * FOX_HEADS, 2 * FOX_HEADS + SB_HEADS
DIFF_Q0, DIFF_K0 = 2 * (FOX_HEADS + SB_HEADS), 2 * (FOX_HEADS + SB_HEADS) + DIFF_HEADS
FOX_V0, SB_V0, DIFF_V0 = 0, FOX_HEADS, FOX_HEADS + SB_HEADS
GATE_FOX0, GATE_SB0, GATE_DIFF0 = 0, FOX_HEADS, FOX_HEADS + SB_HEADS
GATE_FF = FOX_HEADS + SB_HEADS + DIFF_HEADS
GATE_BLOCKS = GATE_FF + 1
FOX_GROUP, SB_GROUP, DIFF_GROUP = 3, 2, 2
ATTN_TQ = 2048
KEY_BLOCK = 256
SB_SUB = 256
PROJ_ROWS = 2048
PROJ_COLS = 1024
WIDE_ROWS = 512


def _params(*semantics):
    return pltpu.CompilerParams(dimension_semantics=semantics, vmem_limit_bytes=VMEM_LIMIT_BYTES)


def _proj_kernel(x_ref, w_ref, *rest, transpose_out):
    o_ref = rest[-1]
    acc = jnp.dot(x_ref[...].astype(w_ref.dtype), w_ref[...], preferred_element_type=jnp.float32)
    if len(rest) == 2:
        acc = acc * rest[0][...]
    for c in range(o_ref.shape[0]):
        blk = acc[:, c * LANES:(c + 1) * LANES]
        o_ref[c] = (blk.T if transpose_out else blk).astype(o_ref.dtype)


def _gate_proj_kernel(x_ref, w_ref, wf_ref, o_ref, *xb_ref):
    x = x_ref[...].astype(w_ref.dtype)
    if xb_ref:
        xb_ref[0][...] = x
    acc = jnp.dot(x, w_ref[...], preferred_element_type=jnp.float32)
    for c in range(GATE_FF):
        o_ref[c] = acc[:, c * LANES:(c + 1) * LANES]
    o_ref[GATE_FF] = jnp.dot(x, wf_ref[...], preferred_element_type=jnp.float32)


def _project_gates(x, w, w_ff, layer, col0, tm):
    t, k = x.shape
    assert t % tm == 0 and col0 % GATE_COLS == 0, (t, tm, col0)
    out_specs = [pl.BlockSpec((GATE_BLOCKS, tm, LANES), lambda i: (0, i, 0))]
    out_shape = [jax.ShapeDtypeStruct((GATE_BLOCKS, t, LANES), jnp.float32)]
    if x.dtype != w.dtype:
        out_specs.append(pl.BlockSpec((tm, k), lambda i: (i, 0)))
        out_shape.append(jax.ShapeDtypeStruct((t, k), w.dtype))
    out = pl.pallas_call(
        _gate_proj_kernel,
        grid=(t // tm,),
        in_specs=[pl.BlockSpec((tm, k), lambda i: (i, 0)),
                  pl.BlockSpec((None, k, GATE_COLS), lambda i: (layer, 0, col0 // GATE_COLS)),
                  pl.BlockSpec((None, k, LANES), lambda i: (layer, 0, 0))],
        out_specs=out_specs,
        out_shape=out_shape,
        compiler_params=_params("parallel"),
        name="gate_projection",
    )(x, w, w_ff)
    return (out[0], out[1]) if len(out) == 2 else (out[0], x)


def _project(x, w, layer, col0, n, col_scale, out_dtype, tm, tn, name, transpose_out=False):
    t, k = x.shape
    assert t % tm == 0 and n % tn == 0 and col0 % tn == 0, (t, tm, n, tn, col0)
    if transpose_out:
        out_spec = pl.BlockSpec((tn // LANES, LANES, tm), lambda i, j: (j, 0, i))
        out_shape = (n // LANES, LANES, t)
    else:
        out_spec = pl.BlockSpec((tn // LANES, tm, LANES), lambda i, j: (j, i, 0))
        out_shape = (n // LANES, t, LANES)
    in_specs = [pl.BlockSpec((tm, k), lambda i, j: (i, 0)),
                pl.BlockSpec((None, k, tn), lambda i, j: (layer, 0, col0 // tn + j))]
    operands = [x, w]
    if col_scale is not None:
        in_specs.append(pl.BlockSpec((1, tn), lambda i, j: (0, j)))
        operands.append(col_scale)
    return pl.pallas_call(
        functools.partial(_proj_kernel, transpose_out=transpose_out),
        grid=(t // tm, n // tn),
        in_specs=in_specs,
        out_specs=out_spec,
        out_shape=jax.ShapeDtypeStruct(out_shape, out_dtype),
        compiler_params=_params("parallel", "parallel"),
        name=name,
    )(*operands)


def _log_sigmoid(z):
    return jnp.minimum(z, 0.0) - jnp.log1p(jnp.exp(-jnp.abs(z)))


def _split3(x):
    x1 = x.astype(jnp.bfloat16)
    r = x - x1.astype(jnp.float32)
    x2 = r.astype(jnp.bfloat16)
    x3 = (r - x2.astype(jnp.float32)).astype(jnp.bfloat16)
    return x1, x2, x3


def _decay_kernel(ff_ref, bf_ref, tri_ref, c_ref, *, chunk):
    s = ff_ref.shape[1]
    tri = tri_ref[...]
    carry = jnp.zeros((1, LANES), jnp.float32)
    for c in range(s // chunk):
        lf = _log_sigmoid(ff_ref[0, c * chunk:(c + 1) * chunk, :] + bf_ref[...])
        x1, x2, x3 = _split3(lf)
        cs = (jnp.dot(tri, x1, preferred_element_type=jnp.float32)
              + jnp.dot(tri, x2, preferred_element_type=jnp.float32)
              + jnp.dot(tri, x3, preferred_element_type=jnp.float32)) + carry
        c_ref[0, c * chunk:(c + 1) * chunk, :] = cs
        carry = cs[chunk - 1:chunk, :]


def _forget_prefix(gates, b_f_row, batch, seq, chunk):
    tri = (lax.broadcasted_iota(jnp.int32, (chunk, chunk), 0)
           >= lax.broadcasted_iota(jnp.int32, (chunk, chunk), 1)).astype(jnp.bfloat16)
    return pl.pallas_call(
        functools.partial(_decay_kernel, chunk=chunk),
        grid=(batch,),
        in_specs=[
            pl.BlockSpec((1, seq, LANES), lambda b: (GATE_FF, b, 0)),
            pl.BlockSpec((1, LANES), lambda b: (0, 0)),
            pl.BlockSpec((chunk, chunk), lambda b: (0, 0)),
        ],
        out_specs=pl.BlockSpec((1, seq, LANES), lambda b: (b, 0, 0)),
        out_shape=jax.ShapeDtypeStruct((batch, seq, LANES), jnp.float32),
        compiler_params=_params("parallel"),
        name="forget_prefix",
    )(gates, b_f_row, tri)


def _kq(k, q):
    return lax.dot_general(k, q, (((1,), (1,)), ((), ())), preferred_element_type=jnp.float32)


def _silu(g):
    return g * (1.0 / (1.0 + jnp.exp(-g)))


def _lane_tiles(a, tile):
    n = a.shape[1] // LANES
    return jnp.concatenate([tile(a[:, i * LANES:(i + 1) * LANES]) for i in range(n)], axis=1)


def _gather_lanes(x, ranges):
    if ranges is None:
        return x
    return jnp.concatenate([x[:, a:b] for a, b in ranges], axis=1)


def _scatter_lanes(x, new, ranges):
    if ranges is None:
        return new
    pieces, pos, off = [], 0, 0
    for a, b in ranges:
        if a > pos:
            pieces.append(x[:, pos:a])
        pieces.append(new[:, off:off + b - a])
        off += b - a
        pos = b
    if pos < x.shape[1]:
        pieces.append(x[:, pos:])
    return jnp.concatenate(pieces, axis=1)


ONES_ROWS = 16


def _with_ones_rows(vt):
    ones = (lax.broadcasted_iota(jnp.int32, (ONES_ROWS, vt.shape[1]), 0) == 0).astype(vt.dtype)
    return jnp.concatenate([vt, ones], axis=0)


def _softmax_steps(sts, vts, carry, ranges=None, mask=None):
    stats = []
    for st, (m, _) in zip(sts, carry):
        if mask is not None:
            st = jnp.where(mask, st, NEG_BIG)
        m_old = _gather_lanes(m, ranges)
        m_new = jnp.maximum(m_old, jnp.max(st, axis=0, keepdims=True))
        stats.append((m_new, jnp.exp2(m_old - m_new), jnp.exp2(st - m_new)))
    out = []
    for vt, (m, acc), (m_new, alpha, p) in zip(vts, carry, stats):
        acc_new = alpha * _gather_lanes(acc, ranges) + jnp.dot(
            vt, p.astype(vt.dtype), preferred_element_type=jnp.float32)
        out.append((_scatter_lanes(m, m_new, ranges), _scatter_lanes(acc, acc_new, ranges)))
    return tuple(out)


def _normalised(acc):
    return acc[0:HEAD_DIM] * (1.0 / acc[HEAD_DIM:HEAD_DIM + 1])


def _attn_specs(q0, k0, v0, g0, seq, heads):
    blk = lambda base: pl.BlockSpec((heads, seq, LANES), lambda b, hg: (base // heads + hg, b, 0))
    vt = pl.BlockSpec((heads, LANES, seq), lambda b, hg: (v0 // heads + hg, 0, b))
    return blk(q0), blk(k0), vt, blk(g0)


def _causal(tk, width, strict=False):
    rows = lax.broadcasted_iota(jnp.int32, (tk, width), 0)
    cols = lax.broadcasted_iota(jnp.int32, (tk, width), 1)
    return cols > rows if strict else cols >= rows


def _fox_kernel(q_ref, k_ref, vt_ref, g_ref, c_ref, o_ref, bias_ref, vta_ref, *, tq, tk):
    heads, seq = q_ref.shape[0], q_ref.shape[1]
    ratio = tq // tk
    lane = lax.broadcasted_iota(jnp.int32, (seq, LANES), 1)
    for g in range(heads):
        h = pl.program_id(1) * heads + g
        col = jnp.sum(jnp.where(lane == h, c_ref[0], 0.0), axis=1, keepdims=True)
        bias_ref[g] = jnp.broadcast_to(col * (-LOG2E), (seq, LANES))
        vta_ref[g] = _with_ones_rows(vt_ref[g])

    for qi in range(seq // tq):
        r0 = qi * tq
        qs = [q_ref[g, r0:r0 + tq, :] for g in range(heads)]

        def scores(c0, lane0):
            return [_lane_tiles(_kq(k_ref[g, pl.ds(c0, tk), :], qs[g][lane0:]),
                                lambda a, g=g: a + bias_ref[g, pl.ds(c0, tk), :])
                    for g in range(heads)]

        def values(c0):
            return [vta_ref[g, :, pl.ds(c0, tk)] for g in range(heads)]

        def body(j, carry):
            c0 = pl.multiple_of(j * tk, tk)
            return _softmax_steps(scores(c0, 0), values(c0), carry)

        init = (jnp.full((1, tq), NEG_BIG, jnp.float32),
                jnp.zeros((HEAD_DIM + ONES_ROWS, tq), jnp.float32))
        carry = lax.fori_loop(0, qi * ratio, body, (init,) * heads)
        for d in range(ratio):
            ranges = [(d * tk, tq)] if d else None
            carry = _softmax_steps(scores(r0 + d * tk, d * tk), values(r0 + d * tk), carry, ranges,
                                   _causal(tk, tq - d * tk))
        for g in range(heads):
            out = _normalised(carry[g][1]).T * _silu(g_ref[g, r0:r0 + tq, :])
            o_ref[r0:r0 + tq, g * LANES:(g + 1) * LANES] = out.astype(o_ref.dtype)


def _fox_attention(qk, vt, gates, c, batch, seq, tq, tk, heads):
    q_s, k_s, v_s, g_s = _attn_specs(FOX_Q0, FOX_K0, FOX_V0, GATE_FOX0, seq, heads)
    return pl.pallas_call(
        functools.partial(_fox_kernel, tq=tq, tk=tk),
        grid=(batch, FOX_HEADS // heads),
        in_specs=[q_s, k_s, v_s, g_s, pl.BlockSpec((1, seq, LANES), lambda b, hg: (b, 0, 0))],
        out_specs=pl.BlockSpec((seq, heads * LANES), lambda b, hg: (b, hg)),
        out_shape=jax.ShapeDtypeStruct((batch * seq, FOX_W), jnp.bfloat16),
        scratch_shapes=[pltpu.VMEM((heads, seq, LANES), jnp.float32),
                        pltpu.VMEM((heads, HEAD_DIM + ONES_ROWS, seq), jnp.bfloat16)],
        compiler_params=_params("parallel", "parallel"),
        name="fox_attention",
    )(qk, qk, vt, gates, c)


def _sb_blocks(qs, ks, vts, ut, states, ranges, strict):
    tk = ks[0].shape[0]
    sub_keys = ut.shape[0]
    zs = [_kq(k, q) for k, q in zip(ks, qs)]
    parts = []
    for z in zs:
        neg_abs = lax.bitcast_convert_type(
            lax.bitcast_convert_type(z, jnp.uint32) | jnp.uint32(0x80000000), jnp.float32)
        sp = jnp.log2(1.0 + jnp.exp2(neg_abs))
        log_beta = jnp.minimum(z, 0.0) - sp
        log_keep = log_beta - z
        if strict is not None:
            log_keep = jnp.where(strict, log_keep, 0.0)
        parts.append((log_beta, log_keep, log_keep.astype(jnp.bfloat16)))
    subs = []
    for _, _, terms in parts:
        subs.append([jnp.dot(ut, terms[i:i + sub_keys], preferred_element_type=jnp.float32)
                     for i in range(0, tk, sub_keys)])
    ws, carries = [], []
    for (log_beta, log_keep, _), sub, (carry_all, _) in zip(parts, subs, states):
        carry = _gather_lanes(carry_all, ranges)
        laters = [None] * len(sub)
        for n in reversed(range(len(sub))):
            laters[n] = sub[n] + carry
            carry = carry + sub[n][0:1] + log_keep[n * sub_keys:n * sub_keys + 1]
        w = jnp.exp2(log_beta + jnp.concatenate(laters, axis=0))
        if strict is not None:
            w = jnp.where(strict, w, 0.0)
        ws.append(w)
        carries.append(_scatter_lanes(carry_all, carry, ranges))
    out = []
    for w, vt, carry, (_, acc) in zip(ws, vts, carries, states):
        acc_new = _gather_lanes(acc, ranges) + jnp.dot(vt, w.astype(vt.dtype),
                                                       preferred_element_type=jnp.float32)
        out.append((carry, _scatter_lanes(acc, acc_new, ranges)))
    return tuple(out)


def _sb_kernel(q_ref, k_ref, vt_ref, g_ref, ut_ref, o_ref, *, tq, tk):
    heads, seq = q_ref.shape[0], q_ref.shape[1]
    ratio = tq // tk
    ut = ut_ref[...]

    for qi in range(seq // tq):
        r0 = qi * tq
        qs = [q_ref[g, r0:r0 + tq, :] for g in range(heads)]

        def block(c0, lane0, state, ranges, mask):
            return _sb_blocks([q[lane0:] for q in qs],
                              [k_ref[g, pl.ds(c0, tk), :] for g in range(heads)],
                              [vt_ref[g, :, pl.ds(c0, tk)] for g in range(heads)], ut, state, ranges, mask)

        init = (jnp.zeros((1, tq), jnp.float32), jnp.zeros((HEAD_DIM, tq), jnp.float32))
        state = (init,) * heads
        for d in reversed(range(ratio)):
            state = block(r0 + d * tk, d * tk, state, [(d * tk, tq)] if d else None,
                          _causal(tk, tq - d * tk, strict=True))
        n_full = qi * ratio
        state = lax.fori_loop(
            0, n_full,
            lambda i, st: block(pl.multiple_of((n_full - 1 - i) * tk, tk), 0, st, None, None), state)
        for g in range(heads):
            out = state[g][1].T * _silu(g_ref[g, r0:r0 + tq, :])
            o_ref[r0:r0 + tq, g * LANES:(g + 1) * LANES] = out.astype(o_ref.dtype)


def _sb_attention(qk, vt, gates, batch, seq, tq, tk, heads):
    q_s, k_s, v_s, g_s = _attn_specs(SB_Q0, SB_K0, SB_V0, GATE_SB0, seq, heads)
    sub = min(SB_SUB, tk)
    ut = (lax.broadcasted_iota(jnp.int32, (sub, sub), 1)
          > lax.broadcasted_iota(jnp.int32, (sub, sub), 0)).astype(jnp.bfloat16)
    return pl.pallas_call(
        functools.partial(_sb_kernel, tq=tq, tk=tk),
        grid=(batch, SB_HEADS // heads),
        in_specs=[q_s, k_s, v_s, g_s, pl.BlockSpec((sub, sub), lambda b, hg: (0, 0))],
        out_specs=pl.BlockSpec((seq, heads * LANES), lambda b, hg: (b, hg)),
        out_shape=jax.ShapeDtypeStruct((batch * seq, SB_W), jnp.bfloat16),
        compiler_params=_params("parallel", "parallel"),
        name="sb_attention",
    )(qk, qk, vt, gates, ut)


def _diff_kernel(lam_ref, gsub_ref, q_ref, k_ref, vt_ref, g_ref, o_ref, vta_ref, *, tq, tk, lam_init):
    heads, seq = q_ref.shape[0], q_ref.shape[1]
    ratio = tq // tk
    lp = lam_ref[...]
    lam = (jnp.exp(jnp.sum(lp[0:1] * lp[1:2], axis=1, keepdims=True))
           - jnp.exp(jnp.sum(lp[2:3] * lp[3:4], axis=1, keepdims=True)) + lam_init)
    key = lax.broadcasted_iota(jnp.int32, (tk, LANES), 0).astype(jnp.float32)
    ramps = []
    for g in range(heads):
        h = pl.program_id(1) * heads + g
        expo = jnp.full((tk, LANES), 127 - (8 // DIFF_HEADS) * (h + 1), jnp.int32)
        slope = lax.bitcast_convert_type(expo << 23, jnp.float32) * LOG2E
        ramps.append((slope, slope * key))
        vta_ref[g] = _with_ones_rows(vt_ref[g])
    lane = lax.broadcasted_iota(jnp.int32, (tq, HEAD_DIM), 1)

    for qi in range(seq // tq):
        r0 = qi * tq
        q1s, q2s = [], []
        for g in range(heads):
            q = q_ref[g, r0:r0 + tq, :]
            zero = jnp.zeros_like(q)
            q1s.append(jnp.where(lane < DIFF_QK_DIM, q, zero))
            q2s.append(jnp.where(lane >= DIFF_QK_DIM, q, zero))

        def scores(c0, lane0):
            out = []
            for g in range(heads):
                slope, ramp = ramps[g]
                bias = ramp + slope * jnp.asarray(c0 - r0, jnp.float32)
                qq = jnp.concatenate([q1s[g][lane0:], q2s[g][lane0:]], axis=0)
                out.append(_lane_tiles(_kq(k_ref[g, pl.ds(c0, tk), :], qq), lambda a, b=bias: a + b))
            return out

        def values(c0):
            return [vta_ref[g, :, pl.ds(c0, tk)] for g in range(heads)]

        def body(j, carry):
            c0 = pl.multiple_of(j * tk, tk)
            return _softmax_steps(scores(c0, 0), values(c0), carry)

        init = (jnp.full((1, 2 * tq), NEG_BIG, jnp.float32),
                jnp.zeros((HEAD_DIM + ONES_ROWS, 2 * tq), jnp.float32))
        carry = lax.fori_loop(0, qi * ratio, body, (init,) * heads)
        for d in range(ratio):
            ranges = [(d * tk, tq), (tq + d * tk, 2 * tq)] if d else None
            half = _causal(tk, tq - d * tk)
            carry = _softmax_steps(scores(r0 + d * tk, d * tk), values(r0 + d * tk), carry, ranges,
                                   jnp.concatenate([half, half], axis=1))
        for g in range(heads):
            o = _normalised(carry[g][1])
            o = o[:, 0:tq] - lam * o[:, tq:2 * tq]
            o = o * lax.rsqrt(jnp.mean(o * o, axis=0, keepdims=True) + SUBLN_EPS)
            o = o.T * gsub_ref[...] * (1.0 - lam_init)
            out = o * _silu(g_ref[g, r0:r0 + tq, :])
            o_ref[r0:r0 + tq, g * LANES:(g + 1) * LANES] = out.astype(o_ref.dtype)


def _diff_attention(qk, vt, gates, lam_p, subln_row, batch, seq, tq, tk, lam_init, heads):
    q_s, k_s, v_s, g_s = _attn_specs(DIFF_Q0, DIFF_K0, DIFF_V0, GATE_DIFF0, seq, heads)
    return pl.pallas_call(
        functools.partial(_diff_kernel, tq=tq, tk=tk, lam_init=lam_init),
        grid=(batch, DIFF_HEADS // heads),
        in_specs=[pl.BlockSpec(lam_p.shape, lambda b, hg: (0, 0)),
                  pl.BlockSpec((1, HEAD_DIM), lambda b, hg: (0, 0)),
                  q_s, k_s, v_s, g_s],
        out_specs=pl.BlockSpec((seq, heads * LANES), lambda b, hg: (b, hg)),
        out_shape=jax.ShapeDtypeStruct((batch * seq, DIFF_W), jnp.bfloat16),
        scratch_shapes=[pltpu.VMEM((heads, HEAD_DIM + ONES_ROWS, seq), jnp.bfloat16)],
        compiler_params=_params("parallel", "parallel"),
        name="diff_attention",
    )(lam_p, subln_row, qk, qk, vt, gates)


def _merge_kernel(of_ref, os_ref, od_ref, wf_ref, ws_ref, wd_ref, x_ref, g_ref, b_ref, y_ref, yb_ref,
                  even_ref, odd_ref):
    i = pl.program_id(0)

    @pl.when(i == 0)
    def _():
        odd_ref[...] = jnp.zeros(odd_ref.shape, odd_ref.dtype)

    def step(prev_ref, cur_ref):
        d = cur_ref.shape[1]
        half = d // 2
        z = DEEPNORM_ALPHA * x_ref[...] + prev_ref[...]
        mu = jnp.mean(z, axis=1, keepdims=True)
        zc = z - mu
        var = jnp.mean(zc * zc, axis=1, keepdims=True)
        out = zc * lax.rsqrt(var + LN_EPS) * g_ref[...] + b_ref[...]
        y_ref[...] = out
        yb_ref[...] = out.astype(yb_ref.dtype)
        cur_ref[:, 0:half] = (
            jnp.dot(of_ref[...], wf_ref[:, 0:half], preferred_element_type=jnp.float32)
            + jnp.dot(os_ref[...], ws_ref[:, 0:half], preferred_element_type=jnp.float32)
            + jnp.dot(od_ref[...], wd_ref[:, 0:half], preferred_element_type=jnp.float32))
        bits = lax.bitcast_convert_type(out[:, 0:of_ref.shape[1]], jnp.uint32)
        zero = lax.bitcast_convert_type((bits >> 16) >> 16, jnp.float32).astype(of_ref.dtype)
        cur_ref[:, half:d] = (
            jnp.dot(of_ref[...] + zero, wf_ref[:, half:d], preferred_element_type=jnp.float32)
            + jnp.dot(os_ref[...], ws_ref[:, half:d], preferred_element_type=jnp.float32)
            + jnp.dot(od_ref[...], wd_ref[:, half:d], preferred_element_type=jnp.float32))

    @pl.when(i % 2 == 0)
    def _():
        step(odd_ref, even_ref)

    @pl.when(i % 2 == 1)
    def _():
        step(even_ref, odd_ref)


def _merge(o_fox, o_sb, o_diff, w_out, layer, x, ln_g, ln_b, tm):
    t, d = x.shape
    n = t // tm
    ahead = lambda w: pl.BlockSpec((tm, w), lambda i: (jnp.minimum(i, n - 1), 0))
    behind = lambda: pl.BlockSpec((tm, d), lambda i: (jnp.maximum(i - 1, 0), 0))
    w_rows = lambda rows, start: pl.BlockSpec((None, rows, d), lambda i: (layer, start // rows, 0))
    vec = lambda: pl.BlockSpec((None, 1, d), lambda i: (layer, 0, 0))
    return pl.pallas_call(
        _merge_kernel,
        grid=(n + 1,),
        in_specs=[ahead(FOX_W), ahead(SB_W), ahead(DIFF_W),
                  w_rows(FOX_W, 0), w_rows(SB_W, FOX_W), w_rows(DIFF_W, FOX_W + SB_W),
                  behind(), vec(), vec()],
        out_specs=[behind(), behind()],
        out_shape=[jax.ShapeDtypeStruct((t, d), jnp.float32),
                   jax.ShapeDtypeStruct((t, d), jnp.bfloat16)],
        scratch_shapes=[pltpu.VMEM((tm, d), jnp.float32), pltpu.VMEM((tm, d), jnp.float32)],
        compiler_params=_params("arbitrary"),
        name="merge_layernorm",
    )(o_fox, o_sb, o_diff, w_out, w_out, w_out, x, ln_g, ln_b)


PREP_UNIT = 256
QK_COLS = 2 * (FOX_W + SB_W + DIFF_W)
V_COLS = FOX_W + SB_W + DIFF_W
GATE_COLS = FOX_W + SB_W + DIFF_W


def _cast_kernel(w_ref, o_ref):
    o_ref[...] = w_ref[...].astype(o_ref.dtype)


def _permuted_in_weights(w_in):
    depth, k, _ = w_in.shape
    sizes = (FOX_W,) * 4 + (SB_W,) * 4 + (DIFF_W,) * 4
    starts = [sum(sizes[:i]) for i in range(len(sizes))]
    fq, fk, fv, fg, sq, sk, sv, sg, dq, dk, dv, dg = range(12)
    order = [fq, fk, sq, sk, dq, dk, fv, sv, dv, fg, sg, dg]
    steps, dst, shift = [], 0, 0
    for grp in order:
        new_shift = (starts[grp] - dst) // PREP_UNIT
        steps.append((dst // PREP_UNIT, new_shift - shift))
        shift = new_shift
        dst += sizes[grp]
    n_units = dst // PREP_UNIT

    def src_unit(j):
        return j + sum(jnp.where(j >= first, delta, 0) for first, delta in steps)

    return pl.pallas_call(
        _cast_kernel,
        grid=(depth, n_units),
        in_specs=[pl.BlockSpec((None, k, PREP_UNIT), lambda l, j: (l, 0, src_unit(j)))],
        out_specs=pl.BlockSpec((None, k, PREP_UNIT), lambda l, j: (l, 0, j)),
        out_shape=jax.ShapeDtypeStruct((depth, k, n_units * PREP_UNIT), jnp.bfloat16),
        compiler_params=_params("parallel", "parallel"),
        name="permute_in_weights",
    )(w_in)


def _forget_in_weights(w_in):
    ff = w_in[:, :, QK_COLS + V_COLS + GATE_COLS:]
    return jnp.pad(ff, ((0, 0), (0, 0), (0, LANES - FOX_HEADS))).astype(jnp.bfloat16)


def _qk_col_scale():
    one = lambda n: jnp.ones((n,), jnp.float32)
    full = lambda n, v: jnp.full((n,), v, jnp.float32)
    return jnp.concatenate([
        full(FOX_W, HEAD_DIM ** -0.5 * LOG2E), one(FOX_W),
        full(SB_W, HEAD_DIM ** -0.5 * LOG2E), one(SB_W),
        full(DIFF_W, DIFF_QK_DIM ** -0.5 * LOG2E), one(DIFF_W)])[None, :]


def kernel(x, w_in, b_f, diff_lambda, diff_subln_g, w_out, ln_g, ln_b):
    batch, seq, d_model = x.shape
    depth = w_in.shape[0]
    t = batch * seq
    tk = min(seq, KEY_BLOCK)
    tq = min(seq, ATTN_TQ)

    w_in_b = w_in.astype(jnp.bfloat16)
    w_perm = _permuted_in_weights(w_in_b)
    w_ff = _forget_in_weights(w_in_b)
    w_out_b = w_out.astype(jnp.bfloat16)
    qk_scale = _qk_col_scale()
    b_f_rows = jnp.pad(b_f, ((0, 0), (0, LANES - FOX_HEADS)))
    ln_g3 = ln_g.reshape(depth, 1, d_model)
    ln_b3 = ln_b.reshape(depth, 1, d_model)

    xf = x.reshape(t, d_model)
    xb = xf
    for l in range(depth):
        lam_init = 0.8 - 0.6 * math.exp(-0.3 * l)
        gates, xb = _project_gates(xb, w_perm, w_ff, l, QK_COLS + V_COLS, min(t, WIDE_ROWS))
        tm = min(t, PROJ_ROWS)
        qk = _project(xb, w_perm, l, 0, QK_COLS, qk_scale, jnp.bfloat16, tm, PROJ_COLS, "qk_projection")
        vt = _project(xb, w_perm, l, QK_COLS, V_COLS, None, jnp.bfloat16, tm, PROJ_COLS, "v_projection",
                      transpose_out=True)
        c = _forget_prefix(gates, b_f_rows[l:l + 1], batch, seq, tk)
        o_fox = _fox_attention(qk, vt, gates, c, batch, seq, tq, tk, FOX_GROUP)
        o_sb = _sb_attention(qk, vt, gates, batch, seq, tq, tk, SB_GROUP)
        o_diff = _diff_attention(qk, vt, gates, diff_lambda[l], diff_subln_g[l:l + 1], batch, seq, tq, tk,
                                 lam_init, DIFF_GROUP)
        xf, xb = _merge(o_fox, o_sb, o_diff, w_out_b, l, xf, ln_g3, ln_b3, min(t, WIDE_ROWS))
    return xf.reshape(batch, seq, d_model)
```

```python
import functools
import math

import jax
import jax.numpy as jnp
from jax import lax
from jax.experimental import pallas as pl
from jax.experimental.pallas import tpu as pltpu

HEAD_DIM = 128
FOX_HEADS = 6
SB_HEADS = 6
DIFF_HEADS = 4
DIFF_QK_DIM = HEAD_DIM // 2
FOX_W = FOX_HEADS * HEAD_DIM
SB_W = SB_HEADS * HEAD_DIM
DIFF_W = DIFF_HEADS * HEAD_DIM
DEPTH_FOR_DEEPNORM = 4
DEEPNORM_ALPHA = (2 * DEPTH_FOR_DEEPNORM) ** 0.25
LN_EPS = 1e-5
SUBLN_EPS = 1e-5
NEG_BIG = -1e30
LOG2E = math.log2(math.e)

LANES = 128
VMEM_LIMIT_BYTES = 56 * 1024 * 1024

FOX_Q0, FOX_K0 = 0, FOX_HEADS
SB_Q0, SB_K0 = 2 * FOX_HEADS, 2 * FOX_HEADS + SB_HEADS
DIFF_Q0, DIFF_K0 = 2 * (FOX_HEADS + SB_HEADS), 2 * (FOX_HEADS + SB_HEADS) + DIFF_HEADS
FOX_V0, SB_V0, DIFF_V0 = 0, FOX_HEADS, FOX_HEADS + SB_HEADS
GATE_FOX0, GATE_SB0, GATE_DIFF0 = 0, FOX_HEADS, FOX_HEADS + SB_HEADS
GATE_FF = FOX_HEADS + SB_HEADS + DIFF_HEADS
GATE_BLOCKS = GATE_FF + 1
FOX_GROUP, SB_GROUP, DIFF_GROUP = 3, 2, 2
ATTN_TQ = 2048
KEY_BLOCK = 256
SB_SUB = 256
PROJ_ROWS = 2048
PROJ_COLS = 1024
GATE_ROWS_F32, GATE_ROWS_BF16 = 512, 1024
MERGE_ROWS = 512


def _params(*semantics):
    return pltpu.CompilerParams(dimension_semantics=semantics, vmem_limit_bytes=VMEM_LIMIT_BYTES)


def _proj_kernel(x_ref, w_ref, *rest, transpose_out):
    o_ref = rest[-1]
    acc = jnp.dot(x_ref[...].astype(w_ref.dtype), w_ref[...], preferred_element_type=jnp.float32)
    if len(rest) == 2:
        acc = acc * rest[0][...]
    for c in range(o_ref.shape[0]):
        blk = acc[:, c * LANES:(c + 1) * LANES]
        o_ref[c] = (blk.T if transpose_out else blk).astype(o_ref.dtype)


def _gate_proj_kernel(x_ref, w_ref, wf_ref, o_ref, *xb_ref):
    x = x_ref[...].astype(w_ref.dtype)
    if xb_ref:
        xb_ref[0][...] = x
    acc = jnp.dot(x, w_ref[...], preferred_element_type=jnp.float32)
    for c in range(GATE_FF):
        o_ref[c] = acc[:, c * LANES:(c + 1) * LANES]
    o_ref[GATE_FF] = jnp.dot(x, wf_ref[...], preferred_element_type=jnp.float32)


def _project_gates(x, w, w_ff, layer, col0, tm):
    t, k = x.shape
    assert t % tm == 0 and col0 % GATE_COLS == 0, (t, tm, col0)
    out_specs = [pl.BlockSpec((GATE_BLOCKS, tm, LANES), lambda i: (0, i, 0))]
    out_shape = [jax.ShapeDtypeStruct((GATE_BLOCKS, t, LANES), jnp.float32)]
    if x.dtype != w.dtype:
        out_specs.append(pl.BlockSpec((tm, k), lambda i: (i, 0)))
        out_shape.append(jax.ShapeDtypeStruct((t, k), w.dtype))
    out = pl.pallas_call(
        _gate_proj_kernel,
        grid=(t // tm,),
        in_specs=[pl.BlockSpec((tm, k), lambda i: (i, 0)),
                  pl.BlockSpec((None, k, GATE_COLS), lambda i: (layer, 0, col0 // GATE_COLS)),
                  pl.BlockSpec((None, k, LANES), lambda i: (layer, 0, 0))],
        out_specs=out_specs,
        out_shape=out_shape,
        compiler_params=_params("parallel"),
        name="gate_projection",
    )(x, w, w_ff)
    return (out[0], out[1]) if len(out) == 2 else (out[0], x)


def _project(x, w, layer, col0, n, col_scale, out_dtype, tm, tn, name, transpose_out=False):
    t, k = x.shape
    assert t % tm == 0 and n % tn == 0 and col0 % tn == 0, (t, tm, n, tn, col0)
    if transpose_out:
        out_spec = pl.BlockSpec((tn // LANES, LANES, tm), lambda i, j: (j, 0, i))
        out_shape = (n // LANES, LANES, t)
    else:
        out_spec = pl.BlockSpec((tn // LANES, tm, LANES), lambda i, j: (j, i, 0))
        out_shape = (n // LANES, t, LANES)
    in_specs = [pl.BlockSpec((tm, k), lambda i, j: (i, 0)),
                pl.BlockSpec((None, k, tn), lambda i, j: (layer, 0, col0 // tn + j))]
    operands = [x, w]
    if col_scale is not None:
        in_specs.append(pl.BlockSpec((1, tn), lambda i, j: (0, j)))
        operands.append(col_scale)
    return pl.pallas_call(
        functools.partial(_proj_kernel, transpose_out=transpose_out),
        grid=(t // tm, n // tn),
        in_specs=in_specs,
        out_specs=out_spec,
        out_shape=jax.ShapeDtypeStruct(out_shape, out_dtype),
        compiler_params=_params("parallel", "parallel"),
        name=name,
    )(*operands)


def _log_sigmoid(z):
    return jnp.minimum(z, 0.0) - jnp.log1p(jnp.exp(-jnp.abs(z)))


def _split3(x):
    x1 = x.astype(jnp.bfloat16)
    r = x - x1.astype(jnp.float32)
    x2 = r.astype(jnp.bfloat16)
    x3 = (r - x2.astype(jnp.float32)).astype(jnp.bfloat16)
    return x1, x2, x3


def _decay_kernel(ff_ref, bf_ref, tri_ref, c_ref, *, chunk):
    s = ff_ref.shape[1]
    tri = tri_ref[...]
    carry = jnp.zeros((1, LANES), jnp.float32)
    for c in range(s // chunk):
        lf = _log_sigmoid(ff_ref[0, c * chunk:(c + 1) * chunk, :] + bf_ref[...])
        x1, x2, x3 = _split3(lf)
        cs = (jnp.dot(tri, x1, preferred_element_type=jnp.float32)
              + jnp.dot(tri, x2, preferred_element_type=jnp.float32)
              + jnp.dot(tri, x3, preferred_element_type=jnp.float32)) + carry
        c_ref[0, c * chunk:(c + 1) * chunk, :] = cs
        carry = cs[chunk - 1:chunk, :]


def _forget_prefix(gates, b_f_row, batch, seq, chunk):
    tri = (lax.broadcasted_iota(jnp.int32, (chunk, chunk), 0)
           >= lax.broadcasted_iota(jnp.int32, (chunk, chunk), 1)).astype(jnp.bfloat16)
    return pl.pallas_call(
        functools.partial(_decay_kernel, chunk=chunk),
        grid=(batch,),
        in_specs=[
            pl.BlockSpec((1, seq, LANES), lambda b: (GATE_FF, b, 0)),
            pl.BlockSpec((1, LANES), lambda b: (0, 0)),
            pl.BlockSpec((chunk, chunk), lambda b: (0, 0)),
        ],
        out_specs=pl.BlockSpec((1, seq, LANES), lambda b: (b, 0, 0)),
        out_shape=jax.ShapeDtypeStruct((batch, seq, LANES), jnp.float32),
        compiler_params=_params("parallel"),
        name="forget_prefix",
    )(gates, b_f_row, tri)


def _kq(k, q):
    return lax.dot_general(k, q, (((1,), (1,)), ((), ())), preferred_element_type=jnp.float32)


def _silu(g):
    return g * (1.0 / (1.0 + jnp.exp(-g)))


def _lane_tiles(a, tile):
    n = a.shape[1] // LANES
    return jnp.concatenate([tile(a[:, i * LANES:(i + 1) * LANES]) for i in range(n)], axis=1)


def _gather_lanes(x, ranges):
    if ranges is None:
        return x
    return jnp.concatenate([x[:, a:b] for a, b in ranges], axis=1)


def _scatter_lanes(x, new, ranges):
    if ranges is None:
        return new
    pieces, pos, off = [], 0, 0
    for a, b in ranges:
        if a > pos:
            pieces.append(x[:, pos:a])
        pieces.append(new[:, off:off + b - a])
        off += b - a
        pos = b
    if pos < x.shape[1]:
        pieces.append(x[:, pos:])
    return jnp.concatenate(pieces, axis=1)


ONES_ROWS = 16


def _with_ones_rows(vt):
    ones = (lax.broadcasted_iota(jnp.int32, (ONES_ROWS, vt.shape[1]), 0) == 0).astype(vt.dtype)
    return jnp.concatenate([vt, ones], axis=0)


def _softmax_steps(sts, vts, carry, ranges=None, mask=None):
    stats = []
    for st, (m, _) in zip(sts, carry):
        if mask is not None:
            st = jnp.where(mask, st, NEG_BIG)
        m_old = _gather_lanes(m, ranges)
        m_new = jnp.maximum(m_old, jnp.max(st, axis=0, keepdims=True))
        stats.append((m_new, jnp.exp2(m_old - m_new), jnp.exp2(st - m_new)))
    out = []
    for vt, (m, acc), (m_new, alpha, p) in zip(vts, carry, stats):
        acc_new = alpha * _gather_lanes(acc, ranges) + jnp.dot(
            vt, p.astype(vt.dtype), preferred_element_type=jnp.float32)
        out.append((_scatter_lanes(m, m_new, ranges), _scatter_lanes(acc, acc_new, ranges)))
    return tuple(out)


def _normalised(acc):
    return acc[0:HEAD_DIM] * (1.0 / acc[HEAD_DIM:HEAD_DIM + 1])


def _attn_specs(q0, k0, v0, g0, seq, heads):
    blk = lambda base: pl.BlockSpec((heads, seq, LANES), lambda b, hg: (base // heads + hg, b, 0))
    vt = pl.BlockSpec((heads, LANES, seq), lambda b, hg: (v0 // heads + hg, 0, b))
    return blk(q0), blk(k0), vt, blk(g0)


def _causal(tk, width, strict=False):
    rows = lax.broadcasted_iota(jnp.int32, (tk, width), 0)
    cols = lax.broadcasted_iota(jnp.int32, (tk, width), 1)
    return cols > rows if strict else cols >= rows


def _fox_kernel(q_ref, k_ref, vt_ref, g_ref, c_ref, o_ref, bias_ref, vta_ref, *, tq, tk):
    heads, seq = q_ref.shape[0], q_ref.shape[1]
    ratio = tq // tk
    lane = lax.broadcasted_iota(jnp.int32, (seq, LANES), 1)
    for g in range(heads):
        h = pl.program_id(1) * heads + g
        col = jnp.sum(jnp.where(lane == h, c_ref[0], 0.0), axis=1, keepdims=True)
        bias_ref[g] = jnp.broadcast_to(col * (-LOG2E), (seq, LANES))
        vta_ref[g] = _with_ones_rows(vt_ref[g])

    for qi in range(seq // tq):
        r0 = qi * tq
        qs = [q_ref[g, r0:r0 + tq, :] for g in range(heads)]

        def scores(c0, lane0):
            return [_lane_tiles(_kq(k_ref[g, pl.ds(c0, tk), :], qs[g][lane0:]),
                                lambda a, g=g: a + bias_ref[g, pl.ds(c0, tk), :])
                    for g in range(heads)]

        def values(c0):
            return [vta_ref[g, :, pl.ds(c0, tk)] for g in range(heads)]

        def body(j, carry):
            c0 = pl.multiple_of(j * tk, tk)
            return _softmax_steps(scores(c0, 0), values(c0), carry)

        init = (jnp.full((1, tq), NEG_BIG, jnp.float32),
                jnp.zeros((HEAD_DIM + ONES_ROWS, tq), jnp.float32))
        carry = lax.fori_loop(0, qi * ratio, body, (init,) * heads)
        for d in range(ratio):
            ranges = [(d * tk, tq)] if d else None
            carry = _softmax_steps(scores(r0 + d * tk, d * tk), values(r0 + d * tk), carry, ranges,
                                   _causal(tk, tq - d * tk))
        for g in range(heads):
            out = _normalised(carry[g][1]).T * _silu(g_ref[g, r0:r0 + tq, :])
            o_ref[r0:r0 + tq, g * LANES:(g + 1) * LANES] = out.astype(o_ref.dtype)


def _fox_attention(qk, vt, gates, c, batch, seq, tq, tk, heads):
    q_s, k_s, v_s, g_s = _attn_specs(FOX_Q0, FOX_K0, FOX_V0, GATE_FOX0, seq, heads)
    return pl.pallas_call(
        functools.partial(_fox_kernel, tq=tq, tk=tk),
        grid=(batch, FOX_HEADS // heads),
        in_specs=[q_s, k_s, v_s, g_s, pl.BlockSpec((1, seq, LANES), lambda b, hg: (b, 0, 0))],
        out_specs=pl.BlockSpec((seq, heads * LANES), lambda b, hg: (b, hg)),
        out_shape=jax.ShapeDtypeStruct((batch * seq, FOX_W), jnp.bfloat16),
        scratch_shapes=[pltpu.VMEM((heads, seq, LANES), jnp.float32),
                        pltpu.VMEM((heads, HEAD_DIM + ONES_ROWS, seq), jnp.bfloat16)],
        compiler_params=_params("parallel", "parallel"),
        name="fox_attention",
    )(qk, qk, vt, gates, c)


def _sb_blocks(qs, ks, vts, ut, states, ranges, strict):
    tk = ks[0].shape[0]
    sub_keys = ut.shape[0]
    zs = [_kq(k, q) for k, q in zip(ks, qs)]
    parts = []
    for z in zs:
        neg_abs = lax.bitcast_convert_type(
            lax.bitcast_convert_type(z, jnp.uint32) | jnp.uint32(0x80000000), jnp.float32)
        sp = jnp.log2(1.0 + jnp.exp2(neg_abs))
        log_beta = jnp.minimum(z, 0.0) - sp
        log_keep = log_beta - z
        if strict is not None:
            log_keep = jnp.where(strict, log_keep, 0.0)
        parts.append((log_beta, log_keep, log_keep.astype(jnp.bfloat16)))
    subs = []
    for _, _, terms in parts:
        subs.append([jnp.dot(ut, terms[i:i + sub_keys], preferred_element_type=jnp.float32)
                     for i in range(0, tk, sub_keys)])
    ws, carries = [], []
    for (log_beta, log_keep, _), sub, (carry_all, _) in zip(parts, subs, states):
        carry = _gather_lanes(carry_all, ranges)
        laters = [None] * len(sub)
        for n in reversed(range(len(sub))):
            laters[n] = sub[n] + carry
            carry = carry + sub[n][0:1] + log_keep[n * sub_keys:n * sub_keys + 1]
        w = jnp.exp2(log_beta + jnp.concatenate(laters, axis=0))
        if strict is not None:
            w = jnp.where(strict, w, 0.0)
        ws.append(w)
        carries.append(_scatter_lanes(carry_all, carry, ranges))
    out = []
    for w, vt, carry, (_, acc) in zip(ws, vts, carries, states):
        acc_new = _gather_lanes(acc, ranges) + jnp.dot(vt, w.astype(vt.dtype),
                                                       preferred_element_type=jnp.float32)
        out.append((carry, _scatter_lanes(acc, acc_new, ranges)))
    return tuple(out)


def _sb_kernel(q_ref, k_ref, vt_ref, g_ref, ut_ref, o_ref, *, tq, tk):
    heads, seq = q_ref.shape[0], q_ref.shape[1]
    ratio = tq // tk
    ut = ut_ref[...]

    for qi in range(seq // tq):
        r0 = qi * tq
        qs = [q_ref[g, r0:r0 + tq, :] for g in range(heads)]

        def block(c0, lane0, state, ranges, mask):
            return _sb_blocks([q[lane0:] for q in qs],
                              [k_ref[g, pl.ds(c0, tk), :] for g in range(heads)],
                              [vt_ref[g, :, pl.ds(c0, tk)] for g in range(heads)], ut, state, ranges, mask)

        init = (jnp.zeros((1, tq), jnp.float32), jnp.zeros((HEAD_DIM, tq), jnp.float32))
        state = (init,) * heads
        for d in reversed(range(ratio)):
            state = block(r0 + d * tk, d * tk, state, [(d * tk, tq)] if d else None,
                          _causal(tk, tq - d * tk, strict=True))
        n_full = qi * ratio
        state = lax.fori_loop(
            0, n_full,
            lambda i, st: block(pl.multiple_of((n_full - 1 - i) * tk, tk), 0, st, None, None), state)
        for g in range(heads):
            out = state[g][1].T * _silu(g_ref[g, r0:r0 + tq, :])
            o_ref[r0:r0 + tq, g * LANES:(g + 1) * LANES] = out.astype(o_ref.dtype)


def _sb_attention(qk, vt, gates, batch, seq, tq, tk, heads):
    q_s, k_s, v_s, g_s = _attn_specs(SB_Q0, SB_K0, SB_V0, GATE_SB0, seq, heads)
    sub = min(SB_SUB, tk)
    ut = (lax.broadcasted_iota(jnp.int32, (sub, sub), 1)
          > lax.broadcasted_iota(jnp.int32, (sub, sub), 0)).astype(jnp.bfloat16)
    return pl.pallas_call(
        functools.partial(_sb_kernel, tq=tq, tk=tk),
        grid=(batch, SB_HEADS // heads),
        in_specs=[q_s, k_s, v_s, g_s, pl.BlockSpec((sub, sub), lambda b, hg: (0, 0))],
        out_specs=pl.BlockSpec((seq, heads * LANES), lambda b, hg: (b, hg)),
        out_shape=jax.ShapeDtypeStruct((batch * seq, SB_W), jnp.bfloat16),
        compiler_params=_params("parallel", "parallel"),
        name="sb_attention",
    )(qk, qk, vt, gates, ut)


def _diff_kernel(lam_ref, gsub_ref, q_ref, k_ref, vt_ref, g_ref, o_ref, vta_ref, *, tq, tk, lam_init):
    heads, seq = q_ref.shape[0], q_ref.shape[1]
    ratio = tq // tk
    lp = lam_ref[...]
    lam = (jnp.exp(jnp.sum(lp[0:1] * lp[1:2], axis=1, keepdims=True))
           - jnp.exp(jnp.sum(lp[2:3] * lp[3:4], axis=1, keepdims=True)) + lam_init)
    key = lax.broadcasted_iota(jnp.int32, (tk, LANES), 0).astype(jnp.float32)
    ramps = []
    for g in range(heads):
        h = pl.program_id(1) * heads + g
        expo = jnp.full((tk, LANES), 127 - (8 // DIFF_HEADS) * (h + 1), jnp.int32)
        slope = lax.bitcast_convert_type(expo << 23, jnp.float32) * LOG2E
        ramps.append((slope, slope * key))
        vta_ref[g] = _with_ones_rows(vt_ref[g])
    lane = lax.broadcasted_iota(jnp.int32, (tq, HEAD_DIM), 1)

    for qi in range(seq // tq):
        r0 = qi * tq
        q1s, q2s = [], []
        for g in range(heads):
            q = q_ref[g, r0:r0 + tq, :]
            zero = jnp.zeros_like(q)
            q1s.append(jnp.where(lane < DIFF_QK_DIM, q, zero))
            q2s.append(jnp.where(lane >= DIFF_QK_DIM, q, zero))

        def scores(c0, lane0):
            out = []
            for g in range(heads):
                slope, ramp = ramps[g]
                bias = ramp + slope * jnp.asarray(c0 - r0, jnp.float32)
                qq = jnp.concatenate([q1s[g][lane0:], q2s[g][lane0:]], axis=0)
                out.append(_lane_tiles(_kq(k_ref[g, pl.ds(c0, tk), :], qq), lambda a, b=bias: a + b))
            return out

        def values(c0):
            return [vta_ref[g, :, pl.ds(c0, tk)] for g in range(heads)]

        def body(j, carry):
            c0 = pl.multiple_of(j * tk, tk)
            return _softmax_steps(scores(c0, 0), values(c0), carry)

        init = (jnp.full((1, 2 * tq), NEG_BIG, jnp.float32),
                jnp.zeros((HEAD_DIM + ONES_ROWS, 2 * tq), jnp.float32))
        carry = lax.fori_loop(0, qi * ratio, body, (init,) * heads)
        for d in range(ratio):
            ranges = [(d * tk, tq), (tq + d * tk, 2 * tq)] if d else None
            half = _causal(tk, tq - d * tk)
            carry = _softmax_steps(scores(r0 + d * tk, d * tk), values(r0 + d * tk), carry, ranges,
                                   jnp.concatenate([half, half], axis=1))
        for g in range(heads):
            o = _normalised(carry[g][1])
            o = o[:, 0:tq] - lam * o[:, tq:2 * tq]
            o = o * lax.rsqrt(jnp.mean(o * o, axis=0, keepdims=True) + SUBLN_EPS)
            o = o.T * gsub_ref[...] * (1.0 - lam_init)
            out = o * _silu(g_ref[g, r0:r0 + tq, :])
            o_ref[r0:r0 + tq, g * LANES:(g + 1) * LANES] = out.astype(o_ref.dtype)


def _diff_attention(qk, vt, gates, lam_p, subln_row, batch, seq, tq, tk, lam_init, heads):
    q_s, k_s, v_s, g_s = _attn_specs(DIFF_Q0, DIFF_K0, DIFF_V0, GATE_DIFF0, seq, heads)
    return pl.pallas_call(
        functools.partial(_diff_kernel, tq=tq, tk=tk, lam_init=lam_init),
        grid=(batch, DIFF_HEADS // heads),
        in_specs=[pl.BlockSpec(lam_p.shape, lambda b, hg: (0, 0)),
                  pl.BlockSpec((1, HEAD_DIM), lambda b, hg: (0, 0)),
                  q_s, k_s, v_s, g_s],
        out_specs=pl.BlockSpec((seq, heads * LANES), lambda b, hg: (b, hg)),
        out_shape=jax.ShapeDtypeStruct((batch * seq, DIFF_W), jnp.bfloat16),
        scratch_shapes=[pltpu.VMEM((heads, HEAD_DIM + ONES_ROWS, seq), jnp.bfloat16)],
        compiler_params=_params("parallel", "parallel"),
        name="diff_attention",
    )(lam_p, subln_row, qk, qk, vt, gates)


def _merge_kernel(of_ref, os_ref, od_ref, wf_ref, ws_ref, wd_ref, x_ref, g_ref, b_ref, y_ref, yb_ref):
    y = (jnp.dot(of_ref[...], wf_ref[...], preferred_element_type=jnp.float32)
         + jnp.dot(os_ref[...], ws_ref[...], preferred_element_type=jnp.float32)
         + jnp.dot(od_ref[...], wd_ref[...], preferred_element_type=jnp.float32))
    z = DEEPNORM_ALPHA * x_ref[...] + y
    mu = jnp.mean(z, axis=1, keepdims=True)
    zc = z - mu
    var = jnp.mean(zc * zc, axis=1, keepdims=True)
    out = zc * lax.rsqrt(var + LN_EPS) * g_ref[...] + b_ref[...]
    y_ref[...] = out
    yb_ref[...] = out.astype(yb_ref.dtype)


def _merge(o_fox, o_sb, o_diff, w_out, layer, x, ln_g, ln_b, tm):
    t, d = x.shape
    row = lambda w: pl.BlockSpec((tm, w), lambda i: (i, 0))
    w_rows = lambda rows, start: pl.BlockSpec((None, rows, d), lambda i: (layer, start // rows, 0))
    vec = lambda: pl.BlockSpec((None, 1, d), lambda i: (layer, 0, 0))
    return pl.pallas_call(
        _merge_kernel,
        grid=(t // tm,),
        in_specs=[row(FOX_W), row(SB_W), row(DIFF_W),
                  w_rows(FOX_W, 0), w_rows(SB_W, FOX_W), w_rows(DIFF_W, FOX_W + SB_W),
                  row(d), vec(), vec()],
        out_specs=[row(d), row(d)],
        out_shape=[jax.ShapeDtypeStruct((t, d), jnp.float32),
                   jax.ShapeDtypeStruct((t, d), jnp.bfloat16)],
        compiler_params=_params("parallel"),
        name="merge_layernorm",
    )(o_fox, o_sb, o_diff, w_out, w_out, w_out, x, ln_g, ln_b)


PREP_UNIT = 256
QK_COLS = 2 * (FOX_W + SB_W + DIFF_W)
V_COLS = FOX_W + SB_W + DIFF_W
GATE_COLS = FOX_W + SB_W + DIFF_W


def _cast_kernel(w_ref, o_ref):
    o_ref[...] = w_ref[...].astype(o_ref.dtype)


def _permuted_in_weights(w_in):
    depth, k, _ = w_in.shape
    sizes = (FOX_W,) * 4 + (SB_W,) * 4 + (DIFF_W,) * 4
    starts = [sum(sizes[:i]) for i in range(len(sizes))]
    fq, fk, fv, fg, sq, sk, sv, sg, dq, dk, dv, dg = range(12)
    order = [fq, fk, sq, sk, dq, dk, fv, sv, dv, fg, sg, dg]
    steps, dst, shift = [], 0, 0
    for grp in order:
        new_shift = (starts[grp] - dst) // PREP_UNIT
        steps.append((dst // PREP_UNIT, new_shift - shift))
        shift = new_shift
        dst += sizes[grp]
    n_units = dst // PREP_UNIT

    def src_unit(j):
        return j + sum(jnp.where(j >= first, delta, 0) for first, delta in steps)

    return pl.pallas_call(
        _cast_kernel,
        grid=(n_units,),
        in_specs=[pl.BlockSpec((depth, k, PREP_UNIT), lambda j: (0, 0, src_unit(j)))],
        out_specs=pl.BlockSpec((depth, k, PREP_UNIT), lambda j: (0, 0, j)),
        out_shape=jax.ShapeDtypeStruct((depth, k, n_units * PREP_UNIT), jnp.bfloat16),
        compiler_params=_params("parallel"),
        name="permute_in_weights",
    )(w_in)


def _forget_in_weights(w_in):
    ff = w_in[:, :, QK_COLS + V_COLS + GATE_COLS:]
    return jnp.pad(ff, ((0, 0), (0, 0), (0, LANES - FOX_HEADS))).astype(jnp.bfloat16)


def _qk_col_scale():
    one = lambda n: jnp.ones((n,), jnp.float32)
    full = lambda n, v: jnp.full((n,), v, jnp.float32)
    return jnp.concatenate([
        full(FOX_W, HEAD_DIM ** -0.5 * LOG2E), one(FOX_W),
        full(SB_W, HEAD_DIM ** -0.5 * LOG2E), one(SB_W),
        full(DIFF_W, DIFF_QK_DIM ** -0.5 * LOG2E), one(DIFF_W)])[None, :]


def kernel(x, w_in, b_f, diff_lambda, diff_subln_g, w_out, ln_g, ln_b):
    batch, seq, d_model = x.shape
    depth = w_in.shape[0]
    t = batch * seq
    tk = min(seq, KEY_BLOCK)
    tq = min(seq, ATTN_TQ)

    w_perm = _permuted_in_weights(w_in[:, :, :QK_COLS + V_COLS + GATE_COLS].astype(jnp.bfloat16))
    w_ff = _forget_in_weights(w_in)
    w_out_b = w_out.astype(jnp.bfloat16)
    qk_scale = _qk_col_scale()
    b_f_rows = jnp.pad(b_f, ((0, 0), (0, LANES - FOX_HEADS)))
    ln_g3 = ln_g.reshape(depth, 1, d_model)
    ln_b3 = ln_b.reshape(depth, 1, d_model)

    xf = x.reshape(t, d_model)
    xb = xf
    for l in range(depth):
        lam_init = 0.8 - 0.6 * math.exp(-0.3 * l)
        gate_rows = GATE_ROWS_F32 if xb.dtype == jnp.float32 else GATE_ROWS_BF16
        gates, xb = _project_gates(xb, w_perm, w_ff, l, QK_COLS + V_COLS, min(t, gate_rows))
        tm = min(t, PROJ_ROWS)
        qk = _project(xb, w_perm, l, 0, QK_COLS, qk_scale, jnp.bfloat16, tm, PROJ_COLS, "qk_projection")
        vt = _project(xb, w_perm, l, QK_COLS, V_COLS, None, jnp.bfloat16, tm, PROJ_COLS, "v_projection",
                      transpose_out=True)
        c = _forget_prefix(gates, b_f_rows[l:l + 1], batch, seq, tk)
        o_fox = _fox_attention(qk, vt, gates, c, batch, seq, tq, tk, FOX_GROUP)
        o_sb = _sb_attention(qk, vt, gates, batch, seq, tq, tk, SB_GROUP)
        o_diff = _diff_attention(qk, vt, gates, diff_lambda[l], diff_subln_g[l:l + 1], batch, seq, tq, tk,
                                 lam_init, DIFF_GROUP)
        xf, xb = _merge(o_fox, o_sb, o_diff, w_out_b, l, xf, ln_g3, ln_b3, min(t, MERGE_ROWS))
    return xf.reshape(batch, seq, d_model)
```

```python
import functools
import math

import jax
import jax.numpy as jnp
from jax import lax
from jax.experimental import pallas as pl
from jax.experimental.pallas import tpu as pltpu

HEAD_DIM = 128
FOX_HEADS = 6
SB_HEADS = 6
DIFF_HEADS = 4
DIFF_QK_DIM = HEAD_DIM // 2
FOX_W = FOX_HEADS * HEAD_DIM
SB_W = SB_HEADS * HEAD_DIM
DIFF_W = DIFF_HEADS * HEAD_DIM
DEPTH_FOR_DEEPNORM = 4
DEEPNORM_ALPHA = (2 * DEPTH_FOR_DEEPNORM) ** 0.25
LN_EPS = 1e-5
SUBLN_EPS = 1e-5
NEG_BIG = -1e30
LOG2E = math.log2(math.e)

LANES = 128
VMEM_LIMIT_BYTES = 56 * 1024 * 1024

FOX_Q0, FOX_K0 = 0, FOX_HEADS
SB_Q0, SB_K0 = 2 * FOX_HEADS, 2 * FOX_HEADS + SB_HEADS
DIFF_Q0, DIFF_K0 = 2 * (FOX_HEADS + SB_HEADS), 2 * (FOX_HEADS + SB_HEADS) + DIFF_HEADS
FOX_V0, SB_V0, DIFF_V0 = 0, FOX_HEADS, FOX_HEADS + SB_HEADS
GATE_FOX0, GATE_SB0, GATE_DIFF0 = 0, FOX_HEADS, FOX_HEADS + SB_HEADS
GATE_FF = FOX_HEADS + SB_HEADS + DIFF_HEADS
GATE_BLOCKS = GATE_FF + 1
FOX_GROUP, SB_GROUP, DIFF_GROUP = 3, 2, 2
ATTN_TQ = 2048
KEY_BLOCK = 256
SB_SUB = 256
PROJ_ROWS = 2048
PROJ_COLS = 1024
GATE_ROWS_F32, GATE_ROWS_BF16 = 512, 1024
MERGE_ROWS = 512


def _params(*semantics):
    return pltpu.CompilerParams(dimension_semantics=semantics, vmem_limit_bytes=VMEM_LIMIT_BYTES)


def _proj_kernel(x_ref, w_ref, *rest, transpose_out):
    o_ref = rest[-1]
    acc = jnp.dot(x_ref[...].astype(w_ref.dtype), w_ref[...], preferred_element_type=jnp.float32)
    if len(rest) == 2:
        acc = acc * rest[0][...]
    for c in range(o_ref.shape[0]):
        blk = acc[:, c * LANES:(c + 1) * LANES]
        o_ref[c] = (blk.T if transpose_out else blk).astype(o_ref.dtype)


def _gate_proj_kernel(x_ref, w_ref, wf_ref, bf_ref, tri_ref, o_ref, c_ref, *rest, tiles_per_seq):
    carry_ref = rest[-1]
    x = x_ref[...].astype(w_ref.dtype)
    if len(rest) == 2:
        rest[0][...] = x
    acc = jnp.dot(x, w_ref[...], preferred_element_type=jnp.float32)
    for c in range(GATE_FF):
        o_ref[c] = acc[:, c * LANES:(c + 1) * LANES]
    ff = jnp.dot(x, wf_ref[...], preferred_element_type=jnp.float32)

    @pl.when(pl.program_id(0) % tiles_per_seq == 0)
    def _():
        carry_ref[...] = jnp.zeros(carry_ref.shape, carry_ref.dtype)

    tri = tri_ref[...]
    chunk = tri.shape[0]
    carry = carry_ref[...]
    for c in range(ff.shape[0] // chunk):
        rows = slice(c * chunk, (c + 1) * chunk)
        x1, x2, x3 = _split3(_log_sigmoid(ff[rows] + bf_ref[...]))
        cs = (jnp.dot(tri, x1, preferred_element_type=jnp.float32)
              + jnp.dot(tri, x2, preferred_element_type=jnp.float32)
              + jnp.dot(tri, x3, preferred_element_type=jnp.float32)) + carry
        c_ref[rows, :] = cs
        carry = cs[chunk - 1:chunk, :]
    carry_ref[...] = carry


def _project_gates(x, w, w_ff, b_f_row, layer, col0, tm, seq, chunk):
    t, k = x.shape
    assert t % tm == 0 and seq % tm == 0 and tm % chunk == 0 and col0 % GATE_COLS == 0, (t, tm, seq, col0)
    tri = (lax.broadcasted_iota(jnp.int32, (chunk, chunk), 0)
           >= lax.broadcasted_iota(jnp.int32, (chunk, chunk), 1)).astype(jnp.bfloat16)
    out_specs = [pl.BlockSpec((GATE_FF, tm, LANES), lambda i: (0, i, 0)),
                 pl.BlockSpec((tm, LANES), lambda i: (i, 0))]
    out_shape = [jax.ShapeDtypeStruct((GATE_FF, t, LANES), jnp.float32),
                 jax.ShapeDtypeStruct((t, LANES), jnp.float32)]
    if x.dtype != w.dtype:
        out_specs.append(pl.BlockSpec((tm, k), lambda i: (i, 0)))
        out_shape.append(jax.ShapeDtypeStruct((t, k), w.dtype))
    out = pl.pallas_call(
        functools.partial(_gate_proj_kernel, tiles_per_seq=seq // tm),
        grid=(t // tm,),
        in_specs=[pl.BlockSpec((tm, k), lambda i: (i, 0)),
                  pl.BlockSpec((None, k, GATE_COLS), lambda i: (layer, 0, col0 // GATE_COLS)),
                  pl.BlockSpec((None, k, LANES), lambda i: (layer, 0, 0)),
                  pl.BlockSpec((1, LANES), lambda i: (0, 0)),
                  pl.BlockSpec((chunk, chunk), lambda i: (0, 0))],
        out_specs=out_specs,
        out_shape=out_shape,
        scratch_shapes=[pltpu.VMEM((1, LANES), jnp.float32)],
        compiler_params=_params("arbitrary"),
        name="gate_projection",
    )(x, w, w_ff, b_f_row, tri)
    return out[0], out[1], (out[2] if len(out) == 3 else x)


def _project(x, w, layer, col0, n, col_scale, out_dtype, tm, tn, name, transpose_out=False):
    t, k = x.shape
    assert t % tm == 0 and n % tn == 0 and col0 % tn == 0, (t, tm, n, tn, col0)
    if transpose_out:
        out_spec = pl.BlockSpec((tn // LANES, LANES, tm), lambda i, j: (j, 0, i))
        out_shape = (n // LANES, LANES, t)
    else:
        out_spec = pl.BlockSpec((tn // LANES, tm, LANES), lambda i, j: (j, i, 0))
        out_shape = (n // LANES, t, LANES)
    in_specs = [pl.BlockSpec((tm, k), lambda i, j: (i, 0)),
                pl.BlockSpec((None, k, tn), lambda i, j: (layer, 0, col0 // tn + j))]
    operands = [x, w]
    if col_scale is not None:
        in_specs.append(pl.BlockSpec((1, tn), lambda i, j: (0, j)))
        operands.append(col_scale)
    return pl.pallas_call(
        functools.partial(_proj_kernel, transpose_out=transpose_out),
        grid=(t // tm, n // tn),
        in_specs=in_specs,
        out_specs=out_spec,
        out_shape=jax.ShapeDtypeStruct(out_shape, out_dtype),
        compiler_params=_params("parallel", "parallel"),
        name=name,
    )(*operands)


def _log_sigmoid(z):
    return jnp.minimum(z, 0.0) - jnp.log1p(jnp.exp(-jnp.abs(z)))


def _split3(x):
    x1 = x.astype(jnp.bfloat16)
    r = x - x1.astype(jnp.float32)
    x2 = r.astype(jnp.bfloat16)
    x3 = (r - x2.astype(jnp.float32)).astype(jnp.bfloat16)
    return x1, x2, x3


def _kq(k, q):
    return lax.dot_general(k, q, (((1,), (1,)), ((), ())), preferred_element_type=jnp.float32)


def _silu(g):
    return g * (1.0 / (1.0 + jnp.exp(-g)))


def _lane_tiles(a, tile):
    n = a.shape[1] // LANES
    return jnp.concatenate([tile(a[:, i * LANES:(i + 1) * LANES]) for i in range(n)], axis=1)


def _gather_lanes(x, ranges):
    if ranges is None:
        return x
    return jnp.concatenate([x[:, a:b] for a, b in ranges], axis=1)


def _scatter_lanes(x, new, ranges):
    if ranges is None:
        return new
    pieces, pos, off = [], 0, 0
    for a, b in ranges:
        if a > pos:
            pieces.append(x[:, pos:a])
        pieces.append(new[:, off:off + b - a])
        off += b - a
        pos = b
    if pos < x.shape[1]:
        pieces.append(x[:, pos:])
    return jnp.concatenate(pieces, axis=1)


ONES_ROWS = 16


def _with_ones_rows(vt):
    ones = (lax.broadcasted_iota(jnp.int32, (ONES_ROWS, vt.shape[1]), 0) == 0).astype(vt.dtype)
    return jnp.concatenate([vt, ones], axis=0)


def _softmax_steps(sts, vts, carry, ranges=None, mask=None):
    stats = []
    for st, (m, _) in zip(sts, carry):
        if mask is not None:
            st = jnp.where(mask, st, NEG_BIG)
        m_old = _gather_lanes(m, ranges)
        m_new = jnp.maximum(m_old, jnp.max(st, axis=0, keepdims=True))
        stats.append((m_new, jnp.exp2(m_old - m_new), jnp.exp2(st - m_new)))
    out = []
    for vt, (m, acc), (m_new, alpha, p) in zip(vts, carry, stats):
        acc_new = alpha * _gather_lanes(acc, ranges) + jnp.dot(
            vt, p.astype(vt.dtype), preferred_element_type=jnp.float32)
        out.append((_scatter_lanes(m, m_new, ranges), _scatter_lanes(acc, acc_new, ranges)))
    return tuple(out)


def _normalised(acc):
    return acc[0:HEAD_DIM] * (1.0 / acc[HEAD_DIM:HEAD_DIM + 1])


def _attn_specs(q0, k0, v0, g0, seq, heads):
    blk = lambda base: pl.BlockSpec((heads, seq, LANES), lambda b, hg: (base // heads + hg, b, 0))
    vt = pl.BlockSpec((heads, LANES, seq), lambda b, hg: (v0 // heads + hg, 0, b))
    return blk(q0), blk(k0), vt, blk(g0)


def _causal(tk, width, strict=False):
    rows = lax.broadcasted_iota(jnp.int32, (tk, width), 0)
    cols = lax.broadcasted_iota(jnp.int32, (tk, width), 1)
    return cols > rows if strict else cols >= rows


def _fox_kernel(q_ref, k_ref, vt_ref, g_ref, c_ref, o_ref, bias_ref, vta_ref, *, tq, tk):
    heads, seq = q_ref.shape[0], q_ref.shape[1]
    ratio = tq // tk
    lane = lax.broadcasted_iota(jnp.int32, (seq, LANES), 1)
    for g in range(heads):
        h = pl.program_id(1) * heads + g
        col = jnp.sum(jnp.where(lane == h, c_ref[0], 0.0), axis=1, keepdims=True)
        bias_ref[g] = jnp.broadcast_to(col * (-LOG2E), (seq, LANES))
        vta_ref[g] = _with_ones_rows(vt_ref[g])

    for qi in range(seq // tq):
        r0 = qi * tq
        qs = [q_ref[g, r0:r0 + tq, :] for g in range(heads)]

        def scores(c0, lane0):
            return [_lane_tiles(_kq(k_ref[g, pl.ds(c0, tk), :], qs[g][lane0:]),
                                lambda a, g=g: a + bias_ref[g, pl.ds(c0, tk), :])
                    for g in range(heads)]

        def values(c0):
            return [vta_ref[g, :, pl.ds(c0, tk)] for g in range(heads)]

        def body(j, carry):
            c0 = pl.multiple_of(j * tk, tk)
            return _softmax_steps(scores(c0, 0), values(c0), carry)

        init = (jnp.full((1, tq), NEG_BIG, jnp.float32),
                jnp.zeros((HEAD_DIM + ONES_ROWS, tq), jnp.float32))
        carry = lax.fori_loop(0, qi * ratio, body, (init,) * heads)
        for d in range(ratio):
            ranges = [(d * tk, tq)] if d else None
            carry = _softmax_steps(scores(r0 + d * tk, d * tk), values(r0 + d * tk), carry, ranges,
                                   _causal(tk, tq - d * tk))
        for g in range(heads):
            out = _normalised(carry[g][1]).T * _silu(g_ref[g, r0:r0 + tq, :])
            o_ref[r0:r0 + tq, g * LANES:(g + 1) * LANES] = out.astype(o_ref.dtype)


def _fox_attention(qk, vt, gates, c, batch, seq, tq, tk, heads):
    q_s, k_s, v_s, g_s = _attn_specs(FOX_Q0, FOX_K0, FOX_V0, GATE_FOX0, seq, heads)
    return pl.pallas_call(
        functools.partial(_fox_kernel, tq=tq, tk=tk),
        grid=(batch, FOX_HEADS // heads),
        in_specs=[q_s, k_s, v_s, g_s, pl.BlockSpec((1, seq, LANES), lambda b, hg: (b, 0, 0))],
        out_specs=pl.BlockSpec((seq, heads * LANES), lambda b, hg: (b, hg)),
        out_shape=jax.ShapeDtypeStruct((batch * seq, FOX_W), jnp.bfloat16),
        scratch_shapes=[pltpu.VMEM((heads, seq, LANES), jnp.float32),
                        pltpu.VMEM((heads, HEAD_DIM + ONES_ROWS, seq), jnp.bfloat16)],
        compiler_params=_params("parallel", "parallel"),
        name="fox_attention",
    )(qk, qk, vt, gates, c)


def _sb_blocks(qs, ks, vts, ut, states, ranges, strict):
    tk = ks[0].shape[0]
    sub_keys = ut.shape[0]
    zs = [_kq(k, q) for k, q in zip(ks, qs)]
    parts = []
    for z in zs:
        neg_abs = lax.bitcast_convert_type(
            lax.bitcast_convert_type(z, jnp.uint32) | jnp.uint32(0x80000000), jnp.float32)
        sp = jnp.log2(1.0 + jnp.exp2(neg_abs))
        log_beta = jnp.minimum(z, 0.0) - sp
        log_keep = log_beta - z
        if strict is not None:
            log_keep = jnp.where(strict, log_keep, 0.0)
        parts.append((log_beta, log_keep, log_keep.astype(jnp.bfloat16)))
    subs = []
    for _, _, terms in parts:
        subs.append([jnp.dot(ut, terms[i:i + sub_keys], preferred_element_type=jnp.float32)
                     for i in range(0, tk, sub_keys)])
    ws, carries = [], []
    for (log_beta, log_keep, _), sub, (carry_all, _) in zip(parts, subs, states):
        carry = _gather_lanes(carry_all, ranges)
        laters = [None] * len(sub)
        for n in reversed(range(len(sub))):
            laters[n] = sub[n] + carry
            carry = carry + sub[n][0:1] + log_keep[n * sub_keys:n * sub_keys + 1]
        w = jnp.exp2(log_beta + jnp.concatenate(laters, axis=0))
        if strict is not None:
            w = jnp.where(strict, w, 0.0)
        ws.append(w)
        carries.append(_scatter_lanes(carry_all, carry, ranges))
    out = []
    for w, vt, carry, (_, acc) in zip(ws, vts, carries, states):
        acc_new = _gather_lanes(acc, ranges) + jnp.dot(vt, w.astype(vt.dtype),
                                                       preferred_element_type=jnp.float32)
        out.append((carry, _scatter_lanes(acc, acc_new, ranges)))
    return tuple(out)


def _sb_kernel(q_ref, k_ref, vt_ref, g_ref, ut_ref, o_ref, *, tq, tk):
    heads, seq = q_ref.shape[0], q_ref.shape[1]
    ratio = tq // tk
    ut = ut_ref[...]

    for qi in range(seq // tq):
        r0 = qi * tq
        qs = [q_ref[g, r0:r0 + tq, :] for g in range(heads)]

        def block(c0, lane0, state, ranges, mask):
            return _sb_blocks([q[lane0:] for q in qs],
                              [k_ref[g, pl.ds(c0, tk), :] for g in range(heads)],
                              [vt_ref[g, :, pl.ds(c0, tk)] for g in range(heads)], ut, state, ranges, mask)

        init = (jnp.zeros((1, tq), jnp.float32), jnp.zeros((HEAD_DIM, tq), jnp.float32))
        state = (init,) * heads
        for d in reversed(range(ratio)):
            state = block(r0 + d * tk, d * tk, state, [(d * tk, tq)] if d else None,
                          _causal(tk, tq - d * tk, strict=True))
        n_full = qi * ratio
        state = lax.fori_loop(
            0, n_full,
            lambda i, st: block(pl.multiple_of((n_full - 1 - i) * tk, tk), 0, st, None, None), state)
        for g in range(heads):
            out = state[g][1].T * _silu(g_ref[g, r0:r0 + tq, :])
            o_ref[r0:r0 + tq, g * LANES:(g + 1) * LANES] = out.astype(o_ref.dtype)


def _sb_attention(qk, vt, gates, batch, seq, tq, tk, heads):
    q_s, k_s, v_s, g_s = _attn_specs(SB_Q0, SB_K0, SB_V0, GATE_SB0, seq, heads)
    sub = min(SB_SUB, tk)
    ut = (lax.broadcasted_iota(jnp.int32, (sub, sub), 1)
          > lax.broadcasted_iota(jnp.int32, (sub, sub), 0)).astype(jnp.bfloat16)
    return pl.pallas_call(
        functools.partial(_sb_kernel, tq=tq, tk=tk),
        grid=(batch, SB_HEADS // heads),
        in_specs=[q_s, k_s, v_s, g_s, pl.BlockSpec((sub, sub), lambda b, hg: (0, 0))],
        out_specs=pl.BlockSpec((seq, heads * LANES), lambda b, hg: (b, hg)),
        out_shape=jax.ShapeDtypeStruct((batch * seq, SB_W), jnp.bfloat16),
        compiler_params=_params("parallel", "parallel"),
        name="sb_attention",
    )(qk, qk, vt, gates, ut)


def _diff_kernel(lam_ref, gsub_ref, q_ref, k_ref, vt_ref, g_ref, o_ref, vta_ref, *, tq, tk, lam_init):
    heads, seq = q_ref.shape[0], q_ref.shape[1]
    ratio = tq // tk
    lp = lam_ref[...]
    lam = (jnp.exp(jnp.sum(lp[0:1] * lp[1:2], axis=1, keepdims=True))
           - jnp.exp(jnp.sum(lp[2:3] * lp[3:4], axis=1, keepdims=True)) + lam_init)
    key = lax.broadcasted_iota(jnp.int32, (tk, LANES), 0).astype(jnp.float32)
    ramps = []
    for g in range(heads):
        h = pl.program_id(1) * heads + g
        expo = jnp.full((tk, LANES), 127 - (8 // DIFF_HEADS) * (h + 1), jnp.int32)
        slope = lax.bitcast_convert_type(expo << 23, jnp.float32) * LOG2E
        ramps.append((slope, slope * key))
        vta_ref[g] = _with_ones_rows(vt_ref[g])
    lane = lax.broadcasted_iota(jnp.int32, (tq, HEAD_DIM), 1)

    for qi in range(seq // tq):
        r0 = qi * tq
        q1s, q2s = [], []
        for g in range(heads):
            q = q_ref[g, r0:r0 + tq, :]
            zero = jnp.zeros_like(q)
            q1s.append(jnp.where(lane < DIFF_QK_DIM, q, zero))
            q2s.append(jnp.where(lane >= DIFF_QK_DIM, q, zero))

        def scores(c0, lane0):
            out = []
            for g in range(heads):
                slope, ramp = ramps[g]
                bias = ramp + slope * jnp.asarray(c0 - r0, jnp.float32)
                qq = jnp.concatenate([q1s[g][lane0:], q2s[g][lane0:]], axis=0)
                out.append(_lane_tiles(_kq(k_ref[g, pl.ds(c0, tk), :], qq), lambda a, b=bias: a + b))
            return out

        def values(c0):
            return [vta_ref[g, :, pl.ds(c0, tk)] for g in range(heads)]

        def body(j, carry):
            c0 = pl.multiple_of(j * tk, tk)
            return _softmax_steps(scores(c0, 0), values(c0), carry)

        init = (jnp.full((1, 2 * tq), NEG_BIG, jnp.float32),
                jnp.zeros((HEAD_DIM + ONES_ROWS, 2 * tq), jnp.float32))
        carry = lax.fori_loop(0, qi * ratio, body, (init,) * heads)
        for d in range(ratio):
            ranges = [(d * tk, tq), (tq + d * tk, 2 * tq)] if d else None
            half = _causal(tk, tq - d * tk)
            carry = _softmax_steps(scores(r0 + d * tk, d * tk), values(r0 + d * tk), carry, ranges,
                                   jnp.concatenate([half, half], axis=1))
        for g in range(heads):
            o = _normalised(carry[g][1])
            o = o[:, 0:tq] - lam * o[:, tq:2 * tq]
            o = o * lax.rsqrt(jnp.mean(o * o, axis=0, keepdims=True) + SUBLN_EPS)
            o = o.T * gsub_ref[...] * (1.0 - lam_init)
            out = o * _silu(g_ref[g, r0:r0 + tq, :])
            o_ref[r0:r0 + tq, g * LANES:(g + 1) * LANES] = out.astype(o_ref.dtype)


def _diff_attention(qk, vt, gates, lam_p, subln_row, batch, seq, tq, tk, lam_init, heads):
    q_s, k_s, v_s, g_s = _attn_specs(DIFF_Q0, DIFF_K0, DIFF_V0, GATE_DIFF0, seq, heads)
    return pl.pallas_call(
        functools.partial(_diff_kernel, tq=tq, tk=tk, lam_init=lam_init),
        grid=(batch, DIFF_HEADS // heads),
        in_specs=[pl.BlockSpec(lam_p.shape, lambda b, hg: (0, 0)),
                  pl.BlockSpec((1, HEAD_DIM), lambda b, hg: (0, 0)),
                  q_s, k_s, v_s, g_s],
        out_specs=pl.BlockSpec((seq, heads * LANES), lambda b, hg: (b, hg)),
        out_shape=jax.ShapeDtypeStruct((batch * seq, DIFF_W), jnp.bfloat16),
        scratch_shapes=[pltpu.VMEM((heads, HEAD_DIM + ONES_ROWS, seq), jnp.bfloat16)],
        compiler_params=_params("parallel", "parallel"),
        name="diff_attention",
    )(lam_p, subln_row, qk, qk, vt, gates)


def _merge_kernel(of_ref, os_ref, od_ref, wf_ref, ws_ref, wd_ref, x_ref, g_ref, b_ref, y_ref, yb_ref):
    y = (jnp.dot(of_ref[...], wf_ref[...], preferred_element_type=jnp.float32)
         + jnp.dot(os_ref[...], ws_ref[...], preferred_element_type=jnp.float32)
         + jnp.dot(od_ref[...], wd_ref[...], preferred_element_type=jnp.float32))
    z = DEEPNORM_ALPHA * x_ref[...] + y
    mu = jnp.mean(z, axis=1, keepdims=True)
    zc = z - mu
    var = jnp.mean(zc * zc, axis=1, keepdims=True)
    out = zc * lax.rsqrt(var + LN_EPS) * g_ref[...] + b_ref[...]
    y_ref[...] = out
    yb_ref[...] = out.astype(yb_ref.dtype)


def _merge(o_fox, o_sb, o_diff, w_out, layer, x, ln_g, ln_b, tm):
    t, d = x.shape
    row = lambda w: pl.BlockSpec((tm, w), lambda i: (i, 0))
    w_rows = lambda rows, start: pl.BlockSpec((None, rows, d), lambda i: (layer, start // rows, 0))
    vec = lambda: pl.BlockSpec((None, 1, d), lambda i: (layer, 0, 0))
    return pl.pallas_call(
        _merge_kernel,
        grid=(t // tm,),
        in_specs=[row(FOX_W), row(SB_W), row(DIFF_W),
                  w_rows(FOX_W, 0), w_rows(SB_W, FOX_W), w_rows(DIFF_W, FOX_W + SB_W),
                  row(d), vec(), vec()],
        out_specs=[row(d), row(d)],
        out_shape=[jax.ShapeDtypeStruct((t, d), jnp.float32),
                   jax.ShapeDtypeStruct((t, d), jnp.bfloat16)],
        compiler_params=_params("parallel"),
        name="merge_layernorm",
    )(o_fox, o_sb, o_diff, w_out, w_out, w_out, x, ln_g, ln_b)


PREP_UNIT = 256
QK_COLS = 2 * (FOX_W + SB_W + DIFF_W)
V_COLS = FOX_W + SB_W + DIFF_W
GATE_COLS = FOX_W + SB_W + DIFF_W


def _cast_kernel(w_ref, o_ref):
    o_ref[...] = w_ref[...].astype(o_ref.dtype)


def _permuted_in_weights(w_in):
    depth, k, _ = w_in.shape
    sizes = (FOX_W,) * 4 + (SB_W,) * 4 + (DIFF_W,) * 4
    starts = [sum(sizes[:i]) for i in range(len(sizes))]
    fq, fk, fv, fg, sq, sk, sv, sg, dq, dk, dv, dg = range(12)
    order = [fq, fk, sq, sk, dq, dk, fv, sv, dv, fg, sg, dg]
    steps, dst, shift = [], 0, 0
    for grp in order:
        new_shift = (starts[grp] - dst) // PREP_UNIT
        steps.append((dst // PREP_UNIT, new_shift - shift))
        shift = new_shift
        dst += sizes[grp]
    n_units = dst // PREP_UNIT

    def src_unit(j):
        return j + sum(jnp.where(j >= first, delta, 0) for first, delta in steps)

    return pl.pallas_call(
        _cast_kernel,
        grid=(n_units,),
        in_specs=[pl.BlockSpec((depth, k, PREP_UNIT), lambda j: (0, 0, src_unit(j)))],
        out_specs=pl.BlockSpec((depth, k, PREP_UNIT), lambda j: (0, 0, j)),
        out_shape=jax.ShapeDtypeStruct((depth, k, n_units * PREP_UNIT), jnp.bfloat16),
        compiler_params=_params("parallel"),
        name="permute_in_weights",
    )(w_in)


def _forget_in_weights(w_in):
    ff = w_in[:, :, QK_COLS + V_COLS + GATE_COLS:]
    return jnp.pad(ff, ((0, 0), (0, 0), (0, LANES - FOX_HEADS))).astype(jnp.bfloat16)


def _qk_col_scale():
    one = lambda n: jnp.ones((n,), jnp.float32)
    full = lambda n, v: jnp.full((n,), v, jnp.float32)
    return jnp.concatenate([
        full(FOX_W, HEAD_DIM ** -0.5 * LOG2E), one(FOX_W),
        full(SB_W, HEAD_DIM ** -0.5 * LOG2E), one(SB_W),
        full(DIFF_W, DIFF_QK_DIM ** -0.5 * LOG2E), one(DIFF_W)])[None, :]


def kernel(x, w_in, b_f, diff_lambda, diff_subln_g, w_out, ln_g, ln_b):
    batch, seq, d_model = x.shape
    depth = w_in.shape[0]
    t = batch * seq
    tk = min(seq, KEY_BLOCK)
    tq = min(seq, ATTN_TQ)

    w_in_b = w_in.astype(jnp.bfloat16)
    w_perm = _permuted_in_weights(w_in_b)
    w_ff = _forget_in_weights(w_in_b)
    w_out_b = w_out.astype(jnp.bfloat16)
    qk_scale = _qk_col_scale()
    b_f_rows = jnp.pad(b_f, ((0, 0), (0, LANES - FOX_HEADS)))
    ln_g3 = ln_g.reshape(depth, 1, d_model)
    ln_b3 = ln_b.reshape(depth, 1, d_model)

    xf = x.reshape(t, d_model)
    xb = xf
    for l in range(depth):
        lam_init = 0.8 - 0.6 * math.exp(-0.3 * l)
        gate_rows = GATE_ROWS_F32 if xb.dtype == jnp.float32 else GATE_ROWS_BF16
        gates, c, xb = _project_gates(xb, w_perm, w_ff, b_f_rows[l:l + 1], l, QK_COLS + V_COLS,
                                      min(seq, gate_rows), seq, tk)
        c = c.reshape(batch, seq, LANES)
        tm = min(t, PROJ_ROWS)
        qk = _project(xb, w_perm, l, 0, QK_COLS, qk_scale, jnp.bfloat16, tm, PROJ_COLS, "qk_projection")
        vt = _project(xb, w_perm, l, QK_COLS, V_COLS, None, jnp.bfloat16, tm, PROJ_COLS, "v_projection",
                      transpose_out=True)
        o_fox = _fox_attention(qk, vt, gates, c, batch, seq, tq, tk, FOX_GROUP)
        o_sb = _sb_attention(qk, vt, gates, batch, seq, tq, tk, SB_GROUP)
        o_diff = _diff_attention(qk, vt, gates, diff_lambda[l], diff_subln_g[l:l + 1], batch, seq, tq, tk,
                                 lam_init, DIFF_GROUP)
        xf, xb = _merge(o_fox, o_sb, o_diff, w_out_b, l, xf, ln_g3, ln_b3, min(t, MERGE_ROWS))
    return xf.reshape(batch, seq, d_model)
```
